```python
import math
import jax, jax.numpy as jnp
from jax import lax
import numpy as np

D_MODEL = 1024
BATCH = 16
SEQ = 2048
DEPTH = 2

N_MIXERS = 2
EPS = 1e-6

NSA_HEADS = 16
NSA_KV_GROUPS = 4
NSA_HEAD_DIM = 64
NSA_Q_DIM = NSA_HEADS * NSA_HEAD_DIM
NSA_KV_DIM = NSA_KV_GROUPS * NSA_HEAD_DIM
NSA_IN_COLS = NSA_Q_DIM + 6 * NSA_KV_DIM + 3 * NSA_HEADS
CMP_BLOCK = 32
CMP_STRIDE = 16
CMP_HIDDEN = 256
SEL_BLOCK = 64
SEL_TOP_N = 16
WINDOW = 512
Q_BLOCK = 128
SEL_Q_CHUNK = 16

REL_BUCKETS = 32
REL_MAX_DIST = 128

SSD_EXPAND = 2
SSD_D_INNER = SSD_EXPAND * D_MODEL
SSD_HEAD_DIM = 64
SSD_HEADS = SSD_D_INNER // SSD_HEAD_DIM
SSD_GROUPS = 8
SSD_STATE = 128
SSD_CONV = 4
SSD_CHUNK = 128
SSD_CONV_DIM = SSD_D_INNER + 2 * SSD_GROUPS * SSD_STATE
SSD_IN_COLS = SSD_D_INNER + SSD_CONV_DIM + SSD_HEADS
DT_MIN = 1e-3
DT_MAX = 1e-1

MLP_HIDDEN = 4 * D_MODEL

N_NSA_LAYERS = (DEPTH + 1) // 2
N_SSD_LAYERS = DEPTH // 2

kernel_name = 'nsa_mamba2_interleaved_trunk'


def rmsnorm(x, w):
    x32 = x.astype(jnp.float32)
    y = x32 * lax.rsqrt(jnp.mean(x32 * x32, axis=-1, keepdims=True) + EPS)
    return (y * w.astype(jnp.float32)).astype(x.dtype)


def rel_bucket(dist):
    max_exact = REL_BUCKETS // 2
    d = jnp.maximum(dist, 0)
    df = jnp.maximum(d, 1).astype(jnp.float32)
    large = max_exact + (jnp.log(df / max_exact) / math.log(REL_MAX_DIST / max_exact)
                         * (REL_BUCKETS - max_exact)).astype(jnp.int32)
    large = jnp.minimum(large, REL_BUCKETS - 1)
    return jnp.where(d < max_exact, d, large)


def masked_softmax(s, mask, axis=-1):
    s = jnp.where(mask, s, -jnp.inf)
    m = jnp.max(s, axis=axis, keepdims=True)
    m = jnp.where(jnp.isfinite(m), m, 0.0)
    e = jnp.exp(s - m)
    den = jnp.sum(e, axis=axis, keepdims=True)
    return e / jnp.where(den > 0, den, 1.0)


def compress_blocks(t, idx, pe, w1, w2):
    blocks = t[:, idx] + pe[:, None, :]
    hid = jax.nn.silu(jnp.einsum('bnlgd,lde->bnge', blocks, w1))
    return jnp.einsum('bnge,ed->bngd', hid, w2)


def nsa_mixer(h, w_in, q_gain, k_gain, pe_k, w1_k, w2_k, pe_v, w1_v, w2_v, rel_table, w_out):
    B, S, _ = h.shape
    G, Dh = NSA_KV_GROUPS, NSA_HEAD_DIM
    R = NSA_HEADS // G
    dt = h.dtype
    proj = h @ w_in
    q = rmsnorm(proj[..., :NSA_Q_DIM].reshape(B, S, G, R, Dh), q_gain) * (Dh ** -0.5)
    kv = proj[..., NSA_Q_DIM:NSA_Q_DIM + 6 * NSA_KV_DIM].reshape(B, S, 6, G, Dh)
    k_c, v_c, k_s, v_s, k_w, v_w = (kv[:, :, i] for i in range(6))
    gates = jax.nn.sigmoid(proj[..., NSA_Q_DIM + 6 * NSA_KV_DIM:]).reshape(B, S, 3, G, R)
    table = rel_table.reshape(REL_BUCKETS, G, R)

    n_cmp = (S - CMP_BLOCK) // CMP_STRIDE + 1
    cmp_start = jnp.arange(n_cmp) * CMP_STRIDE
    cmp_idx = cmp_start[:, None] + jnp.arange(CMP_BLOCK)[None, :]
    cmp_end = cmp_start + CMP_BLOCK - 1
    k_cmp = rmsnorm(compress_blocks(k_c, cmp_idx, pe_k, w1_k, w2_k), k_gain)
    v_cmp = compress_blocks(v_c, cmp_idx, pe_v, w1_v, w2_v)

    n_sel = S // SEL_BLOCK
    sel_start = jnp.arange(n_sel) * SEL_BLOCK
    overlap = ((cmp_start[:, None] < sel_start[None, :] + SEL_BLOCK)
               & (cmp_start[:, None] + CMP_BLOCK > sel_start[None, :])).astype(jnp.float32)
    top_n = min(SEL_TOP_N, n_sel)
    ks_blocks = rmsnorm(k_s, k_gain).reshape(B, n_sel, SEL_BLOCK, G, Dh).transpose(0, 3, 1, 2, 4)
    vs_blocks = v_s.reshape(B, n_sel, SEL_BLOCK, G, Dh).transpose(0, 3, 1, 2, 4)
    b_idx = jnp.arange(B)[:, None, None, None]
    g_idx = jnp.arange(G)[None, :, None, None]
    g_idx5 = jnp.arange(G)[None, :, None, None, None]
    blk_ids = jnp.arange(n_sel)

    def cmp_sel_chunk(args):
        q_c, t_c = args
        s = jnp.einsum('bqgrd,bngd->bgrqn', q_c, k_cmp).astype(jnp.float32)
        dist = t_c[:, None] - cmp_end[None, :]
        bias = jnp.transpose(table[rel_bucket(dist)], (2, 3, 0, 1))
        p = masked_softmax(s + bias, dist >= 0)
        o_cmp = jnp.einsum('bgrqn,bngd->bqgrd', p.astype(dt), v_cmp)
        imp = jnp.einsum('bgrqn,nj->bgqj', p, overlap)
        cur = t_c // SEL_BLOCK
        blk_valid = sel_start[None, :] <= t_c[:, None]
        forced = ((blk_ids[None, :] == 0) | (blk_ids[None, :] == cur[:, None])
                  | (blk_ids[None, :] == cur[:, None] - 1))
        score = jnp.where(forced, jnp.inf, imp)
        score = jnp.where(blk_valid, score, -jnp.inf)
        _, sel_idx = lax.top_k(score, top_n)
        k_sel = ks_blocks[b_idx, g_idx, sel_idx]
        v_sel = vs_blocks[b_idx, g_idx, sel_idx]
        key_pos = sel_idx[..., None] * SEL_BLOCK + jnp.arange(SEL_BLOCK)
        dist2 = t_c[None, None, :, None, None] - key_pos
        bias2 = jnp.moveaxis(table[rel_bucket(dist2), g_idx5], -1, 2)
        s2 = jnp.einsum('bqgrd,bgqnld->bgrqnl', q_c, k_sel).astype(jnp.float32)
        p2 = masked_softmax(s2 + bias2, (dist2 >= 0)[:, :, None], axis=(-2, -1))
        o_slc = jnp.einsum('bgrqnl,bgqnld->bqgrd', p2.astype(dt), v_sel)
        return o_cmp, o_slc

    nq = S // SEL_Q_CHUNK
    q_chunks = jnp.swapaxes(q.reshape(B, nq, SEL_Q_CHUNK, G, R, Dh), 0, 1)
    t_chunks = jnp.arange(S).reshape(nq, SEL_Q_CHUNK)
    o_cmp, o_slc = lax.map(cmp_sel_chunk, (q_chunks, t_chunks))
    o_cmp = jnp.swapaxes(o_cmp, 0, 1).reshape(B, S, G, R, Dh)
    o_slc = jnp.swapaxes(o_slc, 0, 1).reshape(B, S, G, R, Dh)

    span = WINDOW + Q_BLOCK
    kw_pad = jnp.pad(rmsnorm(k_w, k_gain), ((0, 0), (WINDOW, 0), (0, 0), (0, 0)))
    vw_pad = jnp.pad(v_w, ((0, 0), (WINDOW, 0), (0, 0), (0, 0)))

    def window_block(args):
        q_b, start = args
        k_b = lax.dynamic_slice_in_dim(kw_pad, start, span, axis=1)
        v_b = lax.dynamic_slice_in_dim(vw_pad, start, span, axis=1)
        t_b = start + jnp.arange(Q_BLOCK)
        key_pos = start - WINDOW + jnp.arange(span)
        dist = t_b[:, None] - key_pos[None, :]
        mask = (dist >= 0) & (dist < WINDOW) & (key_pos[None, :] >= 0)
        s = jnp.einsum('bqgrd,bkgd->bgrqk', q_b, k_b).astype(jnp.float32)
        bias = jnp.transpose(table[rel_bucket(dist)], (2, 3, 0, 1))
        p = masked_softmax(s + bias, mask)
        return jnp.einsum('bgrqk,bkgd->bqgrd', p.astype(dt), v_b)

    nb = S // Q_BLOCK
    q_blocks = jnp.swapaxes(q.reshape(B, nb, Q_BLOCK, G, R, Dh), 0, 1)
    o_win = lax.map(window_block, (q_blocks, jnp.arange(nb) * Q_BLOCK))
    o_win = jnp.swapaxes(o_win, 0, 1).reshape(B, S, G, R, Dh)

    o = (gates[:, :, 0][..., None] * o_cmp + gates[:, :, 1][..., None] * o_slc
         + gates[:, :, 2][..., None] * o_win)
    return o.reshape(B, S, NSA_Q_DIM).astype(dt) @ w_out


def ssd_scan(X, A, Bm, Cm):
    b, S, NH, P = X.shape
    G, N = Bm.shape[-2], Bm.shape[-1]
    J = NH // G
    L = SSD_CHUNK
    c = S // L
    X = X.reshape(b, c, L, G, J, P)
    A = jnp.transpose(A.reshape(b, c, L, G, J), (0, 3, 4, 1, 2))
    Bm = Bm.reshape(b, c, L, G, N)
    Cm = Cm.reshape(b, c, L, G, N)
    A_cs = jnp.cumsum(A, axis=-1)
    tril = jnp.tril(jnp.ones((L, L), dtype=bool))
    decay_in = jnp.exp(jnp.where(tril, A_cs[..., :, None] - A_cs[..., None, :], -jnp.inf))
    cb = jnp.einsum('bclgn,bcsgn->bcgls', Cm, Bm)
    y_diag = jnp.einsum('bcgls,bgjcls,bcsgjp->bclgjp', cb, decay_in, X)
    decay_to_end = jnp.exp(A_cs[..., -1:] - A_cs)
    states = jnp.einsum('bcsgn,bgjcs,bcsgjp->bcgjpn', Bm, decay_to_end, X)
    chunk_decay = jnp.exp(A_cs[..., -1])

    def step(carry, inp):
        st, dec = inp
        return carry * dec[..., None, None] + st, carry

    init = jnp.zeros((b, G, J, P, N), states.dtype)
    _, prev = lax.scan(step, init, (jnp.moveaxis(states, 1, 0), jnp.moveaxis(chunk_decay, -1, 0)))
    prev = jnp.moveaxis(prev, 0, 1)
    y_off = jnp.einsum('bclgn,bcgjpn,bgjcl->bclgjp', Cm, prev, jnp.exp(A_cs))
    return (y_diag + y_off).reshape(b, S, NH, P)


def ssd_mixer(h, w_in, conv_w, conv_b, dt_bias, a_log, d_skip, norm_w, w_out):
    B, S, _ = h.shape
    DI, G, N, NH, P = SSD_D_INNER, SSD_GROUPS, SSD_STATE, SSD_HEADS, SSD_HEAD_DIM
    f32 = jnp.float32
    proj = h @ w_in
    z = proj[..., :DI]
    xbc = proj[..., DI:DI + SSD_CONV_DIM]
    dt_raw = proj[..., DI + SSD_CONV_DIM:]
    xbc = lax.conv_general_dilated(xbc, conv_w[:, None, :], window_strides=(1,),
                                   padding=[(SSD_CONV - 1, 0)],
                                   dimension_numbers=('NWC', 'WIO', 'NWC'),
                                   feature_group_count=SSD_CONV_DIM)
    xbc = jax.nn.silu(xbc + conv_b)
    xs = xbc[..., :DI].reshape(B, S, NH, P).astype(f32)
    Bm = xbc[..., DI:DI + G * N].reshape(B, S, G, N).astype(f32)
    Cm = xbc[..., DI + G * N:].reshape(B, S, G, N).astype(f32)
    dt = jax.nn.softplus((dt_raw + dt_bias).astype(f32))
    A = -jnp.exp(a_log.astype(f32))
    y = ssd_scan(xs * dt[..., None], dt * A, Bm, Cm)
    y = y + xs * d_skip.astype(f32)[:, None]
    y = y.reshape(B, S, DI) * jax.nn.silu(z.astype(f32))
    y = rmsnorm(y.reshape(B, S, G, DI // G), norm_w.reshape(G, DI // G)).reshape(B, S, DI)
    return y.astype(h.dtype) @ w_out


def sqrelu_mlp(h, w_up, w_down):
    return jnp.square(jax.nn.relu(h @ w_up)) @ w_down


def setup_inputs(seed: int = 0) -> dict:
    key = jax.random.key(seed)
    ks = jax.random.split(key, 24)
    f32 = jnp.float32

    def nrm(k, shape, scale):
        return scale * jax.random.normal(k, shape, f32)

    na, nb = N_NSA_LAYERS, N_SSD_LAYERS
    x = jax.random.normal(ks[0], (BATCH, SEQ, D_MODEL), f32)
    norm_mix_w = 1.0 + nrm(ks[1], (DEPTH, D_MODEL), 0.02)
    norm_mlp_w = 1.0 + nrm(ks[2], (DEPTH, D_MODEL), 0.02)
    rel_table = nrm(ks[3], (REL_BUCKETS, NSA_HEADS), 0.2)
    nsa_w_in = nrm(ks[4], (na, D_MODEL, NSA_IN_COLS), D_MODEL ** -0.5)
    nsa_q_gain = 1.0 + nrm(ks[5], (na, NSA_HEAD_DIM), 0.02)
    nsa_k_gain = 1.0 + nrm(ks[6], (na, NSA_HEAD_DIM), 0.02)
    cmp_pe_k = nrm(ks[7], (na, CMP_BLOCK, NSA_HEAD_DIM), 0.1)
    cmp_w1_k = nrm(ks[8], (na, CMP_BLOCK, NSA_HEAD_DIM, CMP_HIDDEN), (CMP_BLOCK * NSA_HEAD_DIM) ** -0.5)
    cmp_w2_k = nrm(ks[9], (na, CMP_HIDDEN, NSA_HEAD_DIM), CMP_HIDDEN ** -0.5)
    cmp_pe_v = nrm(ks[10], (na, CMP_BLOCK, NSA_HEAD_DIM), 0.1)
    cmp_w1_v = nrm(ks[11], (na, CMP_BLOCK, NSA_HEAD_DIM, CMP_HIDDEN), (CMP_BLOCK * NSA_HEAD_DIM) ** -0.5)
    cmp_w2_v = nrm(ks[12], (na, CMP_HIDDEN, NSA_HEAD_DIM), CMP_HIDDEN ** -0.5)
    nsa_w_out = nrm(ks[13], (na, NSA_Q_DIM, D_MODEL), NSA_Q_DIM ** -0.5)
    ssd_w_in = nrm(ks[14], (nb, D_MODEL, SSD_IN_COLS), D_MODEL ** -0.5)
    ssd_conv_w = nrm(ks[15], (nb, SSD_CONV, SSD_CONV_DIM), SSD_CONV ** -0.5)
    ssd_conv_b = nrm(ks[16], (nb, SSD_CONV_DIM), 0.02)
    dt0 = jnp.exp(jax.random.uniform(ks[17], (nb, SSD_HEADS), f32)
                  * (math.log(DT_MAX) - math.log(DT_MIN)) + math.log(DT_MIN))
    ssd_dt_bias = dt0 + jnp.log(-jnp.expm1(-dt0))
    ssd_a_log = jnp.log(jax.random.uniform(ks[18], (nb, SSD_HEADS), f32, 1.0, 16.0))
    ssd_d = 1.0 + nrm(ks[19], (nb, SSD_HEADS), 0.1)
    ssd_norm_w = 1.0 + nrm(ks[20], (nb, SSD_D_INNER), 0.02)
    ssd_w_out = nrm(ks[21], (nb, SSD_D_INNER, D_MODEL), SSD_D_INNER ** -0.5)
    mlp_w_up = nrm(ks[22], (DEPTH, D_MODEL, MLP_HIDDEN), D_MODEL ** -0.5)
    mlp_w_down = nrm(ks[23], (DEPTH, MLP_HIDDEN, D_MODEL), MLP_HIDDEN ** -0.5)
    return {'x': x, 'norm_mix_w': norm_mix_w, 'norm_mlp_w': norm_mlp_w, 'rel_table': rel_table,
            'nsa_w_in': nsa_w_in, 'nsa_q_gain': nsa_q_gain, 'nsa_k_gain': nsa_k_gain,
            'cmp_pe_k': cmp_pe_k, 'cmp_w1_k': cmp_w1_k, 'cmp_w2_k': cmp_w2_k,
            'cmp_pe_v': cmp_pe_v, 'cmp_w1_v': cmp_w1_v, 'cmp_w2_v': cmp_w2_v,
            'nsa_w_out': nsa_w_out, 'ssd_w_in': ssd_w_in, 'ssd_conv_w': ssd_conv_w,
            'ssd_conv_b': ssd_conv_b, 'ssd_dt_bias': ssd_dt_bias, 'ssd_a_log': ssd_a_log,
            'ssd_d': ssd_d, 'ssd_norm_w': ssd_norm_w, 'ssd_w_out': ssd_w_out,
            'mlp_w_up': mlp_w_up, 'mlp_w_down': mlp_w_down}


def reference(x, norm_mix_w, norm_mlp_w, rel_table, nsa_w_in, nsa_q_gain, nsa_k_gain,
              cmp_pe_k, cmp_w1_k, cmp_w2_k, cmp_pe_v, cmp_w1_v, cmp_w2_v, nsa_w_out,
              ssd_w_in, ssd_conv_w, ssd_conv_b, ssd_dt_bias, ssd_a_log, ssd_d, ssd_norm_w,
              ssd_w_out, mlp_w_up, mlp_w_down):
    for i in range(DEPTH):
        hn = rmsnorm(x, norm_mix_w[i])
        li = i // N_MIXERS
        if i % N_MIXERS == 0:
            x = x + nsa_mixer(hn, nsa_w_in[li], nsa_q_gain[li], nsa_k_gain[li],
                              cmp_pe_k[li], cmp_w1_k[li], cmp_w2_k[li],
                              cmp_pe_v[li], cmp_w1_v[li], cmp_w2_v[li],
                              rel_table, nsa_w_out[li])
        else:
            x = x + ssd_mixer(hn, ssd_w_in[li], ssd_conv_w[li], ssd_conv_b[li],
                              ssd_dt_bias[li], ssd_a_log[li], ssd_d[li], ssd_norm_w[li],
                              ssd_w_out[li])
        x = x + sqrelu_mlp(rmsnorm(x, norm_mlp_w[i]), mlp_w_up[i], mlp_w_down[i])
    return x
```

```python
import functools
import math

import numpy as np
import jax
import jax.numpy as jnp
from jax import lax
from jax.experimental import pallas as pl
from jax.experimental.pallas import tpu as pltpu

F32 = jnp.float32
BF16 = jnp.bfloat16

D_MODEL = 1024
EPS = 1e-6

NSA_HEADS = 16
NSA_GROUPS = 4
NSA_REP = NSA_HEADS // NSA_GROUPS
HEAD_DIM = 64
NSA_Q_DIM = NSA_HEADS * HEAD_DIM
NSA_KV_DIM = NSA_GROUPS * HEAD_DIM
CMP_BLOCK = 32
CMP_STRIDE = 16
CMP_HIDDEN = 256
SEL_BLOCK = 64
SEL_TOP_N = 16
WINDOW = 512
REL_BUCKETS = 32
REL_MAX_DIST = 128

SSD_D_INNER = 2 * D_MODEL
SSD_HEAD_DIM = 64
SSD_HEADS = SSD_D_INNER // SSD_HEAD_DIM
SSD_GROUPS = 8
SSD_STATE = 128
SSD_CONV = 4
SSD_CHUNK = 128
SSD_CONV_DIM = SSD_D_INNER + 2 * SSD_GROUPS * SSD_STATE
MLP_HIDDEN = 4 * D_MODEL

LANES = 128
SUBLANES = 8
VMEM_LIMIT_BYTES = 56 * 1024 * 1024

TQ = 128
TK_SEL = 256
WIN_SPAN = WINDOW + TQ
N_CMP_PAD = 128
MASKED = -1e30
SCORE_BIG = 3e38

NT_DIMS = (((1,), (1,)), ((), ()))


def _dot(a, b):
    return jnp.dot(a, b, preferred_element_type=F32)


def _dot_nt(a, b):
    return lax.dot_general(a, b, NT_DIMS, preferred_element_type=F32)


def _split3(x):
    hi = x.astype(BF16)
    r1 = x - hi.astype(F32)
    mid = r1.astype(BF16)
    lo = (r1 - mid.astype(F32)).astype(BF16)
    return hi, mid, lo


def _rms_rows(x, w):
    return x * lax.rsqrt(jnp.mean(x * x, axis=-1, keepdims=True) + EPS) * w


def _const_spec(shape):
    nd = len(shape)
    return pl.BlockSpec(shape, lambda *_: (0,) * nd, pipeline_mode=pl.Buffered(1))


def _compiler_params(semantics):
    return pltpu.CompilerParams(dimension_semantics=semantics, vmem_limit_bytes=VMEM_LIMIT_BYTES)


def _nsa_proj_kernel(x_ref, nw_ref, wq_ref, wkv_ref, wg_ref, qg_ref, kg_ref, seg_ref,
                     q_ref, kc_ref, vc_ref, ks_ref, vs_ref, kw_ref, vw_ref, gate_ref):
    h = _rms_rows(x_ref[0], nw_ref[...]).astype(BF16)
    seg = seg_ref[...]

    def head_rms(y, gain):
        sq = y * y
        hi = sq.astype(BF16)
        lo = (sq - hi.astype(F32)).astype(BF16)
        ss = _dot(hi, seg) + _dot(lo, seg)
        return y * lax.rsqrt(ss * (1.0 / HEAD_DIM) + EPS) * gain

    for c in range(NSA_Q_DIM // 256):
        y = _dot(h, wq_ref[c])
        y = (head_rms(y, qg_ref[...]) * (HEAD_DIM ** -0.5)).astype(BF16)
        for r in range(4):
            q_ref[0, 4 * c + r] = y[:, 64 * r:64 * r + 64]

    kv_out = (kc_ref, vc_ref, ks_ref, vs_ref, kw_ref, vw_ref)
    for c in range(6):
        y = _dot(h, wkv_ref[c])
        if c in (2, 4):
            y = head_rms(y, kg_ref[...])
        y = y.astype(kv_out[c].dtype)
        for g in range(NSA_GROUPS):
            kv_out[c][0, g] = y[:, 64 * g:64 * g + 64]

    gl = jax.nn.sigmoid(_dot(h, wg_ref[...]))
    for g in range(NSA_GROUPS):
        gate_ref[0, g] = gl[:, 16 * g:16 * g + 16]


def _nsa_proj(x, nw, wq, wkv, wg, qg, kg, seg, tm):
    B, S, D = x.shape
    grid = (B, S // tm)
    kv_block = pl.BlockSpec((1, NSA_GROUPS, tm, HEAD_DIM), lambda b, i: (b, 0, i, 0))
    kv_bf = jax.ShapeDtypeStruct((B, NSA_GROUPS, S, HEAD_DIM), BF16)
    kv_f32 = jax.ShapeDtypeStruct((B, NSA_GROUPS, S, HEAD_DIM), F32)
    return pl.pallas_call(
        _nsa_proj_kernel,
        grid=grid,
        in_specs=[
            pl.BlockSpec((1, tm, D), lambda b, i: (b, i, 0)),
            _const_spec(nw.shape), _const_spec(wq.shape), _const_spec(wkv.shape),
            _const_spec(wg.shape), _const_spec(qg.shape), _const_spec(kg.shape), _const_spec(seg.shape),
        ],
        out_specs=[
            pl.BlockSpec((1, NSA_HEADS, tm, HEAD_DIM), lambda b, i: (b, 0, i, 0)),
            kv_block, kv_block, kv_block, kv_block, kv_block, kv_block,
            pl.BlockSpec((1, NSA_GROUPS, tm, 16), lambda b, i: (b, 0, i, 0)),
        ],
        out_shape=[
            jax.ShapeDtypeStruct((B, NSA_HEADS, S, HEAD_DIM), BF16),
            kv_f32, kv_f32, kv_bf, kv_bf, kv_bf, kv_bf,
            jax.ShapeDtypeStruct((B, NSA_GROUPS, S, 16), F32),
        ],
        compiler_params=_compiler_params(("parallel", "parallel")),
        name="nsa_proj",
    )(x, nw, wq, wkv, wg, qg, kg, seg)


def _nsa_compress_kernel(kc_ref, vc_ref, pek_ref, pev_ref, w1k_ref, w1v_ref, w2k_ref, w2v_ref, kg_ref,
                         kcmp_ref, vcmp_ref):
    half = CMP_BLOCK // 2

    def compress(src_ref, pe_ref, w1_ref, w2_ref):
        acc_a = jnp.zeros((N_CMP_PAD, CMP_HIDDEN), F32)
        acc_b = jnp.zeros((N_CMP_PAD, CMP_HIDDEN), F32)
        for l in range(half):
            rows = src_ref[0, 0, pl.ds(l, N_CMP_PAD, stride=CMP_STRIDE), :]
            acc_a += _dot((rows + pe_ref[l:l + 1, :]).astype(BF16), w1_ref[l])
            acc_b += _dot((rows + pe_ref[half + l:half + l + 1, :]).astype(BF16), w1_ref[half + l])
        hid = acc_a + pltpu.roll(acc_b, N_CMP_PAD - 1, axis=0)
        hid = hid * jax.nn.sigmoid(hid)
        return _dot(hid.astype(BF16), w2_ref[...])

    k = compress(kc_ref, pek_ref, w1k_ref, w2k_ref)
    kcmp_ref[0, 0] = _rms_rows(k, kg_ref[...]).astype(BF16)
    vcmp_ref[0, 0] = compress(vc_ref, pev_ref, w1v_ref, w2v_ref).astype(BF16)


def _nsa_compress(kc, vc, pek, pev, w1k, w1v, w2k, w2v, kg):
    B, G, S, Dh = kc.shape
    src = pl.BlockSpec((1, 1, S, Dh), lambda b, g: (b, g, 0, 0))
    out = pl.BlockSpec((1, 1, N_CMP_PAD, Dh), lambda b, g: (b, g, 0, 0))
    out_shape = jax.ShapeDtypeStruct((B, G, N_CMP_PAD, Dh), BF16)
    consts = (pek, pev, w1k, w1v, w2k, w2v, kg)
    return pl.pallas_call(
        _nsa_compress_kernel,
        grid=(B, G),
        in_specs=[src, src] + [_const_spec(c.shape) for c in consts],
        out_specs=[out, out],
        out_shape=[out_shape, out_shape],
        compiler_params=_compiler_params(("parallel", "parallel")),
        name="nsa_compress",
    )(kc, vc, *consts)


def _nsa_attn_kernel(q_ref, gate_ref, kcmp_ref, vcmp_ref, ks_ref, vs_ref, kw_ref, vw_ref,
                     bcmp_ref, bsel_ref, bwin_ref, ovl_ref, eye_ref, exp_ref,
                     o_ref, selmask_ref, m_ref, l_ref, acc_ref):
    i = pl.program_id(2)
    rows = NSA_REP * TQ
    q = q_ref[0].reshape(rows, HEAD_DIM)

    s = _dot_nt(q, kcmp_ref[0, 0]) + bcmp_ref[...].reshape(rows, N_CMP_PAD)
    m = jnp.max(s, axis=-1, keepdims=True)
    m = jnp.where(m > 0.5 * MASKED, m, 0.0)
    e = jnp.exp(s - m)
    den = jnp.sum(e, axis=-1, keepdims=True)
    p = e / jnp.where(den > 0.0, den, 1.0)
    o_cmp = _dot(p.astype(BF16), vcmp_ref[0, 0])

    psum = p[0:TQ] + p[TQ:2 * TQ] + p[2 * TQ:3 * TQ] + p[3 * TQ:4 * TQ]
    ph, pm, plo = _split3(psum)
    ovl = ovl_ref[...]
    imp = _dot_nt(ovl, ph) + _dot_nt(ovl, pm) + _dot_nt(ovl, plo)
    n_sel = 32
    imp = imp[0:n_sel]
    tok = i * TQ + lax.broadcasted_iota(jnp.int32, (n_sel, TQ), 1)
    blk = lax.broadcasted_iota(jnp.int32, (n_sel, TQ), 0)
    cur = tok // SEL_BLOCK
    forced = (blk == 0) | (blk == cur) | (blk == cur - 1)
    valid = blk * SEL_BLOCK <= tok
    score = jnp.where(forced, SCORE_BIG, imp)
    score = jnp.where(valid, score, -SCORE_BIG)
    rank = jnp.zeros((n_sel, TQ), F32)
    for jp in range(n_sel):
        other = score[jp:jp + 1, :]
        beats = (other > score) | ((other == score) & (blk > jp))
        rank += beats.astype(F32)
    sel_t = (rank < SEL_TOP_N).astype(BF16)
    sel_t = jnp.concatenate([sel_t, jnp.zeros((LANES - n_sel, TQ), BF16)], axis=0)
    sel = _dot_nt(eye_ref[...], sel_t).astype(BF16)
    for c in range(selmask_ref.shape[0]):
        hit = _dot(sel, exp_ref[:, c * TK_SEL:(c + 1) * TK_SEL])
        selmask_ref[c] = (hit - 1.0) * (-MASKED)

    m_ref[...] = jnp.full(m_ref.shape, MASKED, F32)
    l_ref[...] = jnp.zeros(l_ref.shape, F32)
    acc_ref[...] = jnp.zeros(acc_ref.shape, F32)

    def sel_tile(c, carry):
        k0 = pl.multiple_of(c * TK_SEL, TK_SEL)
        k = ks_ref[0, 0, pl.ds(k0, TK_SEL), :]
        v = vs_ref[0, 0, pl.ds(k0, TK_SEL), :]
        variant = jnp.minimum(i - 2 * c, 3)
        s = _dot_nt(q, k).reshape(NSA_REP, TQ, TK_SEL)
        s = s + bsel_ref[variant] + selmask_ref[c][None]
        s = s.reshape(rows, TK_SEL)
        m_old = m_ref[...]
        m_new = jnp.maximum(m_old, jnp.max(s, axis=-1, keepdims=True))
        alpha = jnp.exp(m_old - m_new)
        pt = jnp.exp(s - m_new)
        l_ref[...] = alpha * l_ref[...] + jnp.sum(pt, axis=-1, keepdims=True)
        acc_ref[...] = alpha * acc_ref[...] + _dot(pt.astype(BF16), v)
        m_ref[...] = m_new
        return carry

    lax.fori_loop(0, i // 2 + 1, sel_tile, 0)
    o_sel = acc_ref[...] / l_ref[...]

    w0 = pl.multiple_of(jnp.maximum(i - WINDOW // TQ, 0) * TQ, TQ)
    kw = kw_ref[0, 0, pl.ds(w0, WIN_SPAN), :]
    vw = vw_ref[0, 0, pl.ds(w0, WIN_SPAN), :]
    s = _dot_nt(q, kw) + bwin_ref[0].reshape(rows, WIN_SPAN)
    m = jnp.max(s, axis=-1, keepdims=True)
    e = jnp.exp(s - m)
    o_win = _dot(e.astype(BF16), vw) / jnp.sum(e, axis=-1, keepdims=True)

    gate = gate_ref[0, 0]
    for r in range(NSA_REP):
        sl = slice(r * TQ, (r + 1) * TQ)
        o_r = (gate[:, r:r + 1] * o_cmp[sl] + gate[:, 4 + r:5 + r] * o_sel[sl]
               + gate[:, 8 + r:9 + r] * o_win[sl])
        o_ref[0, :, r * HEAD_DIM:(r + 1) * HEAD_DIM] = o_r.astype(BF16)


def _nsa_attn(q, gates, kcmp, vcmp, ks, vs, kw, vw, bcmp, bsel, bwin, ovl, eye, expand):
    B, H, S, Dh = q.shape
    G = NSA_GROUPS
    rows = NSA_REP * TQ
    full_kv = pl.BlockSpec((1, 1, S, Dh), lambda g, b, i: (b, g, 0, 0))
    cmp_kv = pl.BlockSpec((1, 1, N_CMP_PAD, Dh), lambda g, b, i: (b, g, 0, 0))
    n_var = WINDOW // TQ
    return pl.pallas_call(
        _nsa_attn_kernel,
        grid=(G, B, S // TQ),
        in_specs=[
            pl.BlockSpec((1, NSA_REP, TQ, Dh), lambda g, b, i: (b, g, i, 0)),
            pl.BlockSpec((1, 1, TQ, 16), lambda g, b, i: (b, g, i, 0)),
            cmp_kv, cmp_kv, full_kv, full_kv, full_kv, full_kv,
            pl.BlockSpec((NSA_REP, TQ, N_CMP_PAD), lambda g, b, i: (g, i, 0)),
            pl.BlockSpec((4, NSA_REP, TQ, TK_SEL), lambda g, b, i: (0, g, 0, 0)),
            pl.BlockSpec((1, NSA_REP, TQ, WIN_SPAN), lambda g, b, i: (jnp.minimum(i, n_var), g, 0, 0)),
            _const_spec(ovl.shape), _const_spec(eye.shape), _const_spec(expand.shape),
        ],
        out_specs=pl.BlockSpec((1, TQ, NSA_REP * Dh), lambda g, b, i: (b, i, g)),
        out_shape=jax.ShapeDtypeStruct((B, S, H * Dh), BF16),
        scratch_shapes=[
            pltpu.VMEM((S // TK_SEL, TQ, TK_SEL), F32),
            pltpu.VMEM((rows, 1), F32),
            pltpu.VMEM((rows, 1), F32),
            pltpu.VMEM((rows, Dh), F32),
        ],
        compiler_params=_compiler_params(("parallel", "parallel", "arbitrary")),
        name="nsa_attn",
    )(q, gates, kcmp, vcmp, ks, vs, kw, vw, bcmp, bsel, bwin, ovl, eye, expand)


def _mix_out_mlp_kernel(x_ref, a_ref, wo_ref, nw_ref, wup_ref, wdn_ref, o_ref):
    x1 = x_ref[...] + _dot(a_ref[...], wo_ref[...])
    h = _rms_rows(x1, nw_ref[...]).astype(BF16)

    def hidden_chunk(c, acc):
        u = jnp.maximum(_dot(h, wup_ref[c]), 0.0)
        return acc + _dot((u * u).astype(BF16), wdn_ref[c])

    acc = lax.fori_loop(0, wup_ref.shape[0], hidden_chunk, jnp.zeros(x1.shape, F32))
    o_ref[...] = x1 + acc


def _mix_out_mlp(x, a, wo, nw, wup, wdn, tm):
    T, D = x.shape
    K = a.shape[1]
    return pl.pallas_call(
        _mix_out_mlp_kernel,
        grid=(T // tm,),
        in_specs=[
            pl.BlockSpec((tm, D), lambda i: (i, 0)),
            pl.BlockSpec((tm, K), lambda i: (i, 0)),
            _const_spec(wo.shape), _const_spec(nw.shape), _const_spec(wup.shape), _const_spec(wdn.shape),
        ],
        out_specs=pl.BlockSpec((tm, D), lambda i: (i, 0)),
        out_shape=jax.ShapeDtypeStruct((T, D), F32),
        compiler_params=_compiler_params(("parallel",)),
        name="mix_out_mlp",
    )(x, a, wo, nw, wup, wdn)


SSD_NCHUNK = 512


def _ssd_proj_kernel(x_ref, nw_ref, wz_ref, wx_ref, wdt_ref, cw_ref, cb_ref, dtb_ref,
                     z_ref, xbc_ref, dt_ref, buf_ref):
    i = pl.program_id(1)
    tm = x_ref.shape[1]
    h = _rms_rows(x_ref[0], nw_ref[...]).astype(BF16)

    for c in range(wz_ref.shape[0]):
        z_ref[0, :, c * SSD_NCHUNK:(c + 1) * SSD_NCHUNK] = _dot(h, wz_ref[c])

    @pl.when(i == 0)
    def _():
        buf_ref[0:SUBLANES, :] = jnp.zeros((SUBLANES, buf_ref.shape[1]), F32)

    @pl.when(i > 0)
    def _():
        buf_ref[0:SUBLANES, :] = buf_ref[tm:tm + SUBLANES, :]

    for c in range(wx_ref.shape[0]):
        cols = slice(c * SSD_NCHUNK, (c + 1) * SSD_NCHUNK)
        buf_ref[SUBLANES:SUBLANES + tm, cols] = _dot(h, wx_ref[c])
        acc = cb_ref[:, cols] + buf_ref[SUBLANES:SUBLANES + tm, cols] * cw_ref[SSD_CONV - 1:SSD_CONV, cols]
        for k in range(SSD_CONV - 1):
            off = SUBLANES - (SSD_CONV - 1) + k
            acc += buf_ref[off:off + tm, cols] * cw_ref[k:k + 1, cols]
        xbc_ref[0, :, cols] = acc * jax.nn.sigmoid(acc)

    dt_ref[0] = jax.nn.softplus(_dot(h, wdt_ref[...]) + dtb_ref[...])


def _ssd_proj(x, nw, wz, wx, wdt, cw, cb, dtb, tm):
    B, S, D = x.shape
    consts = (nw, wz, wx, wdt, cw, cb, dtb)
    return pl.pallas_call(
        _ssd_proj_kernel,
        grid=(B, S // tm),
        in_specs=[pl.BlockSpec((1, tm, D), lambda b, i: (b, i, 0))] + [_const_spec(c.shape) for c in consts],
        out_specs=[
            pl.BlockSpec((1, tm, SSD_D_INNER), lambda b, i: (b, i, 0)),
            pl.BlockSpec((1, tm, SSD_CONV_DIM), lambda b, i: (b, i, 0)),
            pl.BlockSpec((1, tm, LANES), lambda b, i: (b, i, 0)),
        ],
        out_shape=[
            jax.ShapeDtypeStruct((B, S, SSD_D_INNER), F32),
            jax.ShapeDtypeStruct((B, S, SSD_CONV_DIM), F32),
            jax.ShapeDtypeStruct((B, S, LANES), F32),
        ],
        scratch_shapes=[pltpu.VMEM((tm + SUBLANES, SSD_CONV_DIM), F32)],
        compiler_params=_compiler_params(("parallel", "arbitrary")),
        name="ssd_proj",
    )(x, *consts)


def _ssd_scan_kernel(xbc_ref, dt_ref, z_ref, alog_ref, dskip_ref, nw_ref, tril_ref,
                     y_ref, state_ref):
    L = SSD_CHUNK
    P = SSD_HEAD_DIM
    N = SSD_STATE
    heads_per_group = SSD_HEADS // SSD_GROUPS

    @pl.when(pl.program_id(1) == 0)
    def _():
        state_ref[...] = jnp.zeros(state_ref.shape, F32)

    dt = dt_ref[0]
    a = dt * (-jnp.exp(alog_ref[...]))
    tril = tril_ref[...]
    ah, am, al = _split3(a)
    acs = _dot(tril, ah) + _dot(tril, am) + _dot(tril, al)
    acs_t = acs.T
    e_acs = jnp.exp(acs)
    e_last = e_acs[L - 1:L, :]
    to_end_t = jnp.exp(acs_t[:, L - 1:L] - acs_t)
    causal = (lax.broadcasted_iota(jnp.int32, (L, L), 0) >= lax.broadcasted_iota(jnp.int32, (L, L), 1))

    for g in range(SSD_GROUPS):
        b_off = SSD_D_INNER + g * N
        c_off = SSD_D_INNER + SSD_GROUPS * N + g * N
        bm = xbc_ref[0, :, b_off:b_off + N]
        cm = xbc_ref[0, :, c_off:c_off + N].astype(BF16)
        cb = _dot_nt(cm, bm.astype(BF16))
        bm_t = bm.T
        ys = []
        for j in range(heads_per_group):
            hd = g * heads_per_group + j
            x = xbc_ref[0, :, hd * P:(hd + 1) * P]
            xd = (x * dt[:, hd:hd + 1]).astype(BF16)
            diff = acs[:, hd:hd + 1] - acs_t[hd:hd + 1, :]
            decay = jnp.exp(jnp.where(causal, diff, -jnp.inf))
            y = _dot((cb * decay).astype(BF16), xd)
            st = state_ref[hd]
            y = y + _dot(cm, st.astype(BF16)) * e_acs[:, hd:hd + 1]
            state_ref[hd] = st * e_last[:, hd:hd + 1] + _dot((bm_t * to_end_t[hd:hd + 1, :]).astype(BF16), xd)
            zh = z_ref[0, :, hd * P:(hd + 1) * P]
            ys.append((y + x * dskip_ref[:, hd:hd + 1]) * (zh * jax.nn.sigmoid(zh)))
        ss = sum(jnp.sum(yh * yh, axis=-1, keepdims=True) for yh in ys)
        scale = lax.rsqrt(ss * (1.0 / (heads_per_group * P)) + EPS)
        for j, yh in enumerate(ys):
            cols = slice((g * heads_per_group + j) * P, (g * heads_per_group + j + 1) * P)
            y_ref[0, :, cols] = (yh * scale * nw_ref[:, cols]).astype(BF16)


def _ssd_scan(xbc, dt, z, alog, dskip, nw, tril):
    B, S, _ = xbc.shape
    L = SSD_CHUNK
    consts = (alog, dskip, nw, tril)
    return pl.pallas_call(
        _ssd_scan_kernel,
        grid=(B, S // L),
        in_specs=[
            pl.BlockSpec((1, L, SSD_CONV_DIM), lambda b, c: (b, c, 0)),
            pl.BlockSpec((1, L, LANES), lambda b, c: (b, c, 0)),
            pl.BlockSpec((1, L, SSD_D_INNER), lambda b, c: (b, c, 0)),
        ] + [_const_spec(c.shape) for c in consts],
        out_specs=pl.BlockSpec((1, L, SSD_D_INNER), lambda b, c: (b, c, 0)),
        out_shape=jax.ShapeDtypeStruct((B, S, SSD_D_INNER), BF16),
        scratch_shapes=[pltpu.VMEM((SSD_HEADS, SSD_STATE, SSD_HEAD_DIM), F32)],
        compiler_params=_compiler_params(("parallel", "arbitrary")),
        name="ssd_scan",
    )(xbc, dt, z, *consts)


def _rel_bucket_np(dist):
    max_exact = REL_BUCKETS // 2
    d = np.maximum(dist, 0)
    df = np.maximum(d, 1).astype(np.float32)
    large = max_exact + (np.log(df / np.float32(max_exact)) / np.float32(math.log(REL_MAX_DIST / max_exact))
                         * np.float32(REL_BUCKETS - max_exact)).astype(np.int32)
    large = np.minimum(large, REL_BUCKETS - 1)
    return np.where(d < max_exact, d, large).astype(np.int32)


@functools.lru_cache(maxsize=None)
def _position_tables(S):
    n_cmp = (S - CMP_BLOCK) // CMP_STRIDE + 1
    n_sel = S // SEL_BLOCK
    t = np.arange(S)[:, None]
    n = np.arange(N_CMP_PAD)[None, :]
    dist = t - (n * CMP_STRIDE + CMP_BLOCK - 1)
    cmp_ok = (dist >= 0) & (n < n_cmp)
    cmp_bucket = _rel_bucket_np(dist)
    ti = np.arange(TQ)[:, None]
    far = 2 * REL_MAX_DIST
    sel_dist = np.stack([TQ * v + ti - np.arange(TK_SEL)[None, :] for v in range(3)]
                        + [np.full((TQ, TK_SEL), far)])
    assert TQ * 3 - (TK_SEL - 1) >= REL_MAX_DIST
    sel_ok = sel_dist >= 0
    sel_bucket = _rel_bucket_np(sel_dist)
    win_dist = np.stack([TQ * v + ti - np.arange(WIN_SPAN)[None, :] for v in range(WINDOW // TQ + 1)])
    win_ok = (win_dist >= 0) & (win_dist < WINDOW)
    win_bucket = _rel_bucket_np(win_dist)
    j = np.arange(LANES)[:, None]
    overlap = ((n * CMP_STRIDE < j * SEL_BLOCK + SEL_BLOCK) & (n * CMP_STRIDE + CMP_BLOCK > j * SEL_BLOCK)
               & (n < n_cmp) & (j < n_sel)).astype(np.float32)
    expand = (np.arange(S)[None, :] // SEL_BLOCK == j).astype(np.float32)
    return (cmp_bucket, cmp_ok, sel_bucket, sel_ok, win_bucket, win_ok, overlap, expand)


def _bias_table(rel_table, bucket, ok):
    vals = jnp.take(rel_table.T, jnp.asarray(bucket), axis=1)
    vals = jnp.where(jnp.asarray(ok)[None], vals, MASKED)
    if bucket.ndim == 3:
        vals = jnp.moveaxis(vals, 0, 1)
    return vals


def _chunk_cols(w, width):
    K, N = w.shape
    return w.reshape(K, N // width, width).transpose(1, 0, 2)


def kernel(x, norm_mix_w, norm_mlp_w, rel_table, nsa_w_in, nsa_q_gain, nsa_k_gain, cmp_pe_k, cmp_w1_k, cmp_w2_k, cmp_pe_v, cmp_w1_v, cmp_w2_v, nsa_w_out, ssd_w_in, ssd_conv_w, ssd_conv_b, ssd_dt_bias, ssd_a_log, ssd_d, ssd_norm_w, ssd_w_out, mlp_w_up, mlp_w_down):
    B, S, D = x.shape
    T = B * S
    G, R = NSA_GROUPS, NSA_REP
    cmp_bucket, cmp_ok, sel_bucket, sel_ok, win_bucket, win_ok, overlap, expand = _position_tables(S)

    w_in = nsa_w_in[0]
    wq = _chunk_cols(w_in[:, :NSA_Q_DIM].astype(BF16), 256)
    wkv = _chunk_cols(w_in[:, NSA_Q_DIM:NSA_Q_DIM + 6 * NSA_KV_DIM].astype(BF16), 256)
    wg = w_in[:, NSA_Q_DIM + 6 * NSA_KV_DIM:].reshape(D, 3, G, R).transpose(0, 2, 1, 3).reshape(D, G, 3 * R)
    wg = jnp.pad(wg, ((0, 0), (0, 0), (0, 16 - 3 * R))).reshape(D, G * 16).astype(BF16)
    qg = jnp.tile(nsa_q_gain[0], 4)[None, :]
    kg = jnp.tile(nsa_k_gain[0], 4)[None, :]
    seg = jnp.asarray(np.kron(np.eye(4, dtype=np.float32), np.ones((HEAD_DIM, HEAD_DIM), np.float32)), BF16)
    q, kc, vc, ks, vs, kw, vw, gates = _nsa_proj(
        x, norm_mix_w[0][None, :], wq, wkv, wg, qg, kg, seg, tm=512)

    kcmp, vcmp = _nsa_compress(
        kc, vc, cmp_pe_k[0], cmp_pe_v[0], cmp_w1_k[0].astype(BF16), cmp_w1_v[0].astype(BF16),
        cmp_w2_k[0].astype(BF16), cmp_w2_v[0].astype(BF16), nsa_k_gain[0][None, :])

    bcmp = _bias_table(rel_table, cmp_bucket, cmp_ok)
    bsel = _bias_table(rel_table, sel_bucket, sel_ok)
    bwin = _bias_table(rel_table, win_bucket, win_ok)
    o = _nsa_attn(q, gates, kcmp, vcmp, ks, vs, kw, vw, bcmp, bsel, bwin,
                  jnp.asarray(overlap, BF16), jnp.asarray(np.eye(LANES, dtype=np.float32), BF16),
                  jnp.asarray(expand, BF16))

    def mlp_weights(li):
        return (_chunk_cols(mlp_w_up[li].astype(BF16), 512),
                mlp_w_down[li].astype(BF16).reshape(MLP_HIDDEN // 512, 512, D))

    wup, wdn = mlp_weights(0)
    x2 = _mix_out_mlp(x.reshape(T, D), o.reshape(T, NSA_Q_DIM), nsa_w_out[0].astype(BF16),
                      norm_mlp_w[0][None, :], wup, wdn, tm=512)

    w_in = ssd_w_in[0]
    wz = _chunk_cols(w_in[:, :SSD_D_INNER].astype(BF16), SSD_NCHUNK)
    wx = _chunk_cols(w_in[:, SSD_D_INNER:SSD_D_INNER + SSD_CONV_DIM].astype(BF16), SSD_NCHUNK)
    lane_pad = LANES - SSD_HEADS
    wdt = jnp.pad(w_in[:, SSD_D_INNER + SSD_CONV_DIM:], ((0, 0), (0, lane_pad))).astype(BF16)
    pad_heads = lambda v: jnp.pad(v, (0, lane_pad))[None, :]
    z, xbc, dt = _ssd_proj(x2.reshape(B, S, D), norm_mix_w[1][None, :], wz, wx, wdt,
                           ssd_conv_w[0], ssd_conv_b[0][None, :], pad_heads(ssd_dt_bias[0]), tm=256)
    tril = jnp.asarray(np.tril(np.ones((SSD_CHUNK, SSD_CHUNK), np.float32)), BF16)
    y = _ssd_scan(xbc, dt, z, pad_heads(ssd_a_log[0]), pad_heads(ssd_d[0]), ssd_norm_w[0][None, :], tril)

    wup, wdn = mlp_weights(1)
    x4 = _mix_out_mlp(x2, y.reshape(T, SSD_D_INNER), ssd_w_out[0].astype(BF16),
                      norm_mlp_w[1][None, :], wup, wdn, tm=512)
    return x4.reshape(B, S, D)
```

```python
import functools
import math

import numpy as np
import jax
import jax.numpy as jnp
from jax import lax
from jax.experimental import pallas as pl
from jax.experimental.pallas import tpu as pltpu

F32 = jnp.float32
BF16 = jnp.bfloat16

D_MODEL = 1024
EPS = 1e-6

NSA_HEADS = 16
NSA_GROUPS = 4
NSA_REP = NSA_HEADS // NSA_GROUPS
HEAD_DIM = 64
NSA_Q_DIM = NSA_HEADS * HEAD_DIM
NSA_KV_DIM = NSA_GROUPS * HEAD_DIM
CMP_BLOCK = 32
CMP_STRIDE = 16
CMP_HIDDEN = 256
SEL_BLOCK = 64
SEL_TOP_N = 16
WINDOW = 512
REL_BUCKETS = 32
REL_MAX_DIST = 128

SSD_D_INNER = 2 * D_MODEL
SSD_HEAD_DIM = 64
SSD_HEADS = SSD_D_INNER // SSD_HEAD_DIM
SSD_GROUPS = 8
SSD_STATE = 128
SSD_CONV = 4
SSD_CHUNK = 128
SSD_CONV_DIM = SSD_D_INNER + 2 * SSD_GROUPS * SSD_STATE
MLP_HIDDEN = 4 * D_MODEL

LANES = 128
SUBLANES = 8
VMEM_LIMIT_BYTES = 56 * 1024 * 1024

TQ = 128
TK_SEL = 256
WIN_SPAN = WINDOW + TQ
N_CMP_PAD = 128
MASKED = -1e30
SCORE_BIG = 3e38

NT_DIMS = (((1,), (1,)), ((), ()))


def _dot(a, b):
    return jnp.dot(a, b, preferred_element_type=F32)


def _dot_nt(a, b):
    return lax.dot_general(a, b, NT_DIMS, preferred_element_type=F32)


def _split3(x):
    hi = x.astype(BF16)
    r1 = x - hi.astype(F32)
    mid = r1.astype(BF16)
    lo = (r1 - mid.astype(F32)).astype(BF16)
    return hi, mid, lo


def _rms_rows(x, w):
    return x * lax.rsqrt(jnp.mean(x * x, axis=-1, keepdims=True) + EPS) * w


def _const_spec(shape):
    nd = len(shape)
    return pl.BlockSpec(shape, lambda *_: (0,) * nd, pipeline_mode=pl.Buffered(1))


def _compiler_params(semantics):
    return pltpu.CompilerParams(dimension_semantics=semantics, vmem_limit_bytes=VMEM_LIMIT_BYTES)


def _nsa_proj_kernel(x_ref, nw_ref, wq_ref, wk_ref, wvt_ref, wgt_ref, qg_ref, kg_ref, seg_ref,
                     q_ref, kc_ref, vc_ref, ks_ref, kw_ref, vst_ref, vwt_ref, gate_ref):
    tm = x_ref.shape[1]
    h = _rms_rows(x_ref[0], nw_ref[...]).astype(BF16)
    seg = seg_ref[...]

    def head_rms(y, gain):
        sq = y * y
        hi = sq.astype(BF16)
        lo = (sq - hi.astype(F32)).astype(BF16)
        ss = _dot(hi, seg) + _dot(lo, seg)
        return y * lax.rsqrt(ss * (1.0 / HEAD_DIM) + EPS) * gain

    for c in range(NSA_Q_DIM // 256):
        y = _dot(h, wq_ref[c])
        y = (head_rms(y, qg_ref[...]) * (HEAD_DIM ** -0.5)).astype(BF16)
        for r in range(4):
            q_ref[0, 4 * c + r] = y[:, 64 * r:64 * r + 64]

    row_out = (kc_ref, vc_ref, ks_ref, kw_ref)
    for c in range(4):
        y = _dot(h, wk_ref[c])
        if c >= 2:
            y = head_rms(y, kg_ref[...])
        y = y.astype(row_out[c].dtype)
        for g in range(NSA_GROUPS):
            row_out[c][0, g] = y[:, 64 * g:64 * g + 64]

    for c, vt_ref in enumerate((vst_ref, vwt_ref)):
        yt = _dot_nt(wvt_ref[c], h).astype(BF16)
        for g in range(NSA_GROUPS):
            for kb in range(tm // TQ):
                vt_ref[0, g, kb] = yt[64 * g:64 * g + 64, TQ * kb:TQ * (kb + 1)]

    gate_ref[0] = jax.nn.sigmoid(_dot_nt(wgt_ref[...], h))


def _nsa_proj(x, nw, wq, wk, wvt, wgt, qg, kg, seg, tm):
    B, S, D = x.shape
    grid = (B, S // tm)
    row_block = pl.BlockSpec((1, NSA_GROUPS, tm, HEAD_DIM), lambda b, i: (b, 0, i, 0))
    row_bf = jax.ShapeDtypeStruct((B, NSA_GROUPS, S, HEAD_DIM), BF16)
    row_f32 = jax.ShapeDtypeStruct((B, NSA_GROUPS, S, HEAD_DIM), F32)
    vt_block = pl.BlockSpec((1, NSA_GROUPS, tm // TQ, HEAD_DIM, TQ), lambda b, i: (b, 0, i, 0, 0))
    vt_shape = jax.ShapeDtypeStruct((B, NSA_GROUPS, S // TQ, HEAD_DIM, TQ), BF16)
    consts = (nw, wq, wk, wvt, wgt, qg, kg, seg)
    return pl.pallas_call(
        _nsa_proj_kernel,
        grid=grid,
        in_specs=[pl.BlockSpec((1, tm, D), lambda b, i: (b, i, 0))] + [_const_spec(c.shape) for c in consts],
        out_specs=[
            pl.BlockSpec((1, NSA_HEADS, tm, HEAD_DIM), lambda b, i: (b, 0, i, 0)),
            row_block, row_block, row_block, row_block, vt_block, vt_block,
            pl.BlockSpec((1, NSA_GROUPS * 16, tm), lambda b, i: (b, 0, i)),
        ],
        out_shape=[
            jax.ShapeDtypeStruct((B, NSA_HEADS, S, HEAD_DIM), BF16),
            row_f32, row_f32, row_bf, row_bf, vt_shape, vt_shape,
            jax.ShapeDtypeStruct((B, NSA_GROUPS * 16, S), F32),
        ],
        compiler_params=_compiler_params(("parallel", "parallel")),
        name="nsa_proj",
    )(x, *consts)


def _nsa_compress_kernel(kc_ref, vc_ref, pek_ref, pev_ref, w1k_ref, w1v_ref, w2k_ref, w2vt_ref, kg_ref,
                         kcmp_ref, vcmp_ref):
    half = CMP_BLOCK // 2

    def hidden(src_ref, pe_ref, w1_ref):
        acc_a = jnp.zeros((N_CMP_PAD, CMP_HIDDEN), F32)
        acc_b = jnp.zeros((N_CMP_PAD, CMP_HIDDEN), F32)
        for l in range(half):
            rows = src_ref[0, 0, pl.ds(l, N_CMP_PAD, stride=CMP_STRIDE), :]
            acc_a += _dot((rows + pe_ref[l:l + 1, :]).astype(BF16), w1_ref[l])
            acc_b += _dot((rows + pe_ref[half + l:half + l + 1, :]).astype(BF16), w1_ref[half + l])
        hid = acc_a + pltpu.roll(acc_b, N_CMP_PAD - 1, axis=0)
        return (hid * jax.nn.sigmoid(hid)).astype(BF16)

    k = _dot(hidden(kc_ref, pek_ref, w1k_ref), w2k_ref[...])
    kcmp_ref[0, 0] = _rms_rows(k, kg_ref[...]).astype(BF16)
    vcmp_ref[0, 0] = _dot_nt(w2vt_ref[...], hidden(vc_ref, pev_ref, w1v_ref)).astype(BF16)


def _nsa_compress(kc, vc, pek, pev, w1k, w1v, w2k, w2vt, kg):
    B, G, S, Dh = kc.shape
    src = pl.BlockSpec((1, 1, S, Dh), lambda b, g: (b, g, 0, 0))
    consts = (pek, pev, w1k, w1v, w2k, w2vt, kg)
    return pl.pallas_call(
        _nsa_compress_kernel,
        grid=(B, G),
        in_specs=[src, src] + [_const_spec(c.shape) for c in consts],
        out_specs=[pl.BlockSpec((1, 1, N_CMP_PAD, Dh), lambda b, g: (b, g, 0, 0)),
                   pl.BlockSpec((1, 1, Dh, N_CMP_PAD), lambda b, g: (b, g, 0, 0))],
        out_shape=[jax.ShapeDtypeStruct((B, G, N_CMP_PAD, Dh), BF16),
                   jax.ShapeDtypeStruct((B, G, Dh, N_CMP_PAD), BF16)],
        compiler_params=_compiler_params(("parallel", "parallel")),
        name="nsa_compress",
    )(kc, vc, *consts)


def _nsa_attn_kernel(q_ref, gate_ref, kcmp_ref, vcmpt_ref, ks_ref, vst_ref, kw_ref, vwt_ref,
                     bcmp_ref, bsel_ref, bwin_ref, ovl_ref, expt_ref,
                     o_ref, selmask_ref, m_ref, l_ref, acc_ref):
    i = pl.program_id(2)
    n_pair = NSA_REP // 2

    def q_pair(pr):
        return q_ref[0, 2 * pr:2 * pr + 2].reshape(2 * TQ, HEAD_DIM)

    def pair_lanes(a, b):
        return jnp.concatenate([a, b], axis=1)

    kcmp = kcmp_ref[0, 0]
    vcmpt = vcmpt_ref[0, 0]
    o_cmp = []
    psum = jnp.zeros((N_CMP_PAD, TQ), F32)
    for pr in range(n_pair):
        s = _dot_nt(kcmp, q_pair(pr)) + pair_lanes(bcmp_ref[2 * pr, 0], bcmp_ref[2 * pr + 1, 0])
        m = jnp.max(s, axis=0, keepdims=True)
        m = jnp.where(m > 0.5 * MASKED, m, 0.0)
        e = jnp.exp(s - m)
        den = jnp.sum(e, axis=0, keepdims=True)
        p = e / jnp.where(den > 0.0, den, 1.0)
        o_cmp.append(_dot(vcmpt, p.astype(BF16)))
        psum = psum + p[:, :TQ] + p[:, TQ:]

    ph, pm, plo = _split3(psum)
    ovl = ovl_ref[...]
    imp = _dot(ovl, ph) + _dot(ovl, pm) + _dot(ovl, plo)
    n_sel = 32
    imp = imp[0:n_sel]
    tok = i * TQ + lax.broadcasted_iota(jnp.int32, (n_sel, TQ), 1)
    blk = lax.broadcasted_iota(jnp.int32, (n_sel, TQ), 0)
    cur = tok // SEL_BLOCK
    forced = (blk == 0) | (blk == cur) | (blk == cur - 1)
    valid = blk * SEL_BLOCK <= tok
    score = jnp.where(forced, SCORE_BIG, imp)
    score = jnp.where(valid, score, -SCORE_BIG)
    rank = jnp.zeros((n_sel, TQ), F32)
    for jp in range(n_sel):
        other = score[jp:jp + 1, :]
        beats = (other > score) | ((other == score) & (blk > jp))
        rank += beats.astype(F32)
    sel_t = (rank < SEL_TOP_N).astype(BF16)
    sel_t = jnp.concatenate([sel_t, jnp.zeros((LANES - n_sel, TQ), BF16)], axis=0)
    for c in range(selmask_ref.shape[0]):
        hit = _dot(expt_ref[c * TK_SEL:(c + 1) * TK_SEL, :], sel_t)
        selmask_ref[c] = (hit - 1.0) * (-MASKED)

    m_ref[...] = jnp.full(m_ref.shape, MASKED, F32)
    l_ref[...] = jnp.zeros(l_ref.shape, F32)
    acc_ref[...] = jnp.zeros(acc_ref.shape, F32)
    blocks_per_tile = TK_SEL // TQ

    def sel_tile(c, carry):
        k0 = pl.multiple_of(c * TK_SEL, TK_SEL)
        k = ks_ref[0, 0, pl.ds(k0, TK_SEL), :]
        vt = pair_lanes(vst_ref[0, 0, blocks_per_tile * c], vst_ref[0, 0, blocks_per_tile * c + 1])
        variant = jnp.minimum(i - blocks_per_tile * c, 3)
        mask = selmask_ref[c]
        mask = pair_lanes(mask, mask)
        for pr in range(n_pair):
            s = _dot_nt(k, q_pair(pr))
            s = s + pair_lanes(bsel_ref[2 * pr, variant], bsel_ref[2 * pr + 1, variant]) + mask
            m_old = m_ref[pr]
            m_new = jnp.maximum(m_old, jnp.max(s, axis=0, keepdims=True))
            alpha = jnp.exp(m_old - m_new)
            pt = jnp.exp(s - m_new)
            l_ref[pr] = alpha * l_ref[pr] + jnp.sum(pt, axis=0, keepdims=True)
            acc_ref[pr] = alpha * acc_ref[pr] + _dot(vt, pt.astype(BF16))
            m_ref[pr] = m_new
        return carry

    lax.fori_loop(0, i // blocks_per_tile + 1, sel_tile, 0)

    n_wblk = WIN_SPAN // TQ
    b0 = jnp.maximum(i - WINDOW // TQ, 0)
    kw = kw_ref[0, 0, pl.ds(pl.multiple_of(b0 * TQ, TQ), WIN_SPAN), :]
    vwt = jnp.concatenate([vwt_ref[0, 0, b0 + t] for t in range(n_wblk)], axis=1)
    o_win = []
    for pr in range(n_pair):
        s = _dot_nt(kw, q_pair(pr)) + pair_lanes(bwin_ref[2 * pr, 0], bwin_ref[2 * pr + 1, 0])
        m = jnp.max(s, axis=0, keepdims=True)
        e = jnp.exp(s - m)
        o_win.append(_dot(vwt, e.astype(BF16)) / jnp.sum(e, axis=0, keepdims=True))

    gate = gate_ref[0]
    for r in range(NSA_REP):
        pr, half = divmod(r, 2)
        sl = slice(half * TQ, (half + 1) * TQ)
        o_sel = acc_ref[pr, :, sl] / l_ref[pr, :, sl]
        o_t = (gate[r:r + 1] * o_cmp[pr][:, sl] + gate[4 + r:5 + r] * o_sel
               + gate[8 + r:9 + r] * o_win[pr][:, sl])
        o_ref[0, :, r * HEAD_DIM:(r + 1) * HEAD_DIM] = o_t.T.astype(BF16)


def _nsa_attn(q, gates, kcmp, vcmpt, ks, vst, kw, vwt, bcmp, bsel, bwin, ovl, expand_t):
    B, H, S, Dh = q.shape
    G = NSA_GROUPS
    n_blk = S // TQ
    k_rows = pl.BlockSpec((1, 1, S, Dh), lambda g, b, i: (b, g, 0, 0))
    v_cols = pl.BlockSpec((1, 1, n_blk, Dh, TQ), lambda g, b, i: (b, g, 0, 0, 0))
    n_var = WINDOW // TQ
    return pl.pallas_call(
        _nsa_attn_kernel,
        grid=(G, B, n_blk),
        in_specs=[
            pl.BlockSpec((1, NSA_REP, TQ, Dh), lambda g, b, i: (b, g, i, 0)),
            pl.BlockSpec((1, 16, TQ), lambda g, b, i: (b, g, i)),
            pl.BlockSpec((1, 1, N_CMP_PAD, Dh), lambda g, b, i: (b, g, 0, 0)),
            pl.BlockSpec((1, 1, Dh, N_CMP_PAD), lambda g, b, i: (b, g, 0, 0)),
            k_rows, v_cols, k_rows, v_cols,
            pl.BlockSpec((NSA_REP, 1, N_CMP_PAD, TQ), lambda g, b, i: (g, i, 0, 0)),
            pl.BlockSpec((NSA_REP, 4, TK_SEL, TQ), lambda g, b, i: (g, 0, 0, 0)),
            pl.BlockSpec((NSA_REP, 1, WIN_SPAN, TQ), lambda g, b, i: (g, jnp.minimum(i, n_var), 0, 0)),
            _const_spec(ovl.shape), _const_spec(expand_t.shape),
        ],
        out_specs=pl.BlockSpec((1, TQ, NSA_REP * Dh), lambda g, b, i: (b, i, g)),
        out_shape=jax.ShapeDtypeStruct((B, S, H * Dh), BF16),
        scratch_shapes=[
            pltpu.VMEM((S // TK_SEL, TK_SEL, TQ), F32),
            pltpu.VMEM((NSA_REP // 2, 1, 2 * TQ), F32),
            pltpu.VMEM((NSA_REP // 2, 1, 2 * TQ), F32),
            pltpu.VMEM((NSA_REP // 2, Dh, 2 * TQ), F32),
        ],
        compiler_params=_compiler_params(("parallel", "parallel", "arbitrary")),
        name="nsa_attn",
    )(q, gates, kcmp, vcmpt, ks, vst, kw, vwt, bcmp, bsel, bwin, ovl, expand_t)


def _bias_table_kernel(table_ref, code_ref, out_ref):
    h = pl.program_id(0)
    code = code_ref[...]
    acc = jnp.full(code.shape, MASKED, F32)
    for b in range(REL_BUCKETS):
        acc = jnp.where(code == b, table_ref[b, h], acc)
    out_ref[0] = acc


def _bias_table(rel_table, code, rows_per_step):
    rows, lanes = code.shape
    return pl.pallas_call(
        _bias_table_kernel,
        grid=(NSA_HEADS, rows // rows_per_step),
        in_specs=[
            pl.BlockSpec(memory_space=pltpu.SMEM),
            pl.BlockSpec((rows_per_step, lanes), lambda h, i: (i, 0)),
        ],
        out_specs=pl.BlockSpec((1, rows_per_step, lanes), lambda h, i: (h, i, 0)),
        out_shape=jax.ShapeDtypeStruct((NSA_HEADS, rows, lanes), F32),
        compiler_params=_compiler_params(("parallel", "parallel")),
        name="rel_bias_table",
    )(rel_table, code)


def _mix_out_mlp_kernel(x_ref, a_ref, wo_ref, nw_ref, wup_ref, wdn_ref, o_ref):
    x1 = x_ref[...] + _dot(a_ref[...], wo_ref[...])
    h = _rms_rows(x1, nw_ref[...]).astype(BF16)

    def hidden_chunk(c, acc):
        u = jnp.maximum(_dot(h, wup_ref[c]), 0.0)
        return acc + _dot((u * u).astype(BF16), wdn_ref[c])

    acc = lax.fori_loop(0, wup_ref.shape[0], hidden_chunk, jnp.zeros(x1.shape, F32))
    o_ref[...] = x1 + acc


def _mix_out_mlp(x, a, wo, nw, wup, wdn, tm):
    T, D = x.shape
    K = a.shape[1]
    return pl.pallas_call(
        _mix_out_mlp_kernel,
        grid=(T // tm,),
        in_specs=[
            pl.BlockSpec((tm, D), lambda i: (i, 0)),
            pl.BlockSpec((tm, K), lambda i: (i, 0)),
            _const_spec(wo.shape), _const_spec(nw.shape), _const_spec(wup.shape), _const_spec(wdn.shape),
        ],
        out_specs=pl.BlockSpec((tm, D), lambda i: (i, 0)),
        out_shape=jax.ShapeDtypeStruct((T, D), F32),
        compiler_params=_compiler_params(("parallel",)),
        name="mix_out_mlp",
    )(x, a, wo, nw, wup, wdn)


SSD_NCHUNK = 512


def _ssd_proj_kernel(x_ref, nw_ref, wz_ref, wx_ref, wdt_ref, cw_ref, cb_ref, dtb_ref,
                     z_ref, xbc_ref, dt_ref, buf_ref):
    i = pl.program_id(1)
    tm = x_ref.shape[1]
    h = _rms_rows(x_ref[0], nw_ref[...]).astype(BF16)

    for c in range(wz_ref.shape[0]):
        z_ref[0, :, c * SSD_NCHUNK:(c + 1) * SSD_NCHUNK] = _dot(h, wz_ref[c])

    @pl.when(i == 0)
    def _():
        buf_ref[0:SUBLANES, :] = jnp.zeros((SUBLANES, buf_ref.shape[1]), F32)

    @pl.when(i > 0)
    def _():
        buf_ref[0:SUBLANES, :] = buf_ref[tm:tm + SUBLANES, :]

    for c in range(wx_ref.shape[0]):
        cols = slice(c * SSD_NCHUNK, (c + 1) * SSD_NCHUNK)
        buf_ref[SUBLANES:SUBLANES + tm, cols] = _dot(h, wx_ref[c])
        acc = cb_ref[:, cols] + buf_ref[SUBLANES:SUBLANES + tm, cols] * cw_ref[SSD_CONV - 1:SSD_CONV, cols]
        for k in range(SSD_CONV - 1):
            off = SUBLANES - (SSD_CONV - 1) + k
            acc += buf_ref[off:off + tm, cols] * cw_ref[k:k + 1, cols]
        xbc_ref[0, :, cols] = acc * jax.nn.sigmoid(acc)

    dt_ref[0] = jax.nn.softplus(_dot(h, wdt_ref[...]) + dtb_ref[...])


def _ssd_proj(x, nw, wz, wx, wdt, cw, cb, dtb, tm):
    B, S, D = x.shape
    consts = (nw, wz, wx, wdt, cw, cb, dtb)
    return pl.pallas_call(
        _ssd_proj_kernel,
        grid=(B, S // tm),
        in_specs=[pl.BlockSpec((1, tm, D), lambda b, i: (b, i, 0))] + [_const_spec(c.shape) for c in consts],
        out_specs=[
            pl.BlockSpec((1, tm, SSD_D_INNER), lambda b, i: (b, i, 0)),
            pl.BlockSpec((1, tm, SSD_CONV_DIM), lambda b, i: (b, i, 0)),
            pl.BlockSpec((1, tm, LANES), lambda b, i: (b, i, 0)),
        ],
        out_shape=[
            jax.ShapeDtypeStruct((B, S, SSD_D_INNER), F32),
            jax.ShapeDtypeStruct((B, S, SSD_CONV_DIM), F32),
            jax.ShapeDtypeStruct((B, S, LANES), F32),
        ],
        scratch_shapes=[pltpu.VMEM((tm + SUBLANES, SSD_CONV_DIM), F32)],
        compiler_params=_compiler_params(("parallel", "arbitrary")),
        name="ssd_proj",
    )(x, *consts)


def _ssd_scan_kernel(xbc_ref, dt_ref, z_ref, alog_ref, dskip_ref, nw_ref, tril_ref,
                     y_ref, state_ref):
    L = SSD_CHUNK
    P = SSD_HEAD_DIM
    N = SSD_STATE
    heads_per_group = SSD_HEADS // SSD_GROUPS

    @pl.when(pl.program_id(1) == 0)
    def _():
        state_ref[...] = jnp.zeros(state_ref.shape, F32)

    dt = dt_ref[0]
    a = dt * (-jnp.exp(alog_ref[...]))
    tril = tril_ref[...]
    ah, am, al = _split3(a)
    acs = _dot(tril, ah) + _dot(tril, am) + _dot(tril, al)
    acs_t = acs.T
    e_acs = jnp.exp(acs)
    e_last = e_acs[L - 1:L, :]
    to_end_t = jnp.exp(acs_t[:, L - 1:L] - acs_t)
    causal = (lax.broadcasted_iota(jnp.int32, (L, L), 0) >= lax.broadcasted_iota(jnp.int32, (L, L), 1))

    for g in range(SSD_GROUPS):
        b_off = SSD_D_INNER + g * N
        c_off = SSD_D_INNER + SSD_GROUPS * N + g * N
        bm = xbc_ref[0, :, b_off:b_off + N]
        cm = xbc_ref[0, :, c_off:c_off + N].astype(BF16)
        cb = _dot_nt(cm, bm.astype(BF16))
        bm_t = bm.T
        ys = []
        for j in range(heads_per_group):
            hd = g * heads_per_group + j
            x = xbc_ref[0, :, hd * P:(hd + 1) * P]
            xd = (x * dt[:, hd:hd + 1]).astype(BF16)
            diff = acs[:, hd:hd + 1] - acs_t[hd:hd + 1, :]
            decay = jnp.exp(jnp.where(causal, diff, -jnp.inf))
            y = _dot((cb * decay).astype(BF16), xd)
            st = state_ref[hd]
            y = y + _dot(cm, st.astype(BF16)) * e_acs[:, hd:hd + 1]
            state_ref[hd] = st * e_last[:, hd:hd + 1] + _dot((bm_t * to_end_t[hd:hd + 1, :]).astype(BF16), xd)
            zh = z_ref[0, :, hd * P:(hd + 1) * P]
            ys.append((y + x * dskip_ref[:, hd:hd + 1]) * (zh * jax.nn.sigmoid(zh)))
        ss = sum(jnp.sum(yh * yh, axis=-1, keepdims=True) for yh in ys)
        scale = lax.rsqrt(ss * (1.0 / (heads_per_group * P)) + EPS)
        for j, yh in enumerate(ys):
            cols = slice((g * heads_per_group + j) * P, (g * heads_per_group + j + 1) * P)
            y_ref[0, :, cols] = (yh * scale * nw_ref[:, cols]).astype(BF16)


def _ssd_scan(xbc, dt, z, alog, dskip, nw, tril):
    B, S, _ = xbc.shape
    L = SSD_CHUNK
    consts = (alog, dskip, nw, tril)
    return pl.pallas_call(
        _ssd_scan_kernel,
        grid=(B, S // L),
        in_specs=[
            pl.BlockSpec((1, L, SSD_CONV_DIM), lambda b, c: (b, c, 0)),
            pl.BlockSpec((1, L, LANES), lambda b, c: (b, c, 0)),
            pl.BlockSpec((1, L, SSD_D_INNER), lambda b, c: (b, c, 0)),
        ] + [_const_spec(c.shape) for c in consts],
        out_specs=pl.BlockSpec((1, L, SSD_D_INNER), lambda b, c: (b, c, 0)),
        out_shape=jax.ShapeDtypeStruct((B, S, SSD_D_INNER), BF16),
        scratch_shapes=[pltpu.VMEM((SSD_HEADS, SSD_STATE, SSD_HEAD_DIM), F32)],
        compiler_params=_compiler_params(("parallel", "arbitrary")),
        name="ssd_scan",
    )(xbc, dt, z, *consts)


def _rel_bucket_np(dist):
    max_exact = REL_BUCKETS // 2
    d = np.maximum(dist, 0)
    df = np.maximum(d, 1).astype(np.float32)
    large = max_exact + (np.log(df / np.float32(max_exact)) / np.float32(math.log(REL_MAX_DIST / max_exact))
                         * np.float32(REL_BUCKETS - max_exact)).astype(np.int32)
    large = np.minimum(large, REL_BUCKETS - 1)
    return np.where(d < max_exact, d, large).astype(np.int32)


@functools.lru_cache(maxsize=None)
def _position_tables(S):
    n_cmp = (S - CMP_BLOCK) // CMP_STRIDE + 1
    n_sel = S // SEL_BLOCK

    def code(dist, ok):
        return np.where(ok, _rel_bucket_np(dist), -1).astype(np.int32).reshape(-1, TQ)

    ti = np.arange(TQ)[None, :]
    n = np.arange(N_CMP_PAD)[None, :, None]
    t = (np.arange(S // TQ)[:, None, None] * TQ + ti[None])
    dist = t - (n * CMP_STRIDE + CMP_BLOCK - 1)
    cmp_code = code(dist, (dist >= 0) & (n < n_cmp))
    far = 2 * REL_MAX_DIST
    assert TQ * 3 - (TK_SEL - 1) >= REL_MAX_DIST
    sel_dist = np.stack([TQ * v + ti - np.arange(TK_SEL)[:, None] for v in range(3)]
                        + [np.full((TK_SEL, TQ), far)])
    sel_code = code(sel_dist, sel_dist >= 0)
    win_dist = np.stack([TQ * v + ti - np.arange(WIN_SPAN)[:, None] for v in range(WINDOW // TQ + 1)])
    win_code = code(win_dist, (win_dist >= 0) & (win_dist < WINDOW))
    j = np.arange(LANES)[:, None]
    nn = np.arange(N_CMP_PAD)[None, :]
    overlap = ((nn * CMP_STRIDE < j * SEL_BLOCK + SEL_BLOCK) & (nn * CMP_STRIDE + CMP_BLOCK > j * SEL_BLOCK)
               & (nn < n_cmp) & (j < n_sel)).astype(np.float32)
    expand_t = (np.arange(S)[:, None] // SEL_BLOCK == np.arange(LANES)[None, :]).astype(np.float32)
    return cmp_code, sel_code, win_code, overlap, expand_t


def _chunk_cols(w, width):
    K, N = w.shape
    return w.reshape(K, N // width, width).transpose(1, 0, 2)


def kernel(x, norm_mix_w, norm_mlp_w, rel_table, nsa_w_in, nsa_q_gain, nsa_k_gain, cmp_pe_k, cmp_w1_k, cmp_w2_k, cmp_pe_v, cmp_w1_v, cmp_w2_v, nsa_w_out, ssd_w_in, ssd_conv_w, ssd_conv_b, ssd_dt_bias, ssd_a_log, ssd_d, ssd_norm_w, ssd_w_out, mlp_w_up, mlp_w_down):
    B, S, D = x.shape
    T = B * S
    G, R = NSA_GROUPS, NSA_REP
    cmp_code, sel_code, win_code, overlap, expand_t = _position_tables(S)

    w_in = nsa_w_in[0].astype(BF16)
    kv0 = NSA_Q_DIM
    kv_cols = lambda c: w_in[:, kv0 + c * NSA_KV_DIM:kv0 + (c + 1) * NSA_KV_DIM]
    wq = _chunk_cols(w_in[:, :NSA_Q_DIM], 256)
    wk = jnp.stack([kv_cols(0), kv_cols(1), kv_cols(2), kv_cols(4)])
    wvt = jnp.stack([kv_cols(3).T, kv_cols(5).T])
    wg = w_in[:, kv0 + 6 * NSA_KV_DIM:].reshape(D, 3, G, R).transpose(0, 2, 1, 3).reshape(D, G, 3 * R)
    wgt = jnp.pad(wg, ((0, 0), (0, 0), (0, 16 - 3 * R))).reshape(D, G * 16).T
    qg = jnp.tile(nsa_q_gain[0], 4)[None, :]
    kg = jnp.tile(nsa_k_gain[0], 4)[None, :]
    seg = jnp.asarray(np.kron(np.eye(4, dtype=np.float32), np.ones((HEAD_DIM, HEAD_DIM), np.float32)), BF16)
    q, kc, vc, ks, kw, vst, vwt, gates = _nsa_proj(
        x, norm_mix_w[0][None, :], wq, wk, wvt, wgt, qg, kg, seg, tm=512)

    kcmp, vcmpt = _nsa_compress(
        kc, vc, cmp_pe_k[0], cmp_pe_v[0], cmp_w1_k[0].astype(BF16), cmp_w1_v[0].astype(BF16),
        cmp_w2_k[0].astype(BF16), cmp_w2_v[0].astype(BF16).T, nsa_k_gain[0][None, :])

    n_var = WINDOW // TQ + 1
    bcmp = _bias_table(rel_table, jnp.asarray(cmp_code), 512).reshape(NSA_HEADS, S // TQ, N_CMP_PAD, TQ)
    bsel = _bias_table(rel_table, jnp.asarray(sel_code), 512).reshape(NSA_HEADS, 4, TK_SEL, TQ)
    bwin = _bias_table(rel_table, jnp.asarray(win_code), WIN_SPAN).reshape(NSA_HEADS, n_var, WIN_SPAN, TQ)
    o = _nsa_attn(q, gates, kcmp, vcmpt, ks, vst, kw, vwt, bcmp, bsel, bwin,
                  jnp.asarray(overlap, BF16), jnp.asarray(expand_t, BF16))

    def mlp_weights(li):
        return (_chunk_cols(mlp_w_up[li].astype(BF16), 512),
                mlp_w_down[li].astype(BF16).reshape(MLP_HIDDEN // 512, 512, D))

    wup, wdn = mlp_weights(0)
    x2 = _mix_out_mlp(x.reshape(T, D), o.reshape(T, NSA_Q_DIM), nsa_w_out[0].astype(BF16),
                      norm_mlp_w[0][None, :], wup, wdn, tm=512)

    w_in = ssd_w_in[0]
    wz = _chunk_cols(w_in[:, :SSD_D_INNER].astype(BF16), SSD_NCHUNK)
    wx = _chunk_cols(w_in[:, SSD_D_INNER:SSD_D_INNER + SSD_CONV_DIM].astype(BF16), SSD_NCHUNK)
    lane_pad = LANES - SSD_HEADS
    wdt = jnp.pad(w_in[:, SSD_D_INNER + SSD_CONV_DIM:], ((0, 0), (0, lane_pad))).astype(BF16)
    pad_heads = lambda v: jnp.pad(v, (0, lane_pad))[None, :]
    z, xbc, dt = _ssd_proj(x2.reshape(B, S, D), norm_mix_w[1][None, :], wz, wx, wdt,
                           ssd_conv_w[0], ssd_conv_b[0][None, :], pad_heads(ssd_dt_bias[0]), tm=256)
    tril = jnp.asarray(np.tril(np.ones((SSD_CHUNK, SSD_CHUNK), np.float32)), BF16)
    y = _ssd_scan(xbc, dt, z, pad_heads(ssd_a_log[0]), pad_heads(ssd_d[0]), ssd_norm_w[0][None, :], tril)

    wup, wdn = mlp_weights(1)
    x4 = _mix_out_mlp(x2, y.reshape(T, SSD_D_INNER), ssd_w_out[0].astype(BF16),
                      norm_mlp_w[1][None, :], wup, wdn, tm=512)
    return x4.reshape(B, S, D)
```

```python
import functools
import math

import numpy as np
import jax
import jax.numpy as jnp
from jax import lax
from jax.experimental import pallas as pl
from jax.experimental.pallas import tpu as pltpu

F32 = jnp.float32
BF16 = jnp.bfloat16

D_MODEL = 1024
EPS = 1e-6

NSA_HEADS = 16
NSA_GROUPS = 4
NSA_REP = NSA_HEADS // NSA_GROUPS
HEAD_DIM = 64
NSA_Q_DIM = NSA_HEADS * HEAD_DIM
NSA_KV_DIM = NSA_GROUPS * HEAD_DIM
CMP_BLOCK = 32
CMP_STRIDE = 16
CMP_HIDDEN = 256
SEL_BLOCK = 64
SEL_TOP_N = 16
WINDOW = 512
REL_BUCKETS = 32
REL_MAX_DIST = 128

SSD_D_INNER = 2 * D_MODEL
SSD_HEAD_DIM = 64
SSD_HEADS = SSD_D_INNER // SSD_HEAD_DIM
SSD_GROUPS = 8
SSD_STATE = 128
SSD_CONV = 4
SSD_CHUNK = 128
SSD_CONV_DIM = SSD_D_INNER + 2 * SSD_GROUPS * SSD_STATE
MLP_HIDDEN = 4 * D_MODEL

LANES = 128
SUBLANES = 8
VMEM_LIMIT_BYTES = 56 * 1024 * 1024

VT_BLOCK = 128
TQC = 512
TQA = 256
TKA = 256
N_CMP_PAD = 128
N_SEL_PAD = 128
MASKED = -1e30
SCORE_BIG = 3e38
N_SEL_VARIANTS = 3
N_WIN_TILES = WINDOW // TKA + 1

NT_DIMS = (((1,), (1,)), ((), ()))


def _dot(a, b):
    return jnp.dot(a, b, preferred_element_type=F32)


def _dot_nt(a, b):
    return lax.dot_general(a, b, NT_DIMS, preferred_element_type=F32)


def _split3(x):
    hi = x.astype(BF16)
    r1 = x - hi.astype(F32)
    mid = r1.astype(BF16)
    lo = (r1 - mid.astype(F32)).astype(BF16)
    return hi, mid, lo


def _rms_rows(x, w):
    return x * lax.rsqrt(jnp.mean(x * x, axis=-1, keepdims=True) + EPS) * w


def _const_spec(shape):
    nd = len(shape)
    return pl.BlockSpec(shape, lambda *_: (0,) * nd, pipeline_mode=pl.Buffered(1))


def _compiler_params(semantics):
    return pltpu.CompilerParams(dimension_semantics=semantics, vmem_limit_bytes=VMEM_LIMIT_BYTES)


def _nsa_proj_kernel(x_ref, blk_ref, nw_ref, wq_ref, wk_ref, wvt_ref, wgt_ref, qg_ref, kg_ref, seg_ref,
                     q_ref, kc_ref, vc_ref, k_ref, vt_ref, gate_ref):
    tm = x_ref.shape[1]
    h = _rms_rows(x_ref[0], nw_ref[...]).astype(BF16)
    seg = seg_ref[...]

    def head_rms(y, gain):
        sq = y * y
        hi = sq.astype(BF16)
        lo = (sq - hi.astype(F32)).astype(BF16)
        ss = _dot(hi, seg) + _dot(lo, seg)
        return y * lax.rsqrt(ss * (1.0 / HEAD_DIM) + EPS) * gain

    for c in range(NSA_Q_DIM // 256):
        y = _dot(h, wq_ref[c])
        y = (head_rms(y, qg_ref[...]) * (HEAD_DIM ** -0.5)).astype(BF16)
        for r in range(4):
            q_ref[0, 4 * c + r] = y[:, 64 * r:64 * r + 64]

    for c, raw_ref in enumerate((kc_ref, vc_ref)):
        y = _dot(h, wk_ref[c])
        for g in range(NSA_GROUPS):
            raw_ref[0, g] = y[:, 64 * g:64 * g + 64]
    for br in range(2):
        y = head_rms(_dot(h, wk_ref[2 + br]), kg_ref[...]).astype(BF16)
        tag = blk_ref[...] if br == 0 else jnp.zeros(blk_ref.shape, BF16)
        for g in range(NSA_GROUPS):
            k_ref[0, g, br, :, 0:HEAD_DIM] = y[:, 64 * g:64 * g + 64]
            k_ref[0, g, br, :, HEAD_DIM:2 * HEAD_DIM] = tag
    for br in range(2):
        yt = _dot_nt(wvt_ref[br], h).astype(BF16)
        for g in range(NSA_GROUPS):
            for kb in range(tm // VT_BLOCK):
                vt_ref[0, g, br, kb] = yt[64 * g:64 * g + 64, VT_BLOCK * kb:VT_BLOCK * (kb + 1)]

    gate_ref[0] = jax.nn.sigmoid(_dot_nt(wgt_ref[...], h))


def _nsa_proj(x, key_block, nw, wq, wk, wvt, wgt, qg, kg, seg, tm):
    B, S, D = x.shape
    G = NSA_GROUPS
    raw_block = pl.BlockSpec((1, G, tm, HEAD_DIM), lambda b, i: (b, 0, i, 0))
    raw_shape = jax.ShapeDtypeStruct((B, G, S, HEAD_DIM), F32)
    consts = (nw, wq, wk, wvt, wgt, qg, kg, seg)
    return pl.pallas_call(
        _nsa_proj_kernel,
        grid=(B, S // tm),
        in_specs=[pl.BlockSpec((1, tm, D), lambda b, i: (b, i, 0)),
                  pl.BlockSpec((tm, HEAD_DIM), lambda b, i: (i, 0))] + [_const_spec(c.shape) for c in consts],
        out_specs=[
            pl.BlockSpec((1, NSA_HEADS, tm, HEAD_DIM), lambda b, i: (b, 0, i, 0)),
            raw_block, raw_block,
            pl.BlockSpec((1, G, 2, tm, 2 * HEAD_DIM), lambda b, i: (b, 0, 0, i, 0)),
            pl.BlockSpec((1, G, 2, tm // VT_BLOCK, HEAD_DIM, VT_BLOCK), lambda b, i: (b, 0, 0, i, 0, 0)),
            pl.BlockSpec((1, G * 16, tm), lambda b, i: (b, 0, i)),
        ],
        out_shape=[
            jax.ShapeDtypeStruct((B, NSA_HEADS, S, HEAD_DIM), BF16),
            raw_shape, raw_shape,
            jax.ShapeDtypeStruct((B, G, 2, S, 2 * HEAD_DIM), BF16),
            jax.ShapeDtypeStruct((B, G, 2, S // VT_BLOCK, HEAD_DIM, VT_BLOCK), BF16),
            jax.ShapeDtypeStruct((B, G * 16, S), F32),
        ],
        compiler_params=_compiler_params(("parallel", "parallel")),
        name="nsa_proj",
    )(x, key_block, *consts)


def _nsa_compress_kernel(kc_ref, vc_ref, pek_ref, pev_ref, w1k_ref, w1v_ref, w2k_ref, w2vt_ref, kg_ref,
                         kcmp_ref, vcmp_ref):
    half = CMP_BLOCK // 2

    def hidden(src_ref, pe_ref, w1_ref):
        acc_a = jnp.zeros((N_CMP_PAD, CMP_HIDDEN), F32)
        acc_b = jnp.zeros((N_CMP_PAD, CMP_HIDDEN), F32)
        for l in range(half):
            rows = src_ref[0, 0, pl.ds(l, N_CMP_PAD, stride=CMP_STRIDE), :]
            acc_a += _dot((rows + pe_ref[l:l + 1, :]).astype(BF16), w1_ref[l])
            acc_b += _dot((rows + pe_ref[half + l:half + l + 1, :]).astype(BF16), w1_ref[half + l])
        hid = acc_a + pltpu.roll(acc_b, N_CMP_PAD - 1, axis=0)
        return (hid * jax.nn.sigmoid(hid)).astype(BF16)

    k = _dot(hidden(kc_ref, pek_ref, w1k_ref), w2k_ref[...])
    kcmp_ref[0, 0] = _rms_rows(k, kg_ref[...]).astype(BF16)
    vcmp_ref[0, 0] = _dot_nt(w2vt_ref[...], hidden(vc_ref, pev_ref, w1v_ref)).astype(BF16)


def _nsa_compress(kc, vc, pek, pev, w1k, w1v, w2k, w2vt, kg):
    B, G, S, Dh = kc.shape
    src = pl.BlockSpec((1, 1, S, Dh), lambda b, g: (b, g, 0, 0))
    consts = (pek, pev, w1k, w1v, w2k, w2vt, kg)
    return pl.pallas_call(
        _nsa_compress_kernel,
        grid=(B, G),
        in_specs=[src, src] + [_const_spec(c.shape) for c in consts],
        out_specs=[pl.BlockSpec((1, 1, N_CMP_PAD, Dh), lambda b, g: (b, g, 0, 0)),
                   pl.BlockSpec((1, 1, Dh, N_CMP_PAD), lambda b, g: (b, g, 0, 0))],
        out_shape=[jax.ShapeDtypeStruct((B, G, N_CMP_PAD, Dh), BF16),
                   jax.ShapeDtypeStruct((B, G, Dh, N_CMP_PAD), BF16)],
        compiler_params=_compiler_params(("parallel", "parallel")),
        name="nsa_compress",
    )(kc, vc, *consts)


def _bias_table_kernel(table_ref, code_ref, out_ref):
    h = pl.program_id(0)
    code = code_ref[...]
    acc = jnp.full(code.shape, MASKED, F32)
    for b in range(REL_BUCKETS):
        acc = jnp.where(code == b, table_ref[b, h], acc)
    out_ref[0] = acc


def _bias_table(rel_table, code, rows_per_step):
    rows, lanes = code.shape
    return pl.pallas_call(
        _bias_table_kernel,
        grid=(NSA_HEADS, rows // rows_per_step),
        in_specs=[
            pl.BlockSpec(memory_space=pltpu.SMEM),
            pl.BlockSpec((rows_per_step, lanes), lambda h, i: (i, 0)),
        ],
        out_specs=pl.BlockSpec((1, rows_per_step, lanes), lambda h, i: (h, i, 0)),
        out_shape=jax.ShapeDtypeStruct((NSA_HEADS, rows, lanes), F32),
        compiler_params=_compiler_params(("parallel", "parallel")),
        name="rel_bias_table",
    )(rel_table, code)


def _nsa_cmp_select_kernel(q_ref, gate_ref, kcmp_ref, vcmpt_ref, bias_ref, ovl_ref, o_ref, qa_ref):
    i = pl.program_id(2)
    kcmp = kcmp_ref[0, 0]
    vcmpt = vcmpt_ref[0, 0]

    scores = [_dot_nt(kcmp, q_ref[0, r]) for r in range(NSA_REP)]
    probs = []
    for r in range(NSA_REP):
        s = scores[r] + bias_ref[r, 0]
        m = jnp.max(s, axis=0, keepdims=True)
        m = jnp.where(m > 0.5 * MASKED, m, 0.0)
        e = jnp.exp(s - m)
        den = jnp.sum(e, axis=0, keepdims=True)
        probs.append(e / jnp.where(den > 0.0, den, 1.0))
    outs = [_dot(vcmpt, p.astype(BF16)) for p in probs]

    ph, pm, plo = _split3(probs[0] + probs[1] + probs[2] + probs[3])
    ovl = ovl_ref[...]
    n_sel = 32
    imp = (_dot(ovl, ph) + _dot(ovl, pm) + _dot(ovl, plo))[0:n_sel]
    tok = i * TQC + lax.broadcasted_iota(jnp.int32, (n_sel, TQC), 1)
    blk = lax.broadcasted_iota(jnp.int32, (n_sel, TQC), 0)
    cur = tok // SEL_BLOCK
    forced = (blk == 0) | (blk == cur) | (blk == cur - 1)
    valid = blk * SEL_BLOCK <= tok
    score = jnp.where(forced, SCORE_BIG, imp)
    score = jnp.where(valid, score, -SCORE_BIG)
    rank = jnp.zeros((n_sel, TQC), F32)
    for jp in range(n_sel):
        other = score[jp:jp + 1, :]
        beats = (other > score) | ((other == score) & (blk > jp))
        rank += beats.astype(F32)
    blk_mask = jnp.where(rank < SEL_TOP_N, 0.0, MASKED)
    blk_mask = jnp.concatenate([blk_mask, jnp.zeros((HEAD_DIM - n_sel, TQC), F32)], axis=0).T.astype(BF16)
    for r in range(NSA_REP):
        qa_ref[0, r, :, 0:HEAD_DIM] = q_ref[0, r]
        qa_ref[0, r, :, HEAD_DIM:2 * HEAD_DIM] = blk_mask

    gate = gate_ref[0]
    for r in range(NSA_REP):
        o_ref[0, :, r * HEAD_DIM:(r + 1) * HEAD_DIM] = (outs[r] * gate[r:r + 1]).T


def _nsa_cmp_select(q, gates, kcmp, vcmpt, bcmp, ovl):
    B, H, S, Dh = q.shape
    G = NSA_GROUPS
    return pl.pallas_call(
        _nsa_cmp_select_kernel,
        grid=(G, B, S // TQC),
        in_specs=[
            pl.BlockSpec((1, NSA_REP, TQC, Dh), lambda g, b, i: (b, g, i, 0)),
            pl.BlockSpec((1, 16, TQC), lambda g, b, i: (b, g, i)),
            pl.BlockSpec((1, 1, N_CMP_PAD, Dh), lambda g, b, i: (b, g, 0, 0)),
            pl.BlockSpec((1, 1, Dh, N_CMP_PAD), lambda g, b, i: (b, g, 0, 0)),
            pl.BlockSpec((NSA_REP, 1, N_CMP_PAD, TQC), lambda g, b, i: (g, i, 0, 0)),
            _const_spec(ovl.shape),
        ],
        out_specs=[
            pl.BlockSpec((1, TQC, NSA_REP * Dh), lambda g, b, i: (b, i, g)),
            pl.BlockSpec((1, NSA_REP, TQC, 2 * Dh), lambda g, b, i: (b, g, i, 0)),
        ],
        out_shape=[
            jax.ShapeDtypeStruct((B, S, H * Dh), F32),
            jax.ShapeDtypeStruct((B, H, S, 2 * Dh), BF16),
        ],
        compiler_params=_compiler_params(("parallel", "parallel", "arbitrary")),
        name="nsa_cmp_select",
    )(q, gates, kcmp, vcmpt, bcmp, ovl)


def _nsa_attn_kernel(q_ref, gate_ref, k_ref, vt_ref, ocmp_ref, bias_ref,
                     o_ref, s0_ref, s1_ref, p0_ref, p1_ref, a0_ref, a1_ref, m_ref, l_ref, acc_ref):
    i = pl.program_id(2)
    n_sel = i + 1
    n_win = jnp.minimum(N_WIN_TILES, n_sel)
    n_tasks = n_sel + n_win
    s_bufs, p_bufs, a_bufs = (s0_ref, s1_ref), (p0_ref, p1_ref), (a0_ref, a1_ref)
    masked_variant = N_SEL_VARIANTS + N_WIN_TILES

    def task(t):
        is_win = t >= n_sel
        c = jnp.clip(jnp.where(is_win, i - (t - n_sel), t), 0, i)
        behind = i - c
        variant = jnp.where(is_win, N_SEL_VARIANTS + behind, jnp.minimum(behind, N_SEL_VARIANTS - 1))
        variant = jnp.where(t < n_tasks, variant, masked_variant)
        return is_win.astype(jnp.int32), c, variant

    def score_stage(t, slot):
        br, c, variant = task(t)
        k0 = pl.multiple_of(c * TKA, TKA)
        k = k_ref[0, 0, br, pl.ds(k0, TKA), :]
        scores = [_dot_nt(k, q_ref[0, r]) for r in range(NSA_REP)]
        for r in range(NSA_REP):
            s_bufs[slot][r] = scores[r] + bias_ref[r, variant]

    def softmax_stage(t, slot):
        br, _, _ = task(t)
        for r in range(NSA_REP):
            s = s_bufs[slot][r]
            m_old = m_ref[br, r]
            m_new = jnp.maximum(m_old, jnp.max(s, axis=0, keepdims=True))
            alpha = jnp.exp(m_old - m_new)
            p = jnp.exp(s - m_new)
            l_ref[br, r] = alpha * l_ref[br, r] + jnp.sum(p, axis=0, keepdims=True)
            m_ref[br, r] = m_new
            a_bufs[slot][r] = alpha
            p_bufs[slot][r] = p.astype(BF16)

    def value_stage(t, slot):
        br, c, _ = task(t)
        blocks = TKA // VT_BLOCK
        vt = jnp.concatenate([vt_ref[0, 0, br, blocks * c + u] for u in range(blocks)], axis=1)
        outs = [_dot(vt, p_bufs[slot][r]) for r in range(NSA_REP)]
        for r in range(NSA_REP):
            acc_ref[br, r] = a_bufs[slot][r] * acc_ref[br, r] + outs[r]

    m_ref[...] = jnp.full(m_ref.shape, MASKED, F32)
    l_ref[...] = jnp.zeros(l_ref.shape, F32)
    acc_ref[...] = jnp.zeros(acc_ref.shape, F32)
    p1_ref[...] = jnp.zeros(p1_ref.shape, BF16)
    a1_ref[...] = jnp.ones(a1_ref.shape, F32)
    score_stage(0, 0)

    def trip(u, carry):
        t = 2 * u
        value_stage(jnp.maximum(t - 1, 0), 1)
        score_stage(t + 1, 1)
        softmax_stage(t, 0)
        value_stage(t, 0)
        score_stage(t + 2, 0)
        softmax_stage(t + 1, 1)
        return carry

    n_trips = (n_tasks + 1) // 2
    lax.fori_loop(0, n_trips, trip, 0)
    value_stage(2 * n_trips - 1, 1)

    gate = gate_ref[0]
    for r in range(NSA_REP):
        o_t = (gate[4 + r:5 + r] * (acc_ref[0, r] / l_ref[0, r])
               + gate[8 + r:9 + r] * (acc_ref[1, r] / l_ref[1, r]))
        cols = slice(r * HEAD_DIM, (r + 1) * HEAD_DIM)
        o_ref[0, :, cols] = (ocmp_ref[0, :, cols] + o_t.T).astype(BF16)


def _nsa_attn(q_aug, gates, k_sw, vt_sw, o_cmp, bias):
    B, H, S, Dq = q_aug.shape
    Dh = HEAD_DIM
    G = NSA_GROUPS
    n_vt = S // VT_BLOCK
    n_var = bias.shape[1]
    return pl.pallas_call(
        _nsa_attn_kernel,
        grid=(G, B, S // TQA),
        in_specs=[
            pl.BlockSpec((1, NSA_REP, TQA, Dq), lambda g, b, i: (b, g, i, 0)),
            pl.BlockSpec((1, 16, TQA), lambda g, b, i: (b, g, i)),
            pl.BlockSpec((1, 1, 2, S, Dq), lambda g, b, i: (b, g, 0, 0, 0)),
            pl.BlockSpec((1, 1, 2, n_vt, Dh, VT_BLOCK), lambda g, b, i: (b, g, 0, 0, 0, 0)),
            pl.BlockSpec((1, TQA, NSA_REP * Dh), lambda g, b, i: (b, i, g)),
            pl.BlockSpec((NSA_REP, n_var, TKA, TQA), lambda g, b, i: (g, 0, 0, 0), pipeline_mode=pl.Buffered(1)),
        ],
        out_specs=pl.BlockSpec((1, TQA, NSA_REP * Dh), lambda g, b, i: (b, i, g)),
        out_shape=jax.ShapeDtypeStruct((B, S, H * Dh), BF16),
        scratch_shapes=[
            pltpu.VMEM((NSA_REP, TKA, TQA), F32),
            pltpu.VMEM((NSA_REP, TKA, TQA), F32),
            pltpu.VMEM((NSA_REP, TKA, TQA), BF16),
            pltpu.VMEM((NSA_REP, TKA, TQA), BF16),
            pltpu.VMEM((NSA_REP, 1, TQA), F32),
            pltpu.VMEM((NSA_REP, 1, TQA), F32),
            pltpu.VMEM((2, NSA_REP, 1, TQA), F32),
            pltpu.VMEM((2, NSA_REP, 1, TQA), F32),
            pltpu.VMEM((2, NSA_REP, Dh, TQA), F32),
        ],
        compiler_params=_compiler_params(("parallel", "parallel", "arbitrary")),
        name="nsa_attn",
    )(q_aug, gates, k_sw, vt_sw, o_cmp, bias)


def _mix_out_mlp_kernel(x_ref, a_ref, wo_ref, nw_ref, wup_ref, wdn_ref, o_ref):
    x1 = x_ref[...] + _dot(a_ref[...], wo_ref[...])
    h = _rms_rows(x1, nw_ref[...]).astype(BF16)

    def hidden_chunk(c, acc):
        u = jnp.maximum(_dot(h, wup_ref[c]), 0.0)
        return acc + _dot((u * u).astype(BF16), wdn_ref[c])

    acc = lax.fori_loop(0, wup_ref.shape[0], hidden_chunk, jnp.zeros(x1.shape, F32))
    o_ref[...] = x1 + acc


def _mix_out_mlp(x, a, wo, nw, wup, wdn, tm):
    T, D = x.shape
    K = a.shape[1]
    return pl.pallas_call(
        _mix_out_mlp_kernel,
        grid=(T // tm,),
        in_specs=[
            pl.BlockSpec((tm, D), lambda i: (i, 0)),
            pl.BlockSpec((tm, K), lambda i: (i, 0)),
            _const_spec(wo.shape), _const_spec(nw.shape), _const_spec(wup.shape), _const_spec(wdn.shape),
        ],
        out_specs=pl.BlockSpec((tm, D), lambda i: (i, 0)),
        out_shape=jax.ShapeDtypeStruct((T, D), F32),
        compiler_params=_compiler_params(("parallel",)),
        name="mix_out_mlp",
    )(x, a, wo, nw, wup, wdn)


SSD_NCHUNK = 512


def _ssd_proj_kernel(x_ref, nw_ref, wz_ref, wx_ref, wdt_ref, cw_ref, cb_ref, dtb_ref,
                     z_ref, xbc_ref, dt_ref, buf_ref):
    i = pl.program_id(1)
    tm = x_ref.shape[1]
    h = _rms_rows(x_ref[0], nw_ref[...]).astype(BF16)

    for c in range(wz_ref.shape[0]):
        z_ref[0, :, c * SSD_NCHUNK:(c + 1) * SSD_NCHUNK] = _dot(h, wz_ref[c])

    @pl.when(i == 0)
    def _():
        buf_ref[0:SUBLANES, :] = jnp.zeros((SUBLANES, buf_ref.shape[1]), F32)

    @pl.when(i > 0)
    def _():
        buf_ref[0:SUBLANES, :] = buf_ref[tm:tm + SUBLANES, :]

    for c in range(wx_ref.shape[0]):
        cols = slice(c * SSD_NCHUNK, (c + 1) * SSD_NCHUNK)
        buf_ref[SUBLANES:SUBLANES + tm, cols] = _dot(h, wx_ref[c])
        acc = cb_ref[:, cols] + buf_ref[SUBLANES:SUBLANES + tm, cols] * cw_ref[SSD_CONV - 1:SSD_CONV, cols]
        for k in range(SSD_CONV - 1):
            off = SUBLANES - (SSD_CONV - 1) + k
            acc += buf_ref[off:off + tm, cols] * cw_ref[k:k + 1, cols]
        xbc_ref[0, :, cols] = acc * jax.nn.sigmoid(acc)

    dt_ref[0] = jax.nn.softplus(_dot(h, wdt_ref[...]) + dtb_ref[...])


def _ssd_proj(x, nw, wz, wx, wdt, cw, cb, dtb, tm):
    B, S, D = x.shape
    consts = (nw, wz, wx, wdt, cw, cb, dtb)
    return pl.pallas_call(
        _ssd_proj_kernel,
        grid=(B, S // tm),
        in_specs=[pl.BlockSpec((1, tm, D), lambda b, i: (b, i, 0))] + [_const_spec(c.shape) for c in consts],
        out_specs=[
            pl.BlockSpec((1, tm, SSD_D_INNER), lambda b, i: (b, i, 0)),
            pl.BlockSpec((1, tm, SSD_CONV_DIM), lambda b, i: (b, i, 0)),
            pl.BlockSpec((1, tm, LANES), lambda b, i: (b, i, 0)),
        ],
        out_shape=[
            jax.ShapeDtypeStruct((B, S, SSD_D_INNER), F32),
            jax.ShapeDtypeStruct((B, S, SSD_CONV_DIM), F32),
            jax.ShapeDtypeStruct((B, S, LANES), F32),
        ],
        scratch_shapes=[pltpu.VMEM((tm + SUBLANES, SSD_CONV_DIM), F32)],
        compiler_params=_compiler_params(("parallel", "arbitrary")),
        name="ssd_proj",
    )(x, *consts)


def _ssd_scan_kernel(xbc_ref, dt_ref, z_ref, alog_ref, dskip_ref, nw_ref, tril_ref, hexp_ref,
                     y_ref, state_ref):
    L = SSD_CHUNK
    P = SSD_HEAD_DIM
    N = SSD_STATE
    hpg = SSD_HEADS // SSD_GROUPS
    gw = hpg * P

    @pl.when(pl.program_id(1) == 0)
    def _():
        state_ref[...] = jnp.zeros(state_ref.shape, F32)

    dt = dt_ref[0]
    a = dt * (-jnp.exp(alog_ref[...]))
    tril = tril_ref[...]
    ah, am, al = _split3(a)
    acs = _dot(tril, ah) + _dot(tril, am) + _dot(tril, al)
    acs_t = acs.T
    dt_t = dt.T
    e_acs = jnp.exp(acs)
    eh, em, el = _split3(e_acs)
    w_t = jnp.exp(acs_t[:, L - 1:L] - acs_t) * dt_t
    causal = (lax.broadcasted_iota(jnp.int32, (L, L), 0) >= lax.broadcasted_iota(jnp.int32, (L, L), 1))
    head_of_lane = lax.broadcasted_iota(jnp.int32, (L, gw), 1) // P

    for g in range(SSD_GROUPS):
        b_off = SSD_D_INNER + g * N
        c_off = SSD_D_INNER + SSD_GROUPS * N + g * N
        bm = xbc_ref[0, :, b_off:b_off + N]
        cm = xbc_ref[0, :, c_off:c_off + N].astype(BF16)
        cb = _dot_nt(cm, bm.astype(BF16))
        bm_t = bm.T
        xg = xbc_ref[0, :, g * gw:(g + 1) * gw]
        hexp = hexp_ref[g]
        e_exp = _dot(eh, hexp) + _dot(em, hexp) + _dot(el, hexp)
        st = state_ref[g]
        y = _dot(cm, st.astype(BF16)) * e_exp
        st = st * e_exp[L - 1:L, :]
        for j in range(hpg):
            hd = g * hpg + j
            xj = jnp.where(head_of_lane == j, xg, 0.0).astype(BF16)
            diff = acs[:, hd:hd + 1] - acs_t[hd:hd + 1, :]
            decay = jnp.exp(jnp.where(causal, diff, -jnp.inf))
            y = y + _dot((cb * decay * dt_t[hd:hd + 1, :]).astype(BF16), xj)
            st = st + _dot((bm_t * w_t[hd:hd + 1, :]).astype(BF16), xj)
        state_ref[g] = st
        cols = slice(g * gw, (g + 1) * gw)
        zg = z_ref[0, :, cols]
        y = (y + xg * dskip_ref[:, cols]) * (zg * jax.nn.sigmoid(zg))
        y_ref[0, :, cols] = _rms_rows(y, nw_ref[:, cols]).astype(BF16)


def _ssd_scan(xbc, dt, z, alog, dskip, nw, tril, hexp):
    B, S, _ = xbc.shape
    L = SSD_CHUNK
    consts = (alog, dskip, nw, tril, hexp)
    return pl.pallas_call(
        _ssd_scan_kernel,
        grid=(B, S // L),
        in_specs=[
            pl.BlockSpec((1, L, SSD_CONV_DIM), lambda b, c: (b, c, 0)),
            pl.BlockSpec((1, L, LANES), lambda b, c: (b, c, 0)),
            pl.BlockSpec((1, L, SSD_D_INNER), lambda b, c: (b, c, 0)),
        ] + [_const_spec(c.shape) for c in consts],
        out_specs=pl.BlockSpec((1, L, SSD_D_INNER), lambda b, c: (b, c, 0)),
        out_shape=jax.ShapeDtypeStruct((B, S, SSD_D_INNER), BF16),
        scratch_shapes=[pltpu.VMEM((SSD_GROUPS, SSD_STATE, SSD_D_INNER // SSD_GROUPS), F32)],
        compiler_params=_compiler_params(("parallel", "arbitrary")),
        name="ssd_scan",
    )(xbc, dt, z, *consts)


def _rel_bucket_np(dist):
    max_exact = REL_BUCKETS // 2
    d = np.maximum(dist, 0)
    df = np.maximum(d, 1).astype(np.float32)
    large = max_exact + (np.log(df / np.float32(max_exact)) / np.float32(math.log(REL_MAX_DIST / max_exact))
                         * np.float32(REL_BUCKETS - max_exact)).astype(np.int32)
    large = np.minimum(large, REL_BUCKETS - 1)
    return np.where(d < max_exact, d, large).astype(np.int32)


@functools.lru_cache(maxsize=None)
def _position_tables(S):
    n_cmp = (S - CMP_BLOCK) // CMP_STRIDE + 1
    n_sel = S // SEL_BLOCK

    def code(dist, ok):
        return np.where(ok, _rel_bucket_np(dist), -1).astype(np.int32).reshape(-1, dist.shape[-1])

    n = np.arange(N_CMP_PAD)[None, :, None]
    t = np.arange(S // TQC)[:, None, None] * TQC + np.arange(TQC)[None, None, :]
    dist = t - (n * CMP_STRIDE + CMP_BLOCK - 1)
    cmp_code = code(dist, (dist >= 0) & (n < n_cmp))
    assert TQA == TKA and (N_SEL_VARIANTS - 1) * TQA - (TKA - 1) >= REL_MAX_DIST
    ti = np.arange(TQA)[None, :]
    j = np.arange(TKA)[:, None]
    tile_dist = [TQA * behind + ti - j for behind in range(N_WIN_TILES)]
    far = np.full((TKA, TQA), 2 * REL_MAX_DIST)
    sel_dist = np.stack(tile_dist[:N_SEL_VARIANTS - 1] + [far])
    win_dist = np.stack(tile_dist)
    attn_code = np.concatenate([code(sel_dist, sel_dist >= 0),
                                code(win_dist, (win_dist >= 0) & (win_dist < WINDOW)),
                                np.full((TKA, TQA), -1, np.int32)])
    jb = np.arange(N_SEL_PAD)[:, None]
    nn = np.arange(N_CMP_PAD)[None, :]
    overlap = ((nn * CMP_STRIDE < jb * SEL_BLOCK + SEL_BLOCK) & (nn * CMP_STRIDE + CMP_BLOCK > jb * SEL_BLOCK)
               & (nn < n_cmp) & (jb < n_sel)).astype(np.float32)
    key_block = (np.arange(S)[:, None] // SEL_BLOCK == np.arange(HEAD_DIM)[None, :]).astype(np.float32)
    hpg = SSD_HEADS // SSD_GROUPS
    lane_head = np.arange(hpg * SSD_HEAD_DIM)[None, None, :] // SSD_HEAD_DIM
    head_expand = (np.arange(LANES)[None, :, None]
                   == np.arange(SSD_GROUPS)[:, None, None] * hpg + lane_head).astype(np.float32)
    return cmp_code, attn_code, overlap, key_block, head_expand


def _chunk_cols(w, width):
    K, N = w.shape
    return w.reshape(K, N // width, width).transpose(1, 0, 2)


def kernel(x, norm_mix_w, norm_mlp_w, rel_table, nsa_w_in, nsa_q_gain, nsa_k_gain, cmp_pe_k, cmp_w1_k, cmp_w2_k, cmp_pe_v, cmp_w1_v, cmp_w2_v, nsa_w_out, ssd_w_in, ssd_conv_w, ssd_conv_b, ssd_dt_bias, ssd_a_log, ssd_d, ssd_norm_w, ssd_w_out, mlp_w_up, mlp_w_down):
    B, S, D = x.shape
    T = B * S
    G, R = NSA_GROUPS, NSA_REP
    cmp_code, attn_code, overlap, key_block, head_expand = _position_tables(S)

    w_in = nsa_w_in[0].astype(BF16)
    kv0 = NSA_Q_DIM
    kv_cols = lambda c: w_in[:, kv0 + c * NSA_KV_DIM:kv0 + (c + 1) * NSA_KV_DIM]
    wq = _chunk_cols(w_in[:, :NSA_Q_DIM], 256)
    wk = jnp.stack([kv_cols(0), kv_cols(1), kv_cols(2), kv_cols(4)])
    wvt = jnp.stack([kv_cols(3).T, kv_cols(5).T])
    wg = w_in[:, kv0 + 6 * NSA_KV_DIM:].reshape(D, 3, G, R).transpose(0, 2, 1, 3).reshape(D, G, 3 * R)
    wgt = jnp.pad(wg, ((0, 0), (0, 0), (0, 16 - 3 * R))).reshape(D, G * 16).T
    qg = jnp.tile(nsa_q_gain[0], 4)[None, :]
    kg = jnp.tile(nsa_k_gain[0], 4)[None, :]
    seg = jnp.asarray(np.kron(np.eye(4, dtype=np.float32), np.ones((HEAD_DIM, HEAD_DIM), np.float32)), BF16)
    q, kc, vc, k_sw, vt_sw, gates = _nsa_proj(
        x, jnp.asarray(key_block, BF16), norm_mix_w[0][None, :], wq, wk, wvt, wgt, qg, kg, seg, tm=512)

    kcmp, vcmpt = _nsa_compress(
        kc, vc, cmp_pe_k[0], cmp_pe_v[0], cmp_w1_k[0].astype(BF16), cmp_w1_v[0].astype(BF16),
        cmp_w2_k[0].astype(BF16), cmp_w2_v[0].astype(BF16).T, nsa_k_gain[0][None, :])

    bcmp = _bias_table(rel_table, jnp.asarray(cmp_code), N_CMP_PAD).reshape(NSA_HEADS, S // TQC, N_CMP_PAD, TQC)
    battn = _bias_table(rel_table, jnp.asarray(attn_code), TKA).reshape(NSA_HEADS, -1, TKA, TQA)
    o_cmp, q_aug = _nsa_cmp_select(q, gates, kcmp, vcmpt, bcmp, jnp.asarray(overlap, BF16))
    o = _nsa_attn(q_aug, gates, k_sw, vt_sw, o_cmp, battn)

    def mlp_weights(li):
        return (_chunk_cols(mlp_w_up[li].astype(BF16), 512),
                mlp_w_down[li].astype(BF16).reshape(MLP_HIDDEN // 512, 512, D))

    wup, wdn = mlp_weights(0)
    x2 = _mix_out_mlp(x.reshape(T, D), o.reshape(T, NSA_Q_DIM), nsa_w_out[0].astype(BF16),
                      norm_mlp_w[0][None, :], wup, wdn, tm=512)

    w_in = ssd_w_in[0]
    wz = _chunk_cols(w_in[:, :SSD_D_INNER].astype(BF16), SSD_NCHUNK)
    wx = _chunk_cols(w_in[:, SSD_D_INNER:SSD_D_INNER + SSD_CONV_DIM].astype(BF16), SSD_NCHUNK)
    lane_pad = LANES - SSD_HEADS
    wdt = jnp.pad(w_in[:, SSD_D_INNER + SSD_CONV_DIM:], ((0, 0), (0, lane_pad))).astype(BF16)
    pad_heads = lambda v: jnp.pad(v, (0, lane_pad))[None, :]
    z, xbc, dt = _ssd_proj(x2.reshape(B, S, D), norm_mix_w[1][None, :], wz, wx, wdt,
                           ssd_conv_w[0], ssd_conv_b[0][None, :], pad_heads(ssd_dt_bias[0]), tm=256)
    tril = jnp.asarray(np.tril(np.ones((SSD_CHUNK, SSD_CHUNK), np.float32)), BF16)
    dskip = jnp.repeat(ssd_d[0], SSD_HEAD_DIM)[None, :]
    y = _ssd_scan(xbc, dt, z, pad_heads(ssd_a_log[0]), dskip, ssd_norm_w[0][None, :], tril,
                  jnp.asarray(head_expand, BF16))

    wup, wdn = mlp_weights(1)
    x4 = _mix_out_mlp(x2, y.reshape(T, SSD_D_INNER), ssd_w_out[0].astype(BF16),
                      norm_mlp_w[1][None, :], wup, wdn, tm=512)
    return x4.reshape(B, S, D)
```

```python
import functools
import math

import numpy as np
import jax
import jax.numpy as jnp
from jax import lax
from jax.experimental import pallas as pl
from jax.experimental.pallas import tpu as pltpu

F32 = jnp.float32
BF16 = jnp.bfloat16

D_MODEL = 1024
EPS = 1e-6

NSA_HEADS = 16
NSA_GROUPS = 4
NSA_REP = NSA_HEADS // NSA_GROUPS
HEAD_DIM = 64
NSA_Q_DIM = NSA_HEADS * HEAD_DIM
NSA_KV_DIM = NSA_GROUPS * HEAD_DIM
CMP_BLOCK = 32
CMP_STRIDE = 16
CMP_HIDDEN = 256
SEL_BLOCK = 64
SEL_TOP_N = 16
WINDOW = 512
REL_BUCKETS = 32
REL_MAX_DIST = 128

SSD_D_INNER = 2 * D_MODEL
SSD_HEAD_DIM = 64
SSD_HEADS = SSD_D_INNER // SSD_HEAD_DIM
SSD_GROUPS = 8
SSD_STATE = 128
SSD_CONV = 4
SSD_CHUNK = 128
SSD_CONV_DIM = SSD_D_INNER + 2 * SSD_GROUPS * SSD_STATE
MLP_HIDDEN = 4 * D_MODEL

LANES = 128
SUBLANES = 8
VMEM_LIMIT_BYTES = 56 * 1024 * 1024

VT_BLOCK = 128
TQC = 512
TQA = 256
TKA = 256
N_CMP_PAD = 128
N_SEL_PAD = 128
MASKED = -1e30
LOG2E = math.log2(math.e)
SCORE_BIG = 3e38
N_SEL_VARIANTS = 3
N_WIN_TILES = WINDOW // TKA + 1

NT_DIMS = (((1,), (1,)), ((), ()))


def _dot(a, b):
    return jnp.dot(a, b, preferred_element_type=F32)


def _dot_nt(a, b):
    return lax.dot_general(a, b, NT_DIMS, preferred_element_type=F32)


def _split3(x):
    hi = x.astype(BF16)
    r1 = x - hi.astype(F32)
    mid = r1.astype(BF16)
    lo = (r1 - mid.astype(F32)).astype(BF16)
    return hi, mid, lo


def _rms_rows(x, w):
    return x * lax.rsqrt(jnp.mean(x * x, axis=-1, keepdims=True) + EPS) * w


def _silu(x):
    h = 0.5 * x
    return h + h * jnp.tanh(h)


def _const_spec(shape):
    nd = len(shape)
    return pl.BlockSpec(shape, lambda *_: (0,) * nd, pipeline_mode=pl.Buffered(1))


def _compiler_params(semantics):
    return pltpu.CompilerParams(dimension_semantics=semantics, vmem_limit_bytes=VMEM_LIMIT_BYTES)


def _nsa_proj_kernel(x_ref, blk_ref, nw_ref, wq_ref, wk_ref, wvt_ref, wgt_ref, qg_ref, kg_ref, seg_ref,
                     q_ref, kc_ref, vc_ref, k_ref, vt_ref, gate_ref):
    tm = x_ref.shape[1]
    h = _rms_rows(x_ref[0], nw_ref[...]).astype(BF16)
    seg = seg_ref[...]

    def head_rms(y, gain):
        sq = y * y
        hi = sq.astype(BF16)
        lo = (sq - hi.astype(F32)).astype(BF16)
        ss = _dot(hi, seg) + _dot(lo, seg)
        return y * lax.rsqrt(ss * (1.0 / HEAD_DIM) + EPS) * gain

    for c in range(NSA_Q_DIM // 256):
        y = _dot(h, wq_ref[c])
        y = (head_rms(y, qg_ref[...]) * (HEAD_DIM ** -0.5 * LOG2E)).astype(BF16)
        for r in range(4):
            q_ref[0, 4 * c + r] = y[:, 64 * r:64 * r + 64]

    for c, raw_ref in enumerate((kc_ref, vc_ref)):
        y = _dot(h, wk_ref[c])
        for g in range(NSA_GROUPS):
            raw_ref[0, g] = y[:, 64 * g:64 * g + 64]
    for br in range(2):
        y = head_rms(_dot(h, wk_ref[2 + br]), kg_ref[...]).astype(BF16)
        tag = blk_ref[...] if br == 0 else jnp.zeros(blk_ref.shape, BF16)
        for g in range(NSA_GROUPS):
            k_ref[0, g, br, :, 0:HEAD_DIM] = y[:, 64 * g:64 * g + 64]
            k_ref[0, g, br, :, HEAD_DIM:2 * HEAD_DIM] = tag
    for br in range(2):
        yt = _dot_nt(wvt_ref[br], h).astype(BF16)
        for g in range(NSA_GROUPS):
            for kb in range(tm // VT_BLOCK):
                vt_ref[0, g, br, kb] = yt[64 * g:64 * g + 64, VT_BLOCK * kb:VT_BLOCK * (kb + 1)]

    gate_ref[0] = jax.nn.sigmoid(_dot_nt(wgt_ref[...], h))


def _nsa_proj(x, key_block, nw, wq, wk, wvt, wgt, qg, kg, seg, tm):
    B, S, D = x.shape
    G = NSA_GROUPS
    raw_block = pl.BlockSpec((1, G, tm, HEAD_DIM), lambda b, i: (b, 0, i, 0))
    raw_shape = jax.ShapeDtypeStruct((B, G, S, HEAD_DIM), F32)
    consts = (nw, wq, wk, wvt, wgt, qg, kg, seg)
    return pl.pallas_call(
        _nsa_proj_kernel,
        grid=(B, S // tm),
        in_specs=[pl.BlockSpec((1, tm, D), lambda b, i: (b, i, 0)),
                  pl.BlockSpec((tm, HEAD_DIM), lambda b, i: (i, 0))] + [_const_spec(c.shape) for c in consts],
        out_specs=[
            pl.BlockSpec((1, NSA_HEADS, tm, HEAD_DIM), lambda b, i: (b, 0, i, 0)),
            raw_block, raw_block,
            pl.BlockSpec((1, G, 2, tm, 2 * HEAD_DIM), lambda b, i: (b, 0, 0, i, 0)),
            pl.BlockSpec((1, G, 2, tm // VT_BLOCK, HEAD_DIM, VT_BLOCK), lambda b, i: (b, 0, 0, i, 0, 0)),
            pl.BlockSpec((1, G * 16, tm), lambda b, i: (b, 0, i)),
        ],
        out_shape=[
            jax.ShapeDtypeStruct((B, NSA_HEADS, S, HEAD_DIM), BF16),
            raw_shape, raw_shape,
            jax.ShapeDtypeStruct((B, G, 2, S, 2 * HEAD_DIM), BF16),
            jax.ShapeDtypeStruct((B, G, 2, S // VT_BLOCK, HEAD_DIM, VT_BLOCK), BF16),
            jax.ShapeDtypeStruct((B, G * 16, S), F32),
        ],
        compiler_params=_compiler_params(("parallel", "parallel")),
        name="nsa_proj",
    )(x, key_block, *consts)


def _nsa_compress_kernel(kc_ref, vc_ref, pek_ref, pev_ref, w1k_ref, w1v_ref, w2k_ref, w2vt_ref, kg_ref,
                         kcmp_ref, vcmp_ref):
    half = CMP_BLOCK // 2

    def hidden(src_ref, pe_ref, w1_ref):
        acc_a = jnp.zeros((N_CMP_PAD, CMP_HIDDEN), F32)
        acc_b = jnp.zeros((N_CMP_PAD, CMP_HIDDEN), F32)
        for l in range(half):
            rows = src_ref[0, 0, pl.ds(l, N_CMP_PAD, stride=CMP_STRIDE), :]
            acc_a += _dot((rows + pe_ref[l:l + 1, :]).astype(BF16), w1_ref[l])
            acc_b += _dot((rows + pe_ref[half + l:half + l + 1, :]).astype(BF16), w1_ref[half + l])
        hid = acc_a + pltpu.roll(acc_b, N_CMP_PAD - 1, axis=0)
        return _silu(hid).astype(BF16)

    k = _dot(hidden(kc_ref, pek_ref, w1k_ref), w2k_ref[...])
    kcmp_ref[0, 0] = _rms_rows(k, kg_ref[...]).astype(BF16)
    vcmp_ref[0, 0] = _dot_nt(w2vt_ref[...], hidden(vc_ref, pev_ref, w1v_ref)).astype(BF16)


def _nsa_compress(kc, vc, pek, pev, w1k, w1v, w2k, w2vt, kg):
    B, G, S, Dh = kc.shape
    src = pl.BlockSpec((1, 1, S, Dh), lambda b, g: (b, g, 0, 0))
    consts = (pek, pev, w1k, w1v, w2k, w2vt, kg)
    return pl.pallas_call(
        _nsa_compress_kernel,
        grid=(B, G),
        in_specs=[src, src] + [_const_spec(c.shape) for c in consts],
        out_specs=[pl.BlockSpec((1, 1, N_CMP_PAD, Dh), lambda b, g: (b, g, 0, 0)),
                   pl.BlockSpec((1, 1, Dh, N_CMP_PAD), lambda b, g: (b, g, 0, 0))],
        out_shape=[jax.ShapeDtypeStruct((B, G, N_CMP_PAD, Dh), BF16),
                   jax.ShapeDtypeStruct((B, G, Dh, N_CMP_PAD), BF16)],
        compiler_params=_compiler_params(("parallel", "parallel")),
        name="nsa_compress",
    )(kc, vc, *consts)


def _bias_table_kernel(table_ref, code_ref, out_ref):
    h = pl.program_id(0)
    code = code_ref[...]
    acc = jnp.full(code.shape, MASKED, F32)
    for b in range(REL_BUCKETS):
        acc = jnp.where(code == b, table_ref[b, h] * LOG2E, acc)
    out_ref[0] = acc


def _bias_table(rel_table, code, rows_per_step):
    rows, lanes = code.shape
    return pl.pallas_call(
        _bias_table_kernel,
        grid=(NSA_HEADS, rows // rows_per_step),
        in_specs=[
            pl.BlockSpec(memory_space=pltpu.SMEM),
            pl.BlockSpec((rows_per_step, lanes), lambda h, i: (i, 0)),
        ],
        out_specs=pl.BlockSpec((1, rows_per_step, lanes), lambda h, i: (h, i, 0)),
        out_shape=jax.ShapeDtypeStruct((NSA_HEADS, rows, lanes), F32),
        compiler_params=_compiler_params(("parallel", "parallel")),
        name="rel_bias_table",
    )(rel_table, code)


def _nsa_cmp_select_kernel(q_ref, gate_ref, kcmp_ref, vcmpt_ref, bias_ref, ovl_ref, o_ref, qa_ref):
    i = pl.program_id(2)
    kcmp = kcmp_ref[0, 0]
    vcmpt = vcmpt_ref[0, 0]

    scores = [_dot_nt(kcmp, q_ref[0, r]) for r in range(NSA_REP)]
    probs = []
    for r in range(NSA_REP):
        s = scores[r] + bias_ref[r, 0]
        m = jnp.max(s, axis=0, keepdims=True)
        m = jnp.where(m > 0.5 * MASKED, m, 0.0)
        e = jnp.exp2(s - m)
        den = jnp.sum(e, axis=0, keepdims=True)
        probs.append(e / jnp.where(den > 0.0, den, 1.0))
    outs = [_dot(vcmpt, p.astype(BF16)) for p in probs]

    ph, pm, plo = _split3(probs[0] + probs[1] + probs[2] + probs[3])
    ovl = ovl_ref[...]
    n_sel = 32
    imp = (_dot(ovl, ph) + _dot(ovl, pm) + _dot(ovl, plo))[0:n_sel]
    tok = i * TQC + lax.broadcasted_iota(jnp.int32, (n_sel, TQC), 1)
    blk = lax.broadcasted_iota(jnp.int32, (n_sel, TQC), 0)
    cur = tok // SEL_BLOCK
    forced = (blk == 0) | (blk == cur) | (blk == cur - 1)
    valid = blk * SEL_BLOCK <= tok
    score = jnp.where(forced, SCORE_BIG, imp)
    score = jnp.where(valid, score, -SCORE_BIG)
    rank = jnp.zeros((n_sel, TQC), F32)
    for jp in range(n_sel):
        other = score[jp:jp + 1, :]
        beats = (other > score) | ((other == score) & (blk > jp))
        rank += beats.astype(F32)
    blk_mask = jnp.where(rank < SEL_TOP_N, 0.0, MASKED)
    blk_mask = jnp.concatenate([blk_mask, jnp.zeros((HEAD_DIM - n_sel, TQC), F32)], axis=0).T.astype(BF16)
    for r in range(NSA_REP):
        qa_ref[0, r, :, 0:HEAD_DIM] = q_ref[0, r]
        qa_ref[0, r, :, HEAD_DIM:2 * HEAD_DIM] = blk_mask

    gate = gate_ref[0]
    for r in range(NSA_REP):
        o_ref[0, :, r * HEAD_DIM:(r + 1) * HEAD_DIM] = (outs[r] * gate[r:r + 1]).T


def _nsa_cmp_select(q, gates, kcmp, vcmpt, bcmp, ovl):
    B, H, S, Dh = q.shape
    G = NSA_GROUPS
    return pl.pallas_call(
        _nsa_cmp_select_kernel,
        grid=(G, B, S // TQC),
        in_specs=[
            pl.BlockSpec((1, NSA_REP, TQC, Dh), lambda g, b, i: (b, g, i, 0)),
            pl.BlockSpec((1, 16, TQC), lambda g, b, i: (b, g, i)),
            pl.BlockSpec((1, 1, N_CMP_PAD, Dh), lambda g, b, i: (b, g, 0, 0)),
            pl.BlockSpec((1, 1, Dh, N_CMP_PAD), lambda g, b, i: (b, g, 0, 0)),
            pl.BlockSpec((NSA_REP, 1, N_CMP_PAD, TQC), lambda g, b, i: (g, i, 0, 0)),
            _const_spec(ovl.shape),
        ],
        out_specs=[
            pl.BlockSpec((1, TQC, NSA_REP * Dh), lambda g, b, i: (b, i, g)),
            pl.BlockSpec((1, NSA_REP, TQC, 2 * Dh), lambda g, b, i: (b, g, i, 0)),
        ],
        out_shape=[
            jax.ShapeDtypeStruct((B, S, H * Dh), F32),
            jax.ShapeDtypeStruct((B, H, S, 2 * Dh), BF16),
        ],
        compiler_params=_compiler_params(("parallel", "parallel", "arbitrary")),
        name="nsa_cmp_select",
    )(q, gates, kcmp, vcmpt, bcmp, ovl)


def _nsa_attn_kernel(q_ref, gate_ref, k_ref, vt_ref, ocmp_ref, bias_ref,
                     o_ref, s0_ref, s1_ref, p0_ref, p1_ref, a0_ref, a1_ref, m_ref, l_ref, acc_ref):
    i = pl.program_id(2)
    n_sel = i + 1
    n_win = jnp.minimum(N_WIN_TILES, n_sel)
    n_tasks = n_sel + n_win
    s_bufs, p_bufs, a_bufs = (s0_ref, s1_ref), (p0_ref, p1_ref), (a0_ref, a1_ref)
    masked_variant = N_SEL_VARIANTS + N_WIN_TILES

    def task(t):
        is_win = t >= n_sel
        c = jnp.clip(jnp.where(is_win, i - (t - n_sel), t), 0, i)
        behind = i - c
        variant = jnp.where(is_win, N_SEL_VARIANTS + behind, jnp.minimum(behind, N_SEL_VARIANTS - 1))
        variant = jnp.where(t < n_tasks, variant, masked_variant)
        return is_win.astype(jnp.int32), c, variant

    def score_stage(t, slot):
        br, c, variant = task(t)
        k0 = pl.multiple_of(c * TKA, TKA)
        k = k_ref[0, 0, br, pl.ds(k0, TKA), :]
        scores = [_dot_nt(k, q_ref[0, r]) for r in range(NSA_REP)]
        for r in range(NSA_REP):
            s_bufs[slot][r] = scores[r] + bias_ref[r, variant]

    def softmax_stage(t, slot):
        br, _, _ = task(t)
        for r in range(NSA_REP):
            m_old = m_ref[br, r]
            m_new = jnp.maximum(m_old, jnp.max(s_bufs[slot][r], axis=0, keepdims=True))
            alpha = jnp.exp2(m_old - m_new)
            p = jnp.exp2(s_bufs[slot][r] - m_new)
            l_ref[br, r] = alpha * l_ref[br, r] + jnp.sum(p, axis=0, keepdims=True)
            m_ref[br, r] = m_new
            a_bufs[slot][r] = alpha
            p_bufs[slot][r] = p.astype(BF16)

    def value_stage(t, slot):
        br, c, _ = task(t)
        blocks = TKA // VT_BLOCK
        vt = jnp.concatenate([vt_ref[0, 0, br, blocks * c + u] for u in range(blocks)], axis=1)
        outs = [_dot(vt, p_bufs[slot][r]) for r in range(NSA_REP)]
        for r in range(NSA_REP):
            acc_ref[br, r] = a_bufs[slot][r] * acc_ref[br, r] + outs[r]

    m_ref[...] = jnp.full(m_ref.shape, MASKED, F32)
    l_ref[...] = jnp.zeros(l_ref.shape, F32)
    acc_ref[...] = jnp.zeros(acc_ref.shape, F32)
    p1_ref[...] = jnp.zeros(p1_ref.shape, BF16)
    a1_ref[...] = jnp.ones(a1_ref.shape, F32)
    score_stage(0, 0)

    def trip(u, carry):
        t = 2 * u
        value_stage(jnp.maximum(t - 1, 0), 1)
        score_stage(t + 1, 1)
        softmax_stage(t, 0)
        value_stage(t, 0)
        score_stage(t + 2, 0)
        softmax_stage(t + 1, 1)
        return carry

    n_trips = (n_tasks + 1) // 2
    lax.fori_loop(0, n_trips, trip, 0)
    value_stage(2 * n_trips - 1, 1)

    gate = gate_ref[0]
    for r in range(NSA_REP):
        o_t = (gate[4 + r:5 + r] * (acc_ref[0, r] / l_ref[0, r])
               + gate[8 + r:9 + r] * (acc_ref[1, r] / l_ref[1, r]))
        cols = slice(r * HEAD_DIM, (r + 1) * HEAD_DIM)
        o_ref[0, :, cols] = (ocmp_ref[0, :, cols] + o_t.T).astype(BF16)


def _nsa_attn(q_aug, gates, k_sw, vt_sw, o_cmp, bias):
    B, H, S, Dq = q_aug.shape
    Dh = HEAD_DIM
    G = NSA_GROUPS
    n_vt = S // VT_BLOCK
    n_var = bias.shape[1]
    return pl.pallas_call(
        _nsa_attn_kernel,
        grid=(G, B, S // TQA),
        in_specs=[
            pl.BlockSpec((1, NSA_REP, TQA, Dq), lambda g, b, i: (b, g, i, 0)),
            pl.BlockSpec((1, 16, TQA), lambda g, b, i: (b, g, i)),
            pl.BlockSpec((1, 1, 2, S, Dq), lambda g, b, i: (b, g, 0, 0, 0)),
            pl.BlockSpec((1, 1, 2, n_vt, Dh, VT_BLOCK), lambda g, b, i: (b, g, 0, 0, 0, 0)),
            pl.BlockSpec((1, TQA, NSA_REP * Dh), lambda g, b, i: (b, i, g)),
            pl.BlockSpec((NSA_REP, n_var, TKA, TQA), lambda g, b, i: (g, 0, 0, 0), pipeline_mode=pl.Buffered(1)),
        ],
        out_specs=pl.BlockSpec((1, TQA, NSA_REP * Dh), lambda g, b, i: (b, i, g)),
        out_shape=jax.ShapeDtypeStruct((B, S, H * Dh), BF16),
        scratch_shapes=[
            pltpu.VMEM((NSA_REP, TKA, TQA), F32),
            pltpu.VMEM((NSA_REP, TKA, TQA), F32),
            pltpu.VMEM((NSA_REP, TKA, TQA), BF16),
            pltpu.VMEM((NSA_REP, TKA, TQA), BF16),
            pltpu.VMEM((NSA_REP, 1, TQA), F32),
            pltpu.VMEM((NSA_REP, 1, TQA), F32),
            pltpu.VMEM((2, NSA_REP, 1, TQA), F32),
            pltpu.VMEM((2, NSA_REP, 1, TQA), F32),
            pltpu.VMEM((2, NSA_REP, Dh, TQA), F32),
        ],
        compiler_params=_compiler_params(("parallel", "parallel", "arbitrary")),
        name="nsa_attn",
    )(q_aug, gates, k_sw, vt_sw, o_cmp, bias)


MLP_CHUNK = 1024


def _mix_out_mlp_kernel(x_ref, a_ref, wo_ref, nw_ref, wup_ref, wdn_ref, o_ref):
    x1 = x_ref[...] + _dot(a_ref[...], wo_ref[...])
    h = _rms_rows(x1, nw_ref[...]).astype(BF16)

    acc = None
    for c in range(wup_ref.shape[0]):
        u = jnp.maximum(_dot(h, wup_ref[c]), 0.0)
        d = _dot((u * u).astype(BF16), wdn_ref[c])
        acc = d if acc is None else acc + d
    o_ref[...] = x1 + acc


def _mix_out_mlp(x, a, wo, nw, wup, wdn, tm):
    T, D = x.shape
    K = a.shape[1]
    return pl.pallas_call(
        _mix_out_mlp_kernel,
        grid=(T // tm,),
        in_specs=[
            pl.BlockSpec((tm, D), lambda i: (i, 0)),
            pl.BlockSpec((tm, K), lambda i: (i, 0)),
            _const_spec(wo.shape), _const_spec(nw.shape), _const_spec(wup.shape), _const_spec(wdn.shape),
        ],
        out_specs=pl.BlockSpec((tm, D), lambda i: (i, 0)),
        out_shape=jax.ShapeDtypeStruct((T, D), F32),
        compiler_params=_compiler_params(("parallel",)),
        name="mix_out_mlp",
    )(x, a, wo, nw, wup, wdn)


SSD_NCHUNK = 512


def _ssd_proj_kernel(x_ref, nw_ref, wz_ref, wx_ref, wdt_ref, cw_ref, cb_ref, dtb_ref,
                     z_ref, xbc_ref, dt_ref, buf_ref):
    i = pl.program_id(1)
    tm = x_ref.shape[1]
    h = _rms_rows(x_ref[0], nw_ref[...]).astype(BF16)

    for c in range(wz_ref.shape[0]):
        z_ref[0, :, c * SSD_NCHUNK:(c + 1) * SSD_NCHUNK] = _dot(h, wz_ref[c])

    @pl.when(i == 0)
    def _():
        buf_ref[0:SUBLANES, :] = jnp.zeros((SUBLANES, buf_ref.shape[1]), F32)

    @pl.when(i > 0)
    def _():
        buf_ref[0:SUBLANES, :] = buf_ref[tm:tm + SUBLANES, :]

    for c in range(wx_ref.shape[0]):
        cols = slice(c * SSD_NCHUNK, (c + 1) * SSD_NCHUNK)
        buf_ref[SUBLANES:SUBLANES + tm, cols] = _dot(h, wx_ref[c])
        acc = cb_ref[:, cols] + buf_ref[SUBLANES:SUBLANES + tm, cols] * cw_ref[SSD_CONV - 1:SSD_CONV, cols]
        for k in range(SSD_CONV - 1):
            off = SUBLANES - (SSD_CONV - 1) + k
            acc += buf_ref[off:off + tm, cols] * cw_ref[k:k + 1, cols]
        xbc_ref[0, :, cols] = _silu(acc)

    dt_ref[0] = jax.nn.softplus(_dot(h, wdt_ref[...]) + dtb_ref[...])


def _ssd_proj(x, nw, wz, wx, wdt, cw, cb, dtb, tm):
    B, S, D = x.shape
    consts = (nw, wz, wx, wdt, cw, cb, dtb)
    return pl.pallas_call(
        _ssd_proj_kernel,
        grid=(B, S // tm),
        in_specs=[pl.BlockSpec((1, tm, D), lambda b, i: (b, i, 0))] + [_const_spec(c.shape) for c in consts],
        out_specs=[
            pl.BlockSpec((1, tm, SSD_D_INNER), lambda b, i: (b, i, 0)),
            pl.BlockSpec((1, tm, SSD_CONV_DIM), lambda b, i: (b, i, 0)),
            pl.BlockSpec((1, tm, LANES), lambda b, i: (b, i, 0)),
        ],
        out_shape=[
            jax.ShapeDtypeStruct((B, S, SSD_D_INNER), F32),
            jax.ShapeDtypeStruct((B, S, SSD_CONV_DIM), F32),
            jax.ShapeDtypeStruct((B, S, LANES), F32),
        ],
        scratch_shapes=[pltpu.VMEM((tm + SUBLANES, SSD_CONV_DIM), F32)],
        compiler_params=_compiler_params(("parallel", "arbitrary")),
        name="ssd_proj",
    )(x, *consts)


def _ssd_scan_kernel(xbc_ref, dt_ref, z_ref, alog_ref, dskip_ref, nw_ref, tril_ref, hexp_ref,
                     y_ref, state_ref):
    L = SSD_CHUNK
    P = SSD_HEAD_DIM
    N = SSD_STATE
    hpg = SSD_HEADS // SSD_GROUPS
    gw = hpg * P

    @pl.when(pl.program_id(1) == 0)
    def _():
        state_ref[...] = jnp.zeros(state_ref.shape, F32)

    dt = dt_ref[0]
    a = dt * (-jnp.exp(alog_ref[...]))
    tril = tril_ref[...]
    ah, am, al = _split3(a)
    acs = _dot(tril, ah) + _dot(tril, am) + _dot(tril, al)
    acs_t = acs.T
    dt_t = dt.T
    e_acs = jnp.exp(acs)
    eh, em, el = _split3(e_acs)
    w_t = jnp.exp(acs_t[:, L - 1:L] - acs_t) * dt_t
    causal = (lax.broadcasted_iota(jnp.int32, (L, L), 0) >= lax.broadcasted_iota(jnp.int32, (L, L), 1))
    head_of_lane = lax.broadcasted_iota(jnp.int32, (L, gw), 1) // P

    for g in range(SSD_GROUPS):
        b_off = SSD_D_INNER + g * N
        c_off = SSD_D_INNER + SSD_GROUPS * N + g * N
        bm = xbc_ref[0, :, b_off:b_off + N]
        cm = xbc_ref[0, :, c_off:c_off + N].astype(BF16)
        cb = _dot_nt(cm, bm.astype(BF16))
        bm_t = bm.T
        xg = xbc_ref[0, :, g * gw:(g + 1) * gw]
        hexp = hexp_ref[g]
        e_exp = _dot(eh, hexp) + _dot(em, hexp) + _dot(el, hexp)
        st = state_ref[g]
        y = _dot(cm, st.astype(BF16)) * e_exp
        st = st * e_exp[L - 1:L, :]
        for j in range(hpg):
            hd = g * hpg + j
            xj = jnp.where(head_of_lane == j, xg, 0.0).astype(BF16)
            diff = acs[:, hd:hd + 1] - acs_t[hd:hd + 1, :]
            decay = jnp.exp(jnp.where(causal, diff, -jnp.inf))
            y = y + _dot((cb * decay * dt_t[hd:hd + 1, :]).astype(BF16), xj)
            st = st + _dot((bm_t * w_t[hd:hd + 1, :]).astype(BF16), xj)
        state_ref[g] = st
        cols = slice(g * gw, (g + 1) * gw)
        zg = z_ref[0, :, cols]
        y = (y + xg * dskip_ref[:, cols]) * _silu(zg)
        y_ref[0, :, cols] = _rms_rows(y, nw_ref[:, cols]).astype(BF16)


def _ssd_scan(xbc, dt, z, alog, dskip, nw, tril, hexp):
    B, S, _ = xbc.shape
    L = SSD_CHUNK
    consts = (alog, dskip, nw, tril, hexp)
    return pl.pallas_call(
        _ssd_scan_kernel,
        grid=(B, S // L),
        in_specs=[
            pl.BlockSpec((1, L, SSD_CONV_DIM), lambda b, c: (b, c, 0)),
            pl.BlockSpec((1, L, LANES), lambda b, c: (b, c, 0)),
            pl.BlockSpec((1, L, SSD_D_INNER), lambda b, c: (b, c, 0)),
        ] + [_const_spec(c.shape) for c in consts],
        out_specs=pl.BlockSpec((1, L, SSD_D_INNER), lambda b, c: (b, c, 0)),
        out_shape=jax.ShapeDtypeStruct((B, S, SSD_D_INNER), BF16),
        scratch_shapes=[pltpu.VMEM((SSD_GROUPS, SSD_STATE, SSD_D_INNER // SSD_GROUPS), F32)],
        compiler_params=_compiler_params(("parallel", "arbitrary")),
        name="ssd_scan",
    )(xbc, dt, z, *consts)


def _rel_bucket_np(dist):
    max_exact = REL_BUCKETS // 2
    d = np.maximum(dist, 0)
    df = np.maximum(d, 1).astype(np.float32)
    large = max_exact + (np.log(df / np.float32(max_exact)) / np.float32(math.log(REL_MAX_DIST / max_exact))
                         * np.float32(REL_BUCKETS - max_exact)).astype(np.int32)
    large = np.minimum(large, REL_BUCKETS - 1)
    return np.where(d < max_exact, d, large).astype(np.int32)


@functools.lru_cache(maxsize=None)
def _position_tables(S):
    n_cmp = (S - CMP_BLOCK) // CMP_STRIDE + 1
    n_sel = S // SEL_BLOCK

    def code(dist, ok):
        return np.where(ok, _rel_bucket_np(dist), -1).astype(np.int32).reshape(-1, dist.shape[-1])

    n = np.arange(N_CMP_PAD)[None, :, None]
    t = np.arange(S // TQC)[:, None, None] * TQC + np.arange(TQC)[None, None, :]
    dist = t - (n * CMP_STRIDE + CMP_BLOCK - 1)
    cmp_code = code(dist, (dist >= 0) & (n < n_cmp))
    assert TQA == TKA and (N_SEL_VARIANTS - 1) * TQA - (TKA - 1) >= REL_MAX_DIST
    ti = np.arange(TQA)[None, :]
    j = np.arange(TKA)[:, None]
    tile_dist = [TQA * behind + ti - j for behind in range(N_WIN_TILES)]
    far = np.full((TKA, TQA), 2 * REL_MAX_DIST)
    sel_dist = np.stack(tile_dist[:N_SEL_VARIANTS - 1] + [far])
    win_dist = np.stack(tile_dist)
    attn_code = np.concatenate([code(sel_dist, sel_dist >= 0),
                                code(win_dist, (win_dist >= 0) & (win_dist < WINDOW)),
                                np.full((TKA, TQA), -1, np.int32)])
    jb = np.arange(N_SEL_PAD)[:, None]
    nn = np.arange(N_CMP_PAD)[None, :]
    overlap = ((nn * CMP_STRIDE < jb * SEL_BLOCK + SEL_BLOCK) & (nn * CMP_STRIDE + CMP_BLOCK > jb * SEL_BLOCK)
               & (nn < n_cmp) & (jb < n_sel)).astype(np.float32)
    key_block = (np.arange(S)[:, None] // SEL_BLOCK == np.arange(HEAD_DIM)[None, :]).astype(np.float32)
    hpg = SSD_HEADS // SSD_GROUPS
    lane_head = np.arange(hpg * SSD_HEAD_DIM)[None, None, :] // SSD_HEAD_DIM
    head_expand = (np.arange(LANES)[None, :, None]
                   == np.arange(SSD_GROUPS)[:, None, None] * hpg + lane_head).astype(np.float32)
    return cmp_code, attn_code, overlap, key_block, head_expand


def _chunk_cols(w, width):
    K, N = w.shape
    return w.reshape(K, N // width, width).transpose(1, 0, 2)


def kernel(x, norm_mix_w, norm_mlp_w, rel_table, nsa_w_in, nsa_q_gain, nsa_k_gain, cmp_pe_k, cmp_w1_k, cmp_w2_k, cmp_pe_v, cmp_w1_v, cmp_w2_v, nsa_w_out, ssd_w_in, ssd_conv_w, ssd_conv_b, ssd_dt_bias, ssd_a_log, ssd_d, ssd_norm_w, ssd_w_out, mlp_w_up, mlp_w_down):
    B, S, D = x.shape
    T = B * S
    G, R = NSA_GROUPS, NSA_REP
    cmp_code, attn_code, overlap, key_block, head_expand = _position_tables(S)

    w_in = nsa_w_in[0].astype(BF16)
    kv0 = NSA_Q_DIM
    kv_cols = lambda c: w_in[:, kv0 + c * NSA_KV_DIM:kv0 + (c + 1) * NSA_KV_DIM]
    wq = _chunk_cols(w_in[:, :NSA_Q_DIM], 256)
    wk = jnp.stack([kv_cols(0), kv_cols(1), kv_cols(2), kv_cols(4)])
    wvt = jnp.stack([kv_cols(3).T, kv_cols(5).T])
    wg = w_in[:, kv0 + 6 * NSA_KV_DIM:].reshape(D, 3, G, R).transpose(0, 2, 1, 3).reshape(D, G, 3 * R)
    wgt = jnp.pad(wg, ((0, 0), (0, 0), (0, 16 - 3 * R))).reshape(D, G * 16).T
    qg = jnp.tile(nsa_q_gain[0], 4)[None, :]
    kg = jnp.tile(nsa_k_gain[0], 4)[None, :]
    seg = jnp.asarray(np.kron(np.eye(4, dtype=np.float32), np.ones((HEAD_DIM, HEAD_DIM), np.float32)), BF16)
    q, kc, vc, k_sw, vt_sw, gates = _nsa_proj(
        x, jnp.asarray(key_block, BF16), norm_mix_w[0][None, :], wq, wk, wvt, wgt, qg, kg, seg, tm=512)

    kcmp, vcmpt = _nsa_compress(
        kc, vc, cmp_pe_k[0], cmp_pe_v[0], cmp_w1_k[0].astype(BF16), cmp_w1_v[0].astype(BF16),
        cmp_w2_k[0].astype(BF16), cmp_w2_v[0].astype(BF16).T, nsa_k_gain[0][None, :])

    bcmp = _bias_table(rel_table, jnp.asarray(cmp_code), N_CMP_PAD).reshape(NSA_HEADS, S // TQC, N_CMP_PAD, TQC)
    battn = _bias_table(rel_table, jnp.asarray(attn_code), TKA).reshape(NSA_HEADS, -1, TKA, TQA)
    o_cmp, q_aug = _nsa_cmp_select(q, gates, kcmp, vcmpt, bcmp, jnp.asarray(overlap, BF16))
    o = _nsa_attn(q_aug, gates, k_sw, vt_sw, o_cmp, battn)

    def mlp_weights(li):
        return (_chunk_cols(mlp_w_up[li].astype(BF16), MLP_CHUNK),
                mlp_w_down[li].astype(BF16).reshape(MLP_HIDDEN // MLP_CHUNK, MLP_CHUNK, D))

    wup, wdn = mlp_weights(0)
    x2 = _mix_out_mlp(x.reshape(T, D), o.reshape(T, NSA_Q_DIM), nsa_w_out[0].astype(BF16),
                      norm_mlp_w[0][None, :], wup, wdn, tm=512)

    w_in = ssd_w_in[0]
    wz = _chunk_cols(w_in[:, :SSD_D_INNER].astype(BF16), SSD_NCHUNK)
    wx = _chunk_cols(w_in[:, SSD_D_INNER:SSD_D_INNER + SSD_CONV_DIM].astype(BF16), SSD_NCHUNK)
    lane_pad = LANES - SSD_HEADS
    wdt = jnp.pad(w_in[:, SSD_D_INNER + SSD_CONV_DIM:], ((0, 0), (0, lane_pad))).astype(BF16)
    pad_heads = lambda v: jnp.pad(v, (0, lane_pad))[None, :]
    z, xbc, dt = _ssd_proj(x2.reshape(B, S, D), norm_mix_w[1][None, :], wz, wx, wdt,
                           ssd_conv_w[0], ssd_conv_b[0][None, :], pad_heads(ssd_dt_bias[0]), tm=256)
    tril = jnp.asarray(np.tril(np.ones((SSD_CHUNK, SSD_CHUNK), np.float32)), BF16)
    dskip = jnp.repeat(ssd_d[0], SSD_HEAD_DIM)[None, :]
    y = _ssd_scan(xbc, dt, z, pad_heads(ssd_a_log[0]), dskip, ssd_norm_w[0][None, :], tril,
                  jnp.asarray(head_expand, BF16))

    wup, wdn = mlp_weights(1)
    x4 = _mix_out_mlp(x2, y.reshape(T, SSD_D_INNER), ssd_w_out[0].astype(BF16),
                      norm_mlp_w[1][None, :], wup, wdn, tm=512)
    return x4.reshape(B, S, D)
```

```python
import functools
import math

import numpy as np
import jax
import jax.numpy as jnp
from jax import lax
from jax.experimental import pallas as pl
from jax.experimental.pallas import tpu as pltpu

F32 = jnp.float32
BF16 = jnp.bfloat16

D_MODEL = 1024
EPS = 1e-6

NSA_HEADS = 16
NSA_GROUPS = 4
NSA_REP = NSA_HEADS // NSA_GROUPS
HEAD_DIM = 64
NSA_Q_DIM = NSA_HEADS * HEAD_DIM
NSA_KV_DIM = NSA_GROUPS * HEAD_DIM
CMP_BLOCK = 32
CMP_STRIDE = 16
CMP_HIDDEN = 256
SEL_BLOCK = 64
SEL_TOP_N = 16
WINDOW = 512
REL_BUCKETS = 32
REL_MAX_DIST = 128

SSD_D_INNER = 2 * D_MODEL
SSD_HEAD_DIM = 64
SSD_HEADS = SSD_D_INNER // SSD_HEAD_DIM
SSD_GROUPS = 8
SSD_STATE = 128
SSD_CONV = 4
SSD_CHUNK = 128
SSD_CONV_DIM = SSD_D_INNER + 2 * SSD_GROUPS * SSD_STATE
MLP_HIDDEN = 4 * D_MODEL

LANES = 128
SUBLANES = 8
VMEM_LIMIT_BYTES = 56 * 1024 * 1024

VT_BLOCK = 128
TQC = 512
TQA = 256
TKA = 256
N_CMP_PAD = 128
N_SEL_PAD = 128
MASKED = -1e30
LOG2E = math.log2(math.e)
SCORE_BIG = 3e38
N_SEL_VARIANTS = 3
N_WIN_TILES = WINDOW // TKA + 1

NT_DIMS = (((1,), (1,)), ((), ()))


def _dot(a, b):
    return jnp.dot(a, b, preferred_element_type=F32)


def _dot_nt(a, b):
    return lax.dot_general(a, b, NT_DIMS, preferred_element_type=F32)


def _split3(x):
    hi = x.astype(BF16)
    r1 = x - hi.astype(F32)
    mid = r1.astype(BF16)
    lo = (r1 - mid.astype(F32)).astype(BF16)
    return hi, mid, lo


def _rms_rows(x, w):
    return x * lax.rsqrt(jnp.mean(x * x, axis=-1, keepdims=True) + EPS) * w


def _silu(x):
    h = 0.5 * x
    return h + h * jnp.tanh(h)


def _const_spec(shape):
    nd = len(shape)
    return pl.BlockSpec(shape, lambda *_: (0,) * nd, pipeline_mode=pl.Buffered(1))


def _compiler_params(semantics):
    return pltpu.CompilerParams(dimension_semantics=semantics, vmem_limit_bytes=VMEM_LIMIT_BYTES)


def _nsa_proj_kernel(x_ref, blk_ref, nw_ref, wq_ref, wk_ref, wvt_ref, wgt_ref, qg_ref, kg_ref, seg_ref,
                     q_ref, kc_ref, vc_ref, k_ref, vt_ref, gate_ref):
    tm = x_ref.shape[1]
    h = _rms_rows(x_ref[0], nw_ref[...]).astype(BF16)
    seg = seg_ref[...]

    def head_rms(y, gain):
        sq = y * y
        hi = sq.astype(BF16)
        lo = (sq - hi.astype(F32)).astype(BF16)
        ss = _dot(hi, seg) + _dot(lo, seg)
        return y * lax.rsqrt(ss * (1.0 / HEAD_DIM) + EPS) * gain

    def emit_q(c, y):
        y = (head_rms(y, qg_ref[...]) * (HEAD_DIM ** -0.5 * LOG2E)).astype(BF16)
        for r in range(4):
            q_ref[0, 4 * c + r] = y[:, 64 * r:64 * r + 64]

    def emit_raw(c, y):
        raw_ref = (kc_ref, vc_ref)[c]
        for g in range(NSA_GROUPS):
            raw_ref[0, g] = y[:, 64 * g:64 * g + 64]

    def emit_k(br, y):
        y = head_rms(y, kg_ref[...]).astype(BF16)
        tag = blk_ref[...] if br == 0 else jnp.zeros(blk_ref.shape, BF16)
        for g in range(NSA_GROUPS):
            k_ref[0, g, br, :, 0:HEAD_DIM] = y[:, 64 * g:64 * g + 64]
            k_ref[0, g, br, :, HEAD_DIM:2 * HEAD_DIM] = tag

    def emit_vt(br, yt):
        yt = yt.astype(BF16)
        for g in range(NSA_GROUPS):
            for kb in range(tm // VT_BLOCK):
                vt_ref[0, g, br, kb] = yt[64 * g:64 * g + 64, VT_BLOCK * kb:VT_BLOCK * (kb + 1)]

    def emit_gates(_, yt):
        gate_ref[0] = jax.nn.sigmoid(yt)

    stages = ([(lambda c=c: _dot(h, wq_ref[c]), emit_q, c) for c in range(NSA_Q_DIM // 256)]
              + [(lambda c=c: _dot(h, wk_ref[c]), emit_raw, c) for c in range(2)]
              + [(lambda br=br: _dot(h, wk_ref[2 + br]), emit_k, br) for br in range(2)]
              + [(lambda br=br: _dot_nt(wvt_ref[br], h), emit_vt, br) for br in range(2)]
              + [(lambda: _dot_nt(wgt_ref[...], h), emit_gates, 0)])
    pending = stages[0][0]()
    for n, (_, emit, arg) in enumerate(stages):
        y = pending
        if n + 1 < len(stages):
            pending = stages[n + 1][0]()
        emit(arg, y)


def _nsa_proj(x, key_block, nw, wq, wk, wvt, wgt, qg, kg, seg, tm):
    B, S, D = x.shape
    G = NSA_GROUPS
    raw_block = pl.BlockSpec((1, G, tm, HEAD_DIM), lambda b, i: (b, 0, i, 0))
    raw_shape = jax.ShapeDtypeStruct((B, G, S, HEAD_DIM), F32)
    consts = (nw, wq, wk, wvt, wgt, qg, kg, seg)
    return pl.pallas_call(
        _nsa_proj_kernel,
        grid=(B, S // tm),
        in_specs=[pl.BlockSpec((1, tm, D), lambda b, i: (b, i, 0)),
                  pl.BlockSpec((tm, HEAD_DIM), lambda b, i: (i, 0))] + [_const_spec(c.shape) for c in consts],
        out_specs=[
            pl.BlockSpec((1, NSA_HEADS, tm, HEAD_DIM), lambda b, i: (b, 0, i, 0)),
            raw_block, raw_block,
            pl.BlockSpec((1, G, 2, tm, 2 * HEAD_DIM), lambda b, i: (b, 0, 0, i, 0)),
            pl.BlockSpec((1, G, 2, tm // VT_BLOCK, HEAD_DIM, VT_BLOCK), lambda b, i: (b, 0, 0, i, 0, 0)),
            pl.BlockSpec((1, G * 16, tm), lambda b, i: (b, 0, i)),
        ],
        out_shape=[
            jax.ShapeDtypeStruct((B, NSA_HEADS, S, HEAD_DIM), BF16),
            raw_shape, raw_shape,
            jax.ShapeDtypeStruct((B, G, 2, S, 2 * HEAD_DIM), BF16),
            jax.ShapeDtypeStruct((B, G, 2, S // VT_BLOCK, HEAD_DIM, VT_BLOCK), BF16),
            jax.ShapeDtypeStruct((B, G * 16, S), F32),
        ],
        compiler_params=_compiler_params(("parallel", "parallel")),
        name="nsa_proj",
    )(x, key_block, *consts)


def _nsa_compress_kernel(kc_ref, vc_ref, pek_ref, pev_ref, w1k_ref, w1v_ref, w2k_ref, w2vt_ref, kg_ref,
                         kcmp_ref, vcmp_ref):
    half = CMP_BLOCK // 2

    def hidden(src_ref, pe_ref, w1_ref):
        acc_a = jnp.zeros((N_CMP_PAD, CMP_HIDDEN), F32)
        acc_b = jnp.zeros((N_CMP_PAD, CMP_HIDDEN), F32)
        for l in range(half):
            rows = src_ref[0, 0, pl.ds(l, N_CMP_PAD, stride=CMP_STRIDE), :]
            acc_a += _dot((rows + pe_ref[l:l + 1, :]).astype(BF16), w1_ref[l])
            acc_b += _dot((rows + pe_ref[half + l:half + l + 1, :]).astype(BF16), w1_ref[half + l])
        hid = acc_a + pltpu.roll(acc_b, N_CMP_PAD - 1, axis=0)
        return _silu(hid).astype(BF16)

    k = _dot(hidden(kc_ref, pek_ref, w1k_ref), w2k_ref[...])
    kcmp_ref[0, 0] = _rms_rows(k, kg_ref[...]).astype(BF16)
    vcmp_ref[0, 0] = _dot_nt(w2vt_ref[...], hidden(vc_ref, pev_ref, w1v_ref)).astype(BF16)


def _nsa_compress(kc, vc, pek, pev, w1k, w1v, w2k, w2vt, kg):
    B, G, S, Dh = kc.shape
    src = pl.BlockSpec((1, 1, S, Dh), lambda b, g: (b, g, 0, 0))
    consts = (pek, pev, w1k, w1v, w2k, w2vt, kg)
    return pl.pallas_call(
        _nsa_compress_kernel,
        grid=(B, G),
        in_specs=[src, src] + [_const_spec(c.shape) for c in consts],
        out_specs=[pl.BlockSpec((1, 1, N_CMP_PAD, Dh), lambda b, g: (b, g, 0, 0)),
                   pl.BlockSpec((1, 1, Dh, N_CMP_PAD), lambda b, g: (b, g, 0, 0))],
        out_shape=[jax.ShapeDtypeStruct((B, G, N_CMP_PAD, Dh), BF16),
                   jax.ShapeDtypeStruct((B, G, Dh, N_CMP_PAD), BF16)],
        compiler_params=_compiler_params(("parallel", "parallel")),
        name="nsa_compress",
    )(kc, vc, *consts)


def _bias_table_kernel(table_ref, code_ref, out_ref):
    h = pl.program_id(0)
    code = code_ref[...]
    acc = jnp.full(code.shape, MASKED, F32)
    for b in range(REL_BUCKETS):
        acc = jnp.where(code == b, table_ref[b, h] * LOG2E, acc)
    out_ref[0] = acc


def _bias_table(rel_table, code, rows_per_step):
    rows, lanes = code.shape
    return pl.pallas_call(
        _bias_table_kernel,
        grid=(NSA_HEADS, rows // rows_per_step),
        in_specs=[
            pl.BlockSpec(memory_space=pltpu.SMEM),
            pl.BlockSpec((rows_per_step, lanes), lambda h, i: (i, 0)),
        ],
        out_specs=pl.BlockSpec((1, rows_per_step, lanes), lambda h, i: (h, i, 0)),
        out_shape=jax.ShapeDtypeStruct((NSA_HEADS, rows, lanes), F32),
        compiler_params=_compiler_params(("parallel", "parallel")),
        name="rel_bias_table",
    )(rel_table, code)


def _nsa_cmp_select_kernel(q_ref, gate_ref, kcmp_ref, vcmpt_ref, bias_ref, ovl_ref, o_ref, qa_ref):
    i = pl.program_id(2)
    kcmp = kcmp_ref[0, 0]
    vcmpt = vcmpt_ref[0, 0]

    scores = [_dot_nt(kcmp, q_ref[0, r]) for r in range(NSA_REP)]
    probs = []
    for r in range(NSA_REP):
        s = scores[r] + bias_ref[r, 0]
        m = jnp.max(s, axis=0, keepdims=True)
        m = jnp.where(m > 0.5 * MASKED, m, 0.0)
        e = jnp.exp2(s - m)
        den = jnp.sum(e, axis=0, keepdims=True)
        probs.append(e / jnp.where(den > 0.0, den, 1.0))
    outs = [_dot(vcmpt, p.astype(BF16)) for p in probs]

    ph, pm, plo = _split3(probs[0] + probs[1] + probs[2] + probs[3])
    ovl = ovl_ref[...]
    n_sel = 32
    imp = (_dot(ovl, ph) + _dot(ovl, pm) + _dot(ovl, plo))[0:n_sel]
    tok = i * TQC + lax.broadcasted_iota(jnp.int32, (n_sel, TQC), 1)
    blk = lax.broadcasted_iota(jnp.int32, (n_sel, TQC), 0)
    cur = tok // SEL_BLOCK
    forced = (blk == 0) | (blk == cur) | (blk == cur - 1)
    valid = blk * SEL_BLOCK <= tok
    score = jnp.where(forced, SCORE_BIG, imp)
    score = jnp.where(valid, score, -SCORE_BIG)
    groups = [slice(SUBLANES * v, SUBLANES * (v + 1)) for v in range(n_sel // SUBLANES)]
    ranks = [jnp.zeros((SUBLANES, TQC), F32) for _ in groups]
    for jp in range(n_sel):
        other = score[jp:jp + 1, :]
        for v, rows in enumerate(groups):
            mine = score[rows]
            if rows.start > jp:
                beats = other >= mine
            elif rows.stop - 1 <= jp:
                beats = other > mine
            else:
                later = lax.broadcasted_iota(jnp.int32, (SUBLANES, TQC), 0) > jp - rows.start
                beats = (other > mine) | ((other == mine) & later)
            ranks[v] = ranks[v] + jnp.where(beats, 1.0, 0.0)
    rank = jnp.concatenate(ranks, axis=0)
    blk_mask = jnp.where(rank < SEL_TOP_N, 0.0, MASKED)
    blk_mask = jnp.concatenate([blk_mask, jnp.zeros((HEAD_DIM - n_sel, TQC), F32)], axis=0).T.astype(BF16)
    for r in range(NSA_REP):
        qa_ref[0, r, :, 0:HEAD_DIM] = q_ref[0, r]
        qa_ref[0, r, :, HEAD_DIM:2 * HEAD_DIM] = blk_mask

    gate = gate_ref[0]
    for r in range(NSA_REP):
        o_ref[0, :, r * HEAD_DIM:(r + 1) * HEAD_DIM] = (outs[r] * gate[r:r + 1]).T


def _nsa_cmp_select(q, gates, kcmp, vcmpt, bcmp, ovl):
    B, H, S, Dh = q.shape
    G = NSA_GROUPS
    return pl.pallas_call(
        _nsa_cmp_select_kernel,
        grid=(G, B, S // TQC),
        in_specs=[
            pl.BlockSpec((1, NSA_REP, TQC, Dh), lambda g, b, i: (b, g, i, 0)),
            pl.BlockSpec((1, 16, TQC), lambda g, b, i: (b, g, i)),
            pl.BlockSpec((1, 1, N_CMP_PAD, Dh), lambda g, b, i: (b, g, 0, 0)),
            pl.BlockSpec((1, 1, Dh, N_CMP_PAD), lambda g, b, i: (b, g, 0, 0)),
            pl.BlockSpec((NSA_REP, 1, N_CMP_PAD, TQC), lambda g, b, i: (g, i, 0, 0)),
            _const_spec(ovl.shape),
        ],
        out_specs=[
            pl.BlockSpec((1, TQC, NSA_REP * Dh), lambda g, b, i: (b, i, g)),
            pl.BlockSpec((1, NSA_REP, TQC, 2 * Dh), lambda g, b, i: (b, g, i, 0)),
        ],
        out_shape=[
            jax.ShapeDtypeStruct((B, S, H * Dh), F32),
            jax.ShapeDtypeStruct((B, H, S, 2 * Dh), BF16),
        ],
        compiler_params=_compiler_params(("parallel", "parallel", "arbitrary")),
        name="nsa_cmp_select",
    )(q, gates, kcmp, vcmpt, bcmp, ovl)


def _nsa_attn_kernel(q_ref, gate_ref, k_ref, vt_ref, ocmp_ref, bias_ref,
                     o_ref, s0_ref, s1_ref, x0_ref, x1_ref, p0_ref, p1_ref, a0_ref, a1_ref,
                     m_ref, l_ref, acc_ref):
    i = pl.program_id(2)
    last_step = pl.num_programs(2) - 1
    s_bufs, x_bufs = (s0_ref, s1_ref), (x0_ref, x1_ref)
    p_bufs, a_bufs = (p0_ref, p1_ref), (a0_ref, a1_ref)
    masked_variant = N_SEL_VARIANTS + N_WIN_TILES

    def n_tasks_of(step):
        n_sel = step + 1
        return n_sel, n_sel + jnp.minimum(N_WIN_TILES, n_sel)

    def task(t, step):
        n_sel, n_tasks = n_tasks_of(step)
        is_win = t >= n_sel
        c = jnp.clip(jnp.where(is_win, step - (t - n_sel), t), 0, step)
        behind = step - c
        variant = jnp.where(is_win, N_SEL_VARIANTS + behind, jnp.minimum(behind, N_SEL_VARIANTS - 1))
        variant = jnp.where(t < n_tasks, variant, masked_variant)
        return is_win.astype(jnp.int32), c, variant

    def score_stage(t, slot, step=i):
        br, c, variant = task(t, step)
        k0 = pl.multiple_of(c * TKA, TKA)
        q0 = pl.multiple_of(step * TQA, TQA)
        k = k_ref[0, 0, br, pl.ds(k0, TKA), :]
        for r in range(NSA_REP):
            s = _dot_nt(k, q_ref[0, r, pl.ds(q0, TQA), :]) + bias_ref[r, variant]
            s_bufs[slot][r] = s
            x_bufs[slot][r] = jnp.max(s, axis=0, keepdims=True)

    def softmax_stage(t, slot):
        br, _, _ = task(t, i)
        for r in range(NSA_REP):
            m_old = m_ref[br, r]
            m_new = jnp.maximum(m_old, x_bufs[slot][r])
            alpha = jnp.exp2(m_old - m_new)
            p = jnp.exp2(s_bufs[slot][r] - m_new)
            l_ref[br, r] = alpha * l_ref[br, r] + jnp.sum(p, axis=0, keepdims=True)
            m_ref[br, r] = m_new
            a_bufs[slot][r] = alpha
            p_bufs[slot][r] = p.astype(BF16)

    def value_stage(t, slot):
        br, c, _ = task(t, i)
        blocks = TKA // VT_BLOCK
        vt = jnp.concatenate([vt_ref[0, 0, br, blocks * c + u] for u in range(blocks)], axis=1)
        outs = [_dot(vt, p_bufs[slot][r]) for r in range(NSA_REP)]
        for r in range(NSA_REP):
            acc_ref[br, r] = a_bufs[slot][r] * acc_ref[br, r] + outs[r]

    m_ref[...] = jnp.full(m_ref.shape, MASKED, F32)
    l_ref[...] = jnp.zeros(l_ref.shape, F32)
    acc_ref[...] = jnp.zeros(acc_ref.shape, F32)
    p1_ref[...] = jnp.zeros(p1_ref.shape, BF16)
    a1_ref[...] = jnp.ones(a1_ref.shape, F32)

    @pl.when(i == 0)
    def _():
        score_stage(0, 0)

    def trip(u, carry):
        t = 2 * u
        value_stage(jnp.maximum(t - 1, 0), 1)
        score_stage(t + 1, 1)
        softmax_stage(t, 0)
        value_stage(t, 0)
        score_stage(t + 2, 0)
        softmax_stage(t + 1, 1)
        return carry

    n_trips = (n_tasks_of(i)[1] + 1) // 2
    lax.fori_loop(0, n_trips, trip, 0)
    value_stage(2 * n_trips - 1, 1)
    score_stage(0, 0, step=jnp.minimum(i + 1, last_step))

    gate = gate_ref[0]
    for r in range(NSA_REP):
        w_sel = gate[4 + r:5 + r] / l_ref[0, r]
        w_win = gate[8 + r:9 + r] / l_ref[1, r]
        o_t = w_sel * acc_ref[0, r] + w_win * acc_ref[1, r]
        cols = slice(r * HEAD_DIM, (r + 1) * HEAD_DIM)
        o_ref[0, :, cols] = (ocmp_ref[0, :, cols] + o_t.T).astype(BF16)


def _nsa_attn(q_aug, gates, k_sw, vt_sw, o_cmp, bias):
    B, H, S, Dq = q_aug.shape
    Dh = HEAD_DIM
    G = NSA_GROUPS
    n_vt = S // VT_BLOCK
    n_var = bias.shape[1]
    return pl.pallas_call(
        _nsa_attn_kernel,
        grid=(G, B, S // TQA),
        in_specs=[
            pl.BlockSpec((1, NSA_REP, S, Dq), lambda g, b, i: (b, g, 0, 0)),
            pl.BlockSpec((1, 16, TQA), lambda g, b, i: (b, g, i)),
            pl.BlockSpec((1, 1, 2, S, Dq), lambda g, b, i: (b, g, 0, 0, 0)),
            pl.BlockSpec((1, 1, 2, n_vt, Dh, VT_BLOCK), lambda g, b, i: (b, g, 0, 0, 0, 0)),
            pl.BlockSpec((1, TQA, NSA_REP * Dh), lambda g, b, i: (b, i, g)),
            pl.BlockSpec((NSA_REP, n_var, TKA, TQA), lambda g, b, i: (g, 0, 0, 0), pipeline_mode=pl.Buffered(1)),
        ],
        out_specs=pl.BlockSpec((1, TQA, NSA_REP * Dh), lambda g, b, i: (b, i, g)),
        out_shape=jax.ShapeDtypeStruct((B, S, H * Dh), BF16),
        scratch_shapes=[
            pltpu.VMEM((NSA_REP, TKA, TQA), F32),
            pltpu.VMEM((NSA_REP, TKA, TQA), F32),
            pltpu.VMEM((NSA_REP, 1, TQA), F32),
            pltpu.VMEM((NSA_REP, 1, TQA), F32),
            pltpu.VMEM((NSA_REP, TKA, TQA), BF16),
            pltpu.VMEM((NSA_REP, TKA, TQA), BF16),
            pltpu.VMEM((NSA_REP, 1, TQA), F32),
            pltpu.VMEM((NSA_REP, 1, TQA), F32),
            pltpu.VMEM((2, NSA_REP, 1, TQA), F32),
            pltpu.VMEM((2, NSA_REP, 1, TQA), F32),
            pltpu.VMEM((2, NSA_REP, Dh, TQA), F32),
        ],
        compiler_params=_compiler_params(("parallel", "parallel", "arbitrary")),
        name="nsa_attn",
    )(q_aug, gates, k_sw, vt_sw, o_cmp, bias)


MLP_CHUNK = 1024


def _mix_out_mlp_kernel(x_ref, a_ref, wo_ref, nw_ref, wup_ref, wdn_ref, o_ref):
    x1 = x_ref[...] + _dot(a_ref[...], wo_ref[...])
    h = _rms_rows(x1, nw_ref[...]).astype(BF16)

    acc = None
    for c in range(wup_ref.shape[0]):
        u = jnp.maximum(_dot(h, wup_ref[c]), 0.0)
        d = _dot((u * u).astype(BF16), wdn_ref[c])
        acc = d if acc is None else acc + d
    o_ref[...] = x1 + acc


def _mix_out_mlp(x, a, wo, nw, wup, wdn, tm):
    T, D = x.shape
    K = a.shape[1]
    return pl.pallas_call(
        _mix_out_mlp_kernel,
        grid=(T // tm,),
        in_specs=[
            pl.BlockSpec((tm, D), lambda i: (i, 0)),
            pl.BlockSpec((tm, K), lambda i: (i, 0)),
            _const_spec(wo.shape), _const_spec(nw.shape), _const_spec(wup.shape), _const_spec(wdn.shape),
        ],
        out_specs=pl.BlockSpec((tm, D), lambda i: (i, 0)),
        out_shape=jax.ShapeDtypeStruct((T, D), F32),
        compiler_params=_compiler_params(("parallel",)),
        name="mix_out_mlp",
    )(x, a, wo, nw, wup, wdn)


SSD_NCHUNK = 512


def _ssd_proj_kernel(x_ref, nw_ref, wz_ref, wx_ref, wdt_ref, cw_ref, cb_ref, dtb_ref,
                     z_ref, xbc_ref, dt_ref, buf_ref):
    i = pl.program_id(1)
    tm = x_ref.shape[1]
    h = _rms_rows(x_ref[0], nw_ref[...]).astype(BF16)

    for c in range(wz_ref.shape[0]):
        z_ref[0, :, c * SSD_NCHUNK:(c + 1) * SSD_NCHUNK] = _dot(h, wz_ref[c])

    @pl.when(i == 0)
    def _():
        buf_ref[0:SUBLANES, :] = jnp.zeros((SUBLANES, buf_ref.shape[1]), F32)

    @pl.when(i > 0)
    def _():
        buf_ref[0:SUBLANES, :] = buf_ref[tm:tm + SUBLANES, :]

    for c in range(wx_ref.shape[0]):
        cols = slice(c * SSD_NCHUNK, (c + 1) * SSD_NCHUNK)
        buf_ref[SUBLANES:SUBLANES + tm, cols] = _dot(h, wx_ref[c])
        acc = cb_ref[:, cols] + buf_ref[SUBLANES:SUBLANES + tm, cols] * cw_ref[SSD_CONV - 1:SSD_CONV, cols]
        for k in range(SSD_CONV - 1):
            off = SUBLANES - (SSD_CONV - 1) + k
            acc += buf_ref[off:off + tm, cols] * cw_ref[k:k + 1, cols]
        xbc_ref[0, :, cols] = _silu(acc)

    dt_ref[0] = jax.nn.softplus(_dot(h, wdt_ref[...]) + dtb_ref[...])


def _ssd_proj(x, nw, wz, wx, wdt, cw, cb, dtb, tm):
    B, S, D = x.shape
    consts = (nw, wz, wx, wdt, cw, cb, dtb)
    return pl.pallas_call(
        _ssd_proj_kernel,
        grid=(B, S // tm),
        in_specs=[pl.BlockSpec((1, tm, D), lambda b, i: (b, i, 0))] + [_const_spec(c.shape) for c in consts],
        out_specs=[
            pl.BlockSpec((1, tm, SSD_D_INNER), lambda b, i: (b, i, 0)),
            pl.BlockSpec((1, tm, SSD_CONV_DIM), lambda b, i: (b, i, 0)),
            pl.BlockSpec((1, tm, LANES), lambda b, i: (b, i, 0)),
        ],
        out_shape=[
            jax.ShapeDtypeStruct((B, S, SSD_D_INNER), F32),
            jax.ShapeDtypeStruct((B, S, SSD_CONV_DIM), F32),
            jax.ShapeDtypeStruct((B, S, LANES), F32),
        ],
        scratch_shapes=[pltpu.VMEM((tm + SUBLANES, SSD_CONV_DIM), F32)],
        compiler_params=_compiler_params(("parallel", "arbitrary")),
        name="ssd_proj",
    )(x, *consts)


def _ssd_scan_kernel(xbc_ref, dt_ref, z_ref, alog_ref, dskip_ref, nw_ref, tril_ref, hexp_ref,
                     y_ref, state_ref):
    L = SSD_CHUNK
    P = SSD_HEAD_DIM
    N = SSD_STATE
    hpg = SSD_HEADS // SSD_GROUPS
    gw = hpg * P

    @pl.when(pl.program_id(1) == 0)
    def _():
        state_ref[...] = jnp.zeros(state_ref.shape, F32)

    dt = dt_ref[0]
    a = dt * (-jnp.exp(alog_ref[...]))
    tril = tril_ref[...]
    ah, am, al = _split3(a)
    acs = _dot(tril, ah) + _dot(tril, am) + _dot(tril, al)
    acs_t = acs.T
    dt_t = dt.T
    e_acs = jnp.exp(acs)
    eh, em, el = _split3(e_acs)
    w_t = jnp.exp(acs_t[:, L - 1:L] - acs_t) * dt_t
    causal = (lax.broadcasted_iota(jnp.int32, (L, L), 0) >= lax.broadcasted_iota(jnp.int32, (L, L), 1))
    head_of_lane = lax.broadcasted_iota(jnp.int32, (L, gw), 1) // P

    for g in range(SSD_GROUPS):
        b_off = SSD_D_INNER + g * N
        c_off = SSD_D_INNER + SSD_GROUPS * N + g * N
        bm = xbc_ref[0, :, b_off:b_off + N]
        cm = xbc_ref[0, :, c_off:c_off + N].astype(BF16)
        cb = _dot_nt(cm, bm.astype(BF16))
        bm_t = bm.T
        xg = xbc_ref[0, :, g * gw:(g + 1) * gw]
        hexp = hexp_ref[g]
        e_exp = _dot(eh, hexp) + _dot(em, hexp) + _dot(el, hexp)
        st = state_ref[g]
        y = _dot(cm, st.astype(BF16)) * e_exp
        st = st * e_exp[L - 1:L, :]
        for j in range(hpg):
            hd = g * hpg + j
            xj = jnp.where(head_of_lane == j, xg, 0.0).astype(BF16)
            diff = acs[:, hd:hd + 1] - acs_t[hd:hd + 1, :]
            decay = jnp.exp(jnp.where(causal, diff, -jnp.inf))
            y = y + _dot((cb * decay * dt_t[hd:hd + 1, :]).astype(BF16), xj)
            st = st + _dot((bm_t * w_t[hd:hd + 1, :]).astype(BF16), xj)
        state_ref[g] = st
        cols = slice(g * gw, (g + 1) * gw)
        zg = z_ref[0, :, cols]
        y = (y + xg * dskip_ref[:, cols]) * _silu(zg)
        y_ref[0, :, cols] = _rms_rows(y, nw_ref[:, cols]).astype(BF16)


def _ssd_scan(xbc, dt, z, alog, dskip, nw, tril, hexp):
    B, S, _ = xbc.shape
    L = SSD_CHUNK
    consts = (alog, dskip, nw, tril, hexp)
    return pl.pallas_call(
        _ssd_scan_kernel,
        grid=(B, S // L),
        in_specs=[
            pl.BlockSpec((1, L, SSD_CONV_DIM), lambda b, c: (b, c, 0)),
            pl.BlockSpec((1, L, LANES), lambda b, c: (b, c, 0)),
            pl.BlockSpec((1, L, SSD_D_INNER), lambda b, c: (b, c, 0)),
        ] + [_const_spec(c.shape) for c in consts],
        out_specs=pl.BlockSpec((1, L, SSD_D_INNER), lambda b, c: (b, c, 0)),
        out_shape=jax.ShapeDtypeStruct((B, S, SSD_D_INNER), BF16),
        scratch_shapes=[pltpu.VMEM((SSD_GROUPS, SSD_STATE, SSD_D_INNER // SSD_GROUPS), F32)],
        compiler_params=_compiler_params(("parallel", "arbitrary")),
        name="ssd_scan",
    )(xbc, dt, z, *consts)


def _rel_bucket_np(dist):
    max_exact = REL_BUCKETS // 2
    d = np.maximum(dist, 0)
    df = np.maximum(d, 1).astype(np.float32)
    large = max_exact + (np.log(df / np.float32(max_exact)) / np.float32(math.log(REL_MAX_DIST / max_exact))
                         * np.float32(REL_BUCKETS - max_exact)).astype(np.int32)
    large = np.minimum(large, REL_BUCKETS - 1)
    return np.where(d < max_exact, d, large).astype(np.int32)


@functools.lru_cache(maxsize=None)
def _position_tables(S):
    n_cmp = (S - CMP_BLOCK) // CMP_STRIDE + 1
    n_sel = S // SEL_BLOCK

    def code(dist, ok):
        return np.where(ok, _rel_bucket_np(dist), -1).astype(np.int32).reshape(-1, dist.shape[-1])

    n = np.arange(N_CMP_PAD)[None, :, None]
    t = np.arange(S // TQC)[:, None, None] * TQC + np.arange(TQC)[None, None, :]
    dist = t - (n * CMP_STRIDE + CMP_BLOCK - 1)
    cmp_code = code(dist, (dist >= 0) & (n < n_cmp))
    assert TQA == TKA and (N_SEL_VARIANTS - 1) * TQA - (TKA - 1) >= REL_MAX_DIST
    ti = np.arange(TQA)[None, :]
    j = np.arange(TKA)[:, None]
    tile_dist = [TQA * behind + ti - j for behind in range(N_WIN_TILES)]
    far = np.full((TKA, TQA), 2 * REL_MAX_DIST)
    sel_dist = np.stack(tile_dist[:N_SEL_VARIANTS - 1] + [far])
    win_dist = np.stack(tile_dist)
    attn_code = np.concatenate([code(sel_dist, sel_dist >= 0),
                                code(win_dist, (win_dist >= 0) & (win_dist < WINDOW)),
                                np.full((TKA, TQA), -1, np.int32)])
    jb = np.arange(N_SEL_PAD)[:, None]
    nn = np.arange(N_CMP_PAD)[None, :]
    overlap = ((nn * CMP_STRIDE < jb * SEL_BLOCK + SEL_BLOCK) & (nn * CMP_STRIDE + CMP_BLOCK > jb * SEL_BLOCK)
               & (nn < n_cmp) & (jb < n_sel)).astype(np.float32)
    key_block = (np.arange(S)[:, None] // SEL_BLOCK == np.arange(HEAD_DIM)[None, :]).astype(np.float32)
    hpg = SSD_HEADS // SSD_GROUPS
    lane_head = np.arange(hpg * SSD_HEAD_DIM)[None, None, :] // SSD_HEAD_DIM
    head_expand = (np.arange(LANES)[None, :, None]
                   == np.arange(SSD_GROUPS)[:, None, None] * hpg + lane_head).astype(np.float32)
    return cmp_code, attn_code, overlap, key_block, head_expand


def _chunk_cols(w, width):
    K, N = w.shape
    return w.reshape(K, N // width, width).transpose(1, 0, 2)


def kernel(x, norm_mix_w, norm_mlp_w, rel_table, nsa_w_in, nsa_q_gain, nsa_k_gain, cmp_pe_k, cmp_w1_k, cmp_w2_k, cmp_pe_v, cmp_w1_v, cmp_w2_v, nsa_w_out, ssd_w_in, ssd_conv_w, ssd_conv_b, ssd_dt_bias, ssd_a_log, ssd_d, ssd_norm_w, ssd_w_out, mlp_w_up, mlp_w_down):
    B, S, D = x.shape
    T = B * S
    G, R = NSA_GROUPS, NSA_REP
    cmp_code, attn_code, overlap, key_block, head_expand = _position_tables(S)

    w_in = nsa_w_in[0].astype(BF16)
    kv0 = NSA_Q_DIM
    kv_cols = lambda c: w_in[:, kv0 + c * NSA_KV_DIM:kv0 + (c + 1) * NSA_KV_DIM]
    wq = _chunk_cols(w_in[:, :NSA_Q_DIM], 256)
    wk = jnp.stack([kv_cols(0), kv_cols(1), kv_cols(2), kv_cols(4)])
    wvt = jnp.stack([kv_cols(3).T, kv_cols(5).T])
    wg = w_in[:, kv0 + 6 * NSA_KV_DIM:].reshape(D, 3, G, R).transpose(0, 2, 1, 3).reshape(D, G, 3 * R)
    wgt = jnp.pad(wg, ((0, 0), (0, 0), (0, 16 - 3 * R))).reshape(D, G * 16).T
    qg = jnp.tile(nsa_q_gain[0], 4)[None, :]
    kg = jnp.tile(nsa_k_gain[0], 4)[None, :]
    seg = jnp.asarray(np.kron(np.eye(4, dtype=np.float32), np.ones((HEAD_DIM, HEAD_DIM), np.float32)), BF16)
    q, kc, vc, k_sw, vt_sw, gates = _nsa_proj(
        x, jnp.asarray(key_block, BF16), norm_mix_w[0][None, :], wq, wk, wvt, wgt, qg, kg, seg, tm=512)

    kcmp, vcmpt = _nsa_compress(
        kc, vc, cmp_pe_k[0], cmp_pe_v[0], cmp_w1_k[0].astype(BF16), cmp_w1_v[0].astype(BF16),
        cmp_w2_k[0].astype(BF16), cmp_w2_v[0].astype(BF16).T, nsa_k_gain[0][None, :])

    bcmp = _bias_table(rel_table, jnp.asarray(cmp_code), N_CMP_PAD).reshape(NSA_HEADS, S // TQC, N_CMP_PAD, TQC)
    battn = _bias_table(rel_table, jnp.asarray(attn_code), TKA).reshape(NSA_HEADS, -1, TKA, TQA)
    o_cmp, q_aug = _nsa_cmp_select(q, gates, kcmp, vcmpt, bcmp, jnp.asarray(overlap, BF16))
    o = _nsa_attn(q_aug, gates, k_sw, vt_sw, o_cmp, battn)

    def mlp_weights(li):
        return (_chunk_cols(mlp_w_up[li].astype(BF16), MLP_CHUNK),
                mlp_w_down[li].astype(BF16).reshape(MLP_HIDDEN // MLP_CHUNK, MLP_CHUNK, D))

    wup, wdn = mlp_weights(0)
    x2 = _mix_out_mlp(x.reshape(T, D), o.reshape(T, NSA_Q_DIM), nsa_w_out[0].astype(BF16),
                      norm_mlp_w[0][None, :], wup, wdn, tm=512)

    w_in = ssd_w_in[0]
    wz = _chunk_cols(w_in[:, :SSD_D_INNER].astype(BF16), SSD_NCHUNK)
    wx = _chunk_cols(w_in[:, SSD_D_INNER:SSD_D_INNER + SSD_CONV_DIM].astype(BF16), SSD_NCHUNK)
    lane_pad = LANES - SSD_HEADS
    wdt = jnp.pad(w_in[:, SSD_D_INNER + SSD_CONV_DIM:], ((0, 0), (0, lane_pad))).astype(BF16)
    pad_heads = lambda v: jnp.pad(v, (0, lane_pad))[None, :]
    z, xbc, dt = _ssd_proj(x2.reshape(B, S, D), norm_mix_w[1][None, :], wz, wx, wdt,
                           ssd_conv_w[0], ssd_conv_b[0][None, :], pad_heads(ssd_dt_bias[0]), tm=512)
    tril = jnp.asarray(np.tril(np.ones((SSD_CHUNK, SSD_CHUNK), np.float32)), BF16)
    dskip = jnp.repeat(ssd_d[0], SSD_HEAD_DIM)[None, :]
    y = _ssd_scan(xbc, dt, z, pad_heads(ssd_a_log[0]), dskip, ssd_norm_w[0][None, :], tril,
                  jnp.asarray(head_expand, BF16))

    wup, wdn = mlp_weights(1)
    x4 = _mix_out_mlp(x2, y.reshape(T, SSD_D_INNER), ssd_w_out[0].astype(BF16),
                      norm_mlp_w[1][None, :], wup, wdn, tm=512)
    return x4.reshape(B, S, D)
```

```python
import functools
import math

import numpy as np
import jax
import jax.numpy as jnp
from jax import lax
from jax.experimental import pallas as pl
from jax.experimental.pallas import tpu as pltpu

F32 = jnp.float32
BF16 = jnp.bfloat16

D_MODEL = 1024
EPS = 1e-6

NSA_HEADS = 16
NSA_GROUPS = 4
NSA_REP = NSA_HEADS // NSA_GROUPS
HEAD_DIM = 64
NSA_Q_DIM = NSA_HEADS * HEAD_DIM
NSA_KV_DIM = NSA_GROUPS * HEAD_DIM
CMP_BLOCK = 32
CMP_STRIDE = 16
CMP_HIDDEN = 256
SEL_BLOCK = 64
SEL_TOP_N = 16
WINDOW = 512
REL_BUCKETS = 32
REL_MAX_DIST = 128

SSD_D_INNER = 2 * D_MODEL
SSD_HEAD_DIM = 64
SSD_HEADS = SSD_D_INNER // SSD_HEAD_DIM
SSD_GROUPS = 8
SSD_STATE = 128
SSD_CONV = 4
SSD_CHUNK = 128
SSD_CONV_DIM = SSD_D_INNER + 2 * SSD_GROUPS * SSD_STATE
MLP_HIDDEN = 4 * D_MODEL

LANES = 128
SUBLANES = 8
VMEM_LIMIT_BYTES = 56 * 1024 * 1024

VT_BLOCK = 128
TQC = 512
TQA = 256
TKA = 256
N_CMP_PAD = 128
N_SEL_PAD = 128
MASKED = -1e30
LOG2E = math.log2(math.e)
SCORE_BIG = 3e38
N_SEL_VARIANTS = 3
N_WIN_TILES = WINDOW // TKA + 1
ACC_ROWS = HEAD_DIM + 16

NT_DIMS = (((1,), (1,)), ((), ()))


def _dot(a, b):
    return jnp.dot(a, b, preferred_element_type=F32)


def _dot_nt(a, b):
    return lax.dot_general(a, b, NT_DIMS, preferred_element_type=F32)


def _split3(x):
    hi = x.astype(BF16)
    r1 = x - hi.astype(F32)
    mid = r1.astype(BF16)
    lo = (r1 - mid.astype(F32)).astype(BF16)
    return hi, mid, lo


def _rms_rows(x, w):
    return x * lax.rsqrt(jnp.mean(x * x, axis=-1, keepdims=True) + EPS) * w


def _silu(x):
    h = 0.5 * x
    return h + h * jnp.tanh(h)


def _const_spec(shape):
    nd = len(shape)
    return pl.BlockSpec(shape, lambda *_: (0,) * nd, pipeline_mode=pl.Buffered(1))


def _compiler_params(semantics):
    return pltpu.CompilerParams(dimension_semantics=semantics, vmem_limit_bytes=VMEM_LIMIT_BYTES)


def _nsa_proj_kernel(x_ref, blk_ref, nw_ref, wq_ref, wk_ref, wvt_ref, wgt_ref, qg_ref, kg_ref, seg_ref,
                     q_ref, kc_ref, vc_ref, k_ref, vt_ref, gate_ref):
    tm = x_ref.shape[1]
    h = _rms_rows(x_ref[0], nw_ref[...]).astype(BF16)
    seg = seg_ref[...]

    def head_rms(y, gain):
        sq = y * y
        hi = sq.astype(BF16)
        lo = (sq - hi.astype(F32)).astype(BF16)
        ss = _dot(hi, seg) + _dot(lo, seg)
        return y * lax.rsqrt(ss * (1.0 / HEAD_DIM) + EPS) * gain

    def emit_q(c, y):
        y = (head_rms(y, qg_ref[...]) * (HEAD_DIM ** -0.5 * LOG2E)).astype(BF16)
        for r in range(4):
            q_ref[0, 4 * c + r] = y[:, 64 * r:64 * r + 64]

    def emit_raw(c, y):
        raw_ref = (kc_ref, vc_ref)[c]
        for g in range(NSA_GROUPS):
            raw_ref[0, g] = y[:, 64 * g:64 * g + 64]

    def emit_k(br, y):
        y = head_rms(y, kg_ref[...]).astype(BF16)
        tag = blk_ref[...] if br == 0 else jnp.zeros(blk_ref.shape, BF16)
        for g in range(NSA_GROUPS):
            k_ref[0, g, br, :, 0:HEAD_DIM] = y[:, 64 * g:64 * g + 64]
            k_ref[0, g, br, :, HEAD_DIM:2 * HEAD_DIM] = tag

    def emit_vt(br, yt):
        yt = yt.astype(BF16)
        for g in range(NSA_GROUPS):
            for kb in range(tm // VT_BLOCK):
                vt_ref[0, g, br, kb] = yt[64 * g:64 * g + 64, VT_BLOCK * kb:VT_BLOCK * (kb + 1)]

    def emit_gates(_, yt):
        gate_ref[0] = jax.nn.sigmoid(yt)

    stages = ([(lambda c=c: _dot(h, wq_ref[:, 256 * c:256 * (c + 1)]), emit_q, c) for c in range(NSA_Q_DIM // 256)]
              + [(lambda c=c: _dot(h, wk_ref[c]), emit_raw, c) for c in range(2)]
              + [(lambda br=br: _dot(h, wk_ref[2 + br]), emit_k, br) for br in range(2)]
              + [(lambda br=br: _dot_nt(wvt_ref[br], h), emit_vt, br) for br in range(2)]
              + [(lambda: _dot_nt(wgt_ref[...], h), emit_gates, 0)])
    pending = stages[0][0]()
    for n, (_, emit, arg) in enumerate(stages):
        y = pending
        if n + 1 < len(stages):
            pending = stages[n + 1][0]()
        emit(arg, y)


def _nsa_proj(x, key_block, nw, wq, wk, wvt, wgt, qg, kg, seg, tm):
    B, S, D = x.shape
    G = NSA_GROUPS
    raw_block = pl.BlockSpec((1, G, tm, HEAD_DIM), lambda b, i: (b, 0, i, 0))
    raw_shape = jax.ShapeDtypeStruct((B, G, S, HEAD_DIM), F32)
    consts = (nw, wq, wk, wvt, wgt, qg, kg, seg)
    return pl.pallas_call(
        _nsa_proj_kernel,
        grid=(B, S // tm),
        in_specs=[pl.BlockSpec((1, tm, D), lambda b, i: (b, i, 0)),
                  pl.BlockSpec((tm, HEAD_DIM), lambda b, i: (i, 0))] + [_const_spec(c.shape) for c in consts],
        out_specs=[
            pl.BlockSpec((1, NSA_HEADS, tm, HEAD_DIM), lambda b, i: (b, 0, i, 0)),
            raw_block, raw_block,
            pl.BlockSpec((1, G, 2, tm, 2 * HEAD_DIM), lambda b, i: (b, 0, 0, i, 0)),
            pl.BlockSpec((1, G, 2, tm // VT_BLOCK, HEAD_DIM, VT_BLOCK), lambda b, i: (b, 0, 0, i, 0, 0)),
            pl.BlockSpec((1, G * 16, tm), lambda b, i: (b, 0, i)),
        ],
        out_shape=[
            jax.ShapeDtypeStruct((B, NSA_HEADS, S, HEAD_DIM), BF16),
            raw_shape, raw_shape,
            jax.ShapeDtypeStruct((B, G, 2, S, 2 * HEAD_DIM), BF16),
            jax.ShapeDtypeStruct((B, G, 2, S // VT_BLOCK, HEAD_DIM, VT_BLOCK), BF16),
            jax.ShapeDtypeStruct((B, G * 16, S), F32),
        ],
        compiler_params=_compiler_params(("parallel", "parallel")),
        name="nsa_proj",
    )(x, key_block, *consts)


def _nsa_compress_kernel(kc_ref, vc_ref, pek_ref, pev_ref, w1k_ref, w1v_ref, w2k_ref, w2vt_ref, kg_ref,
                         kcmp_ref, vcmp_ref):
    half = CMP_BLOCK // 2

    def hidden(src_ref, pe_ref, w1_ref):
        acc_a = jnp.zeros((N_CMP_PAD, CMP_HIDDEN), F32)
        acc_b = jnp.zeros((N_CMP_PAD, CMP_HIDDEN), F32)
        for l in range(half):
            rows = src_ref[0, 0, pl.ds(l, N_CMP_PAD, stride=CMP_STRIDE), :]
            acc_a += _dot((rows + pe_ref[l:l + 1, :]).astype(BF16), w1_ref[l])
            acc_b += _dot((rows + pe_ref[half + l:half + l + 1, :]).astype(BF16), w1_ref[half + l])
        hid = acc_a + pltpu.roll(acc_b, N_CMP_PAD - 1, axis=0)
        return _silu(hid).astype(BF16)

    k = _dot(hidden(kc_ref, pek_ref, w1k_ref), w2k_ref[...])
    kcmp_ref[0, 0] = _rms_rows(k, kg_ref[...]).astype(BF16)
    vcmp_ref[0, 0] = _dot_nt(w2vt_ref[...], hidden(vc_ref, pev_ref, w1v_ref)).astype(BF16)


def _nsa_compress(kc, vc, pek, pev, w1k, w1v, w2k, w2vt, kg):
    B, G, S, Dh = kc.shape
    src = pl.BlockSpec((1, 1, S, Dh), lambda b, g: (b, g, 0, 0))
    consts = (pek, pev, w1k, w1v, w2k, w2vt, kg)
    return pl.pallas_call(
        _nsa_compress_kernel,
        grid=(B, G),
        in_specs=[src, src] + [_const_spec(c.shape) for c in consts],
        out_specs=[pl.BlockSpec((1, 1, N_CMP_PAD, Dh), lambda b, g: (b, g, 0, 0)),
                   pl.BlockSpec((1, 1, Dh, N_CMP_PAD), lambda b, g: (b, g, 0, 0))],
        out_shape=[jax.ShapeDtypeStruct((B, G, N_CMP_PAD, Dh), BF16),
                   jax.ShapeDtypeStruct((B, G, Dh, N_CMP_PAD), BF16)],
        compiler_params=_compiler_params(("parallel", "parallel")),
        name="nsa_compress",
    )(kc, vc, *consts)


BIAS_CHUNK = 8192
BIAS_CODES = 64


def _bias_table_kernel(t3_ref, code_ref, out_ref):
    code = code_ref[...]
    ids = lax.broadcasted_iota(jnp.int32, (BIAS_CODES, code.shape[1]), 0)
    onehot = jnp.where(ids == code, 1.0, 0.0).astype(BF16)
    y = _dot(t3_ref[...], onehot)
    out_ref[...] = y[0:NSA_HEADS] + y[NSA_HEADS:2 * NSA_HEADS] + y[2 * NSA_HEADS:3 * NSA_HEADS]


def _bias_table(table3, code):
    n = code.shape[1]
    return pl.pallas_call(
        _bias_table_kernel,
        grid=(n // BIAS_CHUNK,),
        in_specs=[_const_spec(table3.shape), pl.BlockSpec((1, BIAS_CHUNK), lambda i: (0, i))],
        out_specs=pl.BlockSpec((NSA_HEADS, BIAS_CHUNK), lambda i: (0, i)),
        out_shape=jax.ShapeDtypeStruct((NSA_HEADS, n), F32),
        compiler_params=_compiler_params(("parallel",)),
        name="rel_bias_table",
    )(table3, code)


def _nsa_cmp_select_kernel(q_ref, gate_ref, kcmp_ref, vcmpt_ref, bias_ref, ovl_ref, o_ref, qa_ref):
    i = pl.program_id(2)
    kcmp = kcmp_ref[0, 0]
    vcmpt = vcmpt_ref[0, 0]

    scores = [_dot_nt(kcmp, q_ref[0, r]) for r in range(NSA_REP)]
    probs = []
    for r in range(NSA_REP):
        s = scores[r] + bias_ref[r, 0]
        m = jnp.max(s, axis=0, keepdims=True)
        m = jnp.where(m > 0.5 * MASKED, m, 0.0)
        e = jnp.exp2(s - m)
        den = jnp.sum(e, axis=0, keepdims=True)
        probs.append(e / jnp.where(den > 0.0, den, 1.0))
    outs = [_dot(vcmpt, p.astype(BF16)) for p in probs]

    ph, pm, plo = _split3(probs[0] + probs[1] + probs[2] + probs[3])
    ovl = ovl_ref[...]
    n_sel = 32
    imp = (_dot(ovl, ph) + _dot(ovl, pm) + _dot(ovl, plo))[0:n_sel]
    tok = i * TQC + lax.broadcasted_iota(jnp.int32, (n_sel, TQC), 1)
    blk = lax.broadcasted_iota(jnp.int32, (n_sel, TQC), 0)
    cur = tok // SEL_BLOCK
    forced = (blk == 0) | (blk == cur) | (blk == cur - 1)
    valid = blk * SEL_BLOCK <= tok
    score = jnp.where(forced, SCORE_BIG, imp)
    score = jnp.where(valid, score, -SCORE_BIG)
    groups = [slice(SUBLANES * v, SUBLANES * (v + 1)) for v in range(n_sel // SUBLANES)]
    ranks = [jnp.zeros((SUBLANES, TQC), F32) for _ in groups]
    for jp in range(n_sel):
        other = score[jp:jp + 1, :]
        for v, rows in enumerate(groups):
            mine = score[rows]
            if rows.start > jp:
                beats = other >= mine
            elif rows.stop - 1 <= jp:
                beats = other > mine
            else:
                later = lax.broadcasted_iota(jnp.int32, (SUBLANES, TQC), 0) > jp - rows.start
                beats = (other > mine) | ((other == mine) & later)
            ranks[v] = ranks[v] + jnp.where(beats, 1.0, 0.0)
    rank = jnp.concatenate(ranks, axis=0)
    blk_mask = jnp.where(rank < SEL_TOP_N, 0.0, MASKED)
    blk_mask = jnp.concatenate([blk_mask, jnp.zeros((HEAD_DIM - n_sel, TQC), F32)], axis=0).T.astype(BF16)
    for r in range(NSA_REP):
        qa_ref[0, r, :, 0:HEAD_DIM] = q_ref[0, r]
        qa_ref[0, r, :, HEAD_DIM:2 * HEAD_DIM] = blk_mask

    gate = gate_ref[0]
    for r in range(NSA_REP):
        o_ref[0, :, r * HEAD_DIM:(r + 1) * HEAD_DIM] = (outs[r] * gate[r:r + 1]).T


def _nsa_cmp_select(q, gates, kcmp, vcmpt, bcmp, ovl):
    B, H, S, Dh = q.shape
    G = NSA_GROUPS
    return pl.pallas_call(
        _nsa_cmp_select_kernel,
        grid=(G, B, S // TQC),
        in_specs=[
            pl.BlockSpec((1, NSA_REP, TQC, Dh), lambda g, b, i: (b, g, i, 0)),
            pl.BlockSpec((1, 16, TQC), lambda g, b, i: (b, g, i)),
            pl.BlockSpec((1, 1, N_CMP_PAD, Dh), lambda g, b, i: (b, g, 0, 0)),
            pl.BlockSpec((1, 1, Dh, N_CMP_PAD), lambda g, b, i: (b, g, 0, 0)),
            pl.BlockSpec((NSA_REP, 1, N_CMP_PAD, TQC), lambda g, b, i: (g, i, 0, 0)),
            _const_spec(ovl.shape),
        ],
        out_specs=[
            pl.BlockSpec((1, TQC, NSA_REP * Dh), lambda g, b, i: (b, i, g)),
            pl.BlockSpec((1, NSA_REP, TQC, 2 * Dh), lambda g, b, i: (b, g, i, 0)),
        ],
        out_shape=[
            jax.ShapeDtypeStruct((B, S, H * Dh), F32),
            jax.ShapeDtypeStruct((B, H, S, 2 * Dh), BF16),
        ],
        compiler_params=_compiler_params(("parallel", "parallel", "arbitrary")),
        name="nsa_cmp_select",
    )(q, gates, kcmp, vcmpt, bcmp, ovl)


def _nsa_attn_kernel(q_ref, gate_ref, k_ref, vt_ref, ocmp_ref, bias_ref,
                     o_ref, s0_ref, s1_ref, x0_ref, x1_ref, p0_ref, p1_ref, a0_ref, a1_ref,
                     m_ref, acc_ref):
    i = pl.program_id(2)
    last_step = pl.num_programs(2) - 1
    s_bufs, x_bufs = (s0_ref, s1_ref), (x0_ref, x1_ref)
    p_bufs, a_bufs = (p0_ref, p1_ref), (a0_ref, a1_ref)
    masked_variant = N_SEL_VARIANTS + N_WIN_TILES

    def n_tasks_of(step):
        n_sel = step + 1
        return n_sel, n_sel + jnp.minimum(N_WIN_TILES, n_sel)

    def task(t, step):
        n_sel, n_tasks = n_tasks_of(step)
        is_win = t >= n_sel
        c = jnp.clip(jnp.where(is_win, step - (t - n_sel), t), 0, step)
        behind = step - c
        variant = jnp.where(is_win, N_SEL_VARIANTS + behind, jnp.minimum(behind, N_SEL_VARIANTS - 1))
        variant = jnp.where(t < n_tasks, variant, masked_variant)
        return is_win.astype(jnp.int32), c, variant

    def score_pair(u, step):
        pair = [task(2 * u, step), task(2 * u + 1, step)]
        q0 = pl.multiple_of(step * TQA, TQA)
        k = jnp.concatenate([k_ref[0, 0, br, pl.ds(pl.multiple_of(c * TKA, TKA), TKA), :] for br, c, _ in pair],
                            axis=0)
        for r in range(NSA_REP):
            s2 = _dot_nt(k, q_ref[0, r, pl.ds(q0, TQA), :])
            for slot, (_, _, variant) in enumerate(pair):
                s = s2[slot * TKA:(slot + 1) * TKA] + bias_ref[r, variant]
                s_bufs[slot][r] = s
                x_bufs[slot][r] = jnp.max(s, axis=0, keepdims=True)

    def softmax_stage(t, slot):
        br, _, _ = task(t, i)
        for r in range(NSA_REP):
            m_old = m_ref[br, r]
            m_new = jnp.maximum(m_old, x_bufs[slot][r])
            m_ref[br, r] = m_new
            a_bufs[slot][r] = jnp.exp2(m_old - m_new)
            p_bufs[slot][r] = jnp.exp2(s_bufs[slot][r] - m_new).astype(BF16)

    ones_rows = jnp.where(lax.broadcasted_iota(jnp.int32, (ACC_ROWS - HEAD_DIM, TKA), 0) == 0, 1.0, 0.0).astype(BF16)

    def value_stage(t, slot):
        br, c, _ = task(t, i)
        blocks = TKA // VT_BLOCK
        vt = jnp.concatenate([vt_ref[0, 0, br, blocks * c + u] for u in range(blocks)], axis=1)
        vt = jnp.concatenate([vt, ones_rows], axis=0)
        for r in range(NSA_REP):
            acc_ref[br, r] = a_bufs[slot][r] * acc_ref[br, r] + _dot(vt, p_bufs[slot][r])

    m_ref[...] = jnp.full(m_ref.shape, MASKED, F32)
    acc_ref[...] = jnp.zeros(acc_ref.shape, F32)
    for p_ref, a_ref in zip(p_bufs, a_bufs):
        p_ref[...] = jnp.zeros(p_ref.shape, BF16)
        a_ref[...] = jnp.ones(a_ref.shape, F32)

    @pl.when(i == 0)
    def _():
        score_pair(0, i)

    n_trips = (n_tasks_of(i)[1] + 1) // 2

    def trip(u, carry):
        value_stage(jnp.maximum(2 * u - 2, 0), 0)
        value_stage(jnp.maximum(2 * u - 1, 0), 1)
        softmax_stage(2 * u, 0)
        softmax_stage(2 * u + 1, 1)
        in_tile = u + 1 < n_trips
        score_pair(jnp.where(in_tile, u + 1, 0), jnp.where(in_tile, i, jnp.minimum(i + 1, last_step)))
        return carry

    lax.fori_loop(0, n_trips, trip, 0)
    value_stage(2 * n_trips - 2, 0)
    value_stage(2 * n_trips - 1, 1)

    gate = gate_ref[0]
    for r in range(NSA_REP):
        w_sel = gate[4 + r:5 + r] / acc_ref[0, r, HEAD_DIM:HEAD_DIM + 1, :]
        w_win = gate[8 + r:9 + r] / acc_ref[1, r, HEAD_DIM:HEAD_DIM + 1, :]
        o_t = w_sel * acc_ref[0, r, 0:HEAD_DIM, :] + w_win * acc_ref[1, r, 0:HEAD_DIM, :]
        cols = slice(r * HEAD_DIM, (r + 1) * HEAD_DIM)
        o_ref[0, :, cols] = (ocmp_ref[0, :, cols] + o_t.T).astype(BF16)


def _nsa_attn(q_aug, gates, k_sw, vt_sw, o_cmp, bias):
    B, H, S, Dq = q_aug.shape
    Dh = HEAD_DIM
    G = NSA_GROUPS
    n_vt = S // VT_BLOCK
    n_var = bias.shape[1]
    return pl.pallas_call(
        _nsa_attn_kernel,
        grid=(G, B, S // TQA),
        in_specs=[
            pl.BlockSpec((1, NSA_REP, S, Dq), lambda g, b, i: (b, g, 0, 0)),
            pl.BlockSpec((1, 16, TQA), lambda g, b, i: (b, g, i)),
            pl.BlockSpec((1, 1, 2, S, Dq), lambda g, b, i: (b, g, 0, 0, 0)),
            pl.BlockSpec((1, 1, 2, n_vt, Dh, VT_BLOCK), lambda g, b, i: (b, g, 0, 0, 0, 0)),
            pl.BlockSpec((1, TQA, NSA_REP * Dh), lambda g, b, i: (b, i, g)),
            pl.BlockSpec((NSA_REP, n_var, TKA, TQA), lambda g, b, i: (g, 0, 0, 0), pipeline_mode=pl.Buffered(1)),
        ],
        out_specs=pl.BlockSpec((1, TQA, NSA_REP * Dh), lambda g, b, i: (b, i, g)),
        out_shape=jax.ShapeDtypeStruct((B, S, H * Dh), BF16),
        scratch_shapes=[
            pltpu.VMEM((NSA_REP, TKA, TQA), F32),
            pltpu.VMEM((NSA_REP, TKA, TQA), F32),
            pltpu.VMEM((NSA_REP, 1, TQA), F32),
            pltpu.VMEM((NSA_REP, 1, TQA), F32),
            pltpu.VMEM((NSA_REP, TKA, TQA), BF16),
            pltpu.VMEM((NSA_REP, TKA, TQA), BF16),
            pltpu.VMEM((NSA_REP, 1, TQA), F32),
            pltpu.VMEM((NSA_REP, 1, TQA), F32),
            pltpu.VMEM((2, NSA_REP, 1, TQA), F32),
            pltpu.VMEM((2, NSA_REP, ACC_ROWS, TQA), F32),
        ],
        compiler_params=_compiler_params(("parallel", "parallel", "arbitrary")),
        name="nsa_attn",
    )(q_aug, gates, k_sw, vt_sw, o_cmp, bias)


MLP_CHUNK = 1024


def _mix_out_mlp_kernel(x_ref, a_ref, wo_ref, nw_ref, wup_ref, wdn_ref, o_ref):
    x1 = x_ref[...] + _dot(a_ref[...], wo_ref[...])
    h = _rms_rows(x1, nw_ref[...]).astype(BF16)

    acc = None
    for c in range(wup_ref.shape[1] // MLP_CHUNK):
        hid = slice(c * MLP_CHUNK, (c + 1) * MLP_CHUNK)
        u = jnp.maximum(_dot(h, wup_ref[:, hid]), 0.0)
        d = _dot((u * u).astype(BF16), wdn_ref[hid, :])
        acc = d if acc is None else acc + d
    o_ref[...] = x1 + acc


def _mix_out_mlp(x, a, wo, nw, wup, wdn, tm):
    T, D = x.shape
    K = a.shape[1]
    return pl.pallas_call(
        _mix_out_mlp_kernel,
        grid=(T // tm,),
        in_specs=[
            pl.BlockSpec((tm, D), lambda i: (i, 0)),
            pl.BlockSpec((tm, K), lambda i: (i, 0)),
            _const_spec(wo.shape), _const_spec(nw.shape), _const_spec(wup.shape), _const_spec(wdn.shape),
        ],
        out_specs=pl.BlockSpec((tm, D), lambda i: (i, 0)),
        out_shape=jax.ShapeDtypeStruct((T, D), F32),
        compiler_params=_compiler_params(("parallel",)),
        name="mix_out_mlp",
    )(x, a, wo, nw, wup, wdn)


SSD_NCHUNK = 512


def _ssd_proj_kernel(x_ref, nw_ref, wz_ref, wx_ref, wdt_ref, cw_ref, cb_ref, dtb_ref,
                     z_ref, xbc_ref, dt_ref, buf_ref):
    i = pl.program_id(1)
    tm = x_ref.shape[1]
    h = _rms_rows(x_ref[0], nw_ref[...]).astype(BF16)

    for c in range(wz_ref.shape[0]):
        z_ref[0, :, c * SSD_NCHUNK:(c + 1) * SSD_NCHUNK] = _dot(h, wz_ref[c])

    @pl.when(i == 0)
    def _():
        buf_ref[0:SUBLANES, :] = jnp.zeros((SUBLANES, buf_ref.shape[1]), F32)

    @pl.when(i > 0)
    def _():
        buf_ref[0:SUBLANES, :] = buf_ref[tm:tm + SUBLANES, :]

    for c in range(wx_ref.shape[0]):
        cols = slice(c * SSD_NCHUNK, (c + 1) * SSD_NCHUNK)
        buf_ref[SUBLANES:SUBLANES + tm, cols] = _dot(h, wx_ref[c])
        acc = cb_ref[:, cols] + buf_ref[SUBLANES:SUBLANES + tm, cols] * cw_ref[SSD_CONV - 1:SSD_CONV, cols]
        for k in range(SSD_CONV - 1):
            off = SUBLANES - (SSD_CONV - 1) + k
            acc += buf_ref[off:off + tm, cols] * cw_ref[k:k + 1, cols]
        xbc_ref[0, :, cols] = _silu(acc)

    dt_ref[0] = jax.nn.softplus(_dot(h, wdt_ref[...]) + dtb_ref[...])


def _ssd_proj(x, nw, wz, wx, wdt, cw, cb, dtb, tm):
    B, S, D = x.shape
    consts = (nw, wz, wx, wdt, cw, cb, dtb)
    return pl.pallas_call(
        _ssd_proj_kernel,
        grid=(B, S // tm),
        in_specs=[pl.BlockSpec((1, tm, D), lambda b, i: (b, i, 0))] + [_const_spec(c.shape) for c in consts],
        out_specs=[
            pl.BlockSpec((1, tm, SSD_D_INNER), lambda b, i: (b, i, 0)),
            pl.BlockSpec((1, tm, SSD_CONV_DIM), lambda b, i: (b, i, 0)),
            pl.BlockSpec((1, tm, LANES), lambda b, i: (b, i, 0)),
        ],
        out_shape=[
            jax.ShapeDtypeStruct((B, S, SSD_D_INNER), F32),
            jax.ShapeDtypeStruct((B, S, SSD_CONV_DIM), F32),
            jax.ShapeDtypeStruct((B, S, LANES), F32),
        ],
        scratch_shapes=[pltpu.VMEM((tm + SUBLANES, SSD_CONV_DIM), F32)],
        compiler_params=_compiler_params(("parallel", "arbitrary")),
        name="ssd_proj",
    )(x, *consts)


def _ssd_scan_kernel(xbc_ref, dt_ref, z_ref, alog_ref, dskip_ref, nw_ref, tril_ref, hexp_ref,
                     y_ref, state_ref):
    L = SSD_CHUNK
    P = SSD_HEAD_DIM
    N = SSD_STATE
    hpg = SSD_HEADS // SSD_GROUPS
    gw = hpg * P

    @pl.when(pl.program_id(1) == 0)
    def _():
        state_ref[...] = jnp.zeros(state_ref.shape, F32)

    dt = dt_ref[0]
    a = dt * (-jnp.exp(alog_ref[...]))
    tril = tril_ref[...]
    ah, am, al = _split3(a)
    acs = _dot(tril, ah) + _dot(tril, am) + _dot(tril, al)
    acs_t = acs.T
    dt_t = dt.T
    e_acs = jnp.exp(acs)
    eh, em, el = _split3(e_acs)
    w_t = jnp.exp(acs_t[:, L - 1:L] - acs_t) * dt_t
    causal = (lax.broadcasted_iota(jnp.int32, (L, L), 0) >= lax.broadcasted_iota(jnp.int32, (L, L), 1))
    head_of_lane = lax.broadcasted_iota(jnp.int32, (L, gw), 1) // P

    for g in range(SSD_GROUPS):
        b_off = SSD_D_INNER + g * N
        c_off = SSD_D_INNER + SSD_GROUPS * N + g * N
        bm = xbc_ref[0, :, b_off:b_off + N]
        cm = xbc_ref[0, :, c_off:c_off + N].astype(BF16)
        cb = _dot_nt(cm, bm.astype(BF16))
        bm_t = bm.T
        xg = xbc_ref[0, :, g * gw:(g + 1) * gw]
        hexp = hexp_ref[g]
        e_exp = _dot(eh, hexp) + _dot(em, hexp) + _dot(el, hexp)
        st = state_ref[g]
        y = _dot(cm, st.astype(BF16)) * e_exp
        st = st * e_exp[L - 1:L, :]
        for j in range(hpg):
            hd = g * hpg + j
            xj = jnp.where(head_of_lane == j, xg, 0.0).astype(BF16)
            diff = acs[:, hd:hd + 1] - acs_t[hd:hd + 1, :]
            decay = jnp.exp(jnp.where(causal, diff, -jnp.inf))
            y = y + _dot((cb * decay * dt_t[hd:hd + 1, :]).astype(BF16), xj)
            st = st + _dot((bm_t * w_t[hd:hd + 1, :]).astype(BF16), xj)
        state_ref[g] = st
        cols = slice(g * gw, (g + 1) * gw)
        zg = z_ref[0, :, cols]
        y = (y + xg * dskip_ref[:, cols]) * _silu(zg)
        y_ref[0, :, cols] = _rms_rows(y, nw_ref[:, cols]).astype(BF16)


def _ssd_scan(xbc, dt, z, alog, dskip, nw, tril, hexp):
    B, S, _ = xbc.shape
    L = SSD_CHUNK
    consts = (alog, dskip, nw, tril, hexp)
    return pl.pallas_call(
        _ssd_scan_kernel,
        grid=(B, S // L),
        in_specs=[
            pl.BlockSpec((1, L, SSD_CONV_DIM), lambda b, c: (b, c, 0)),
            pl.BlockSpec((1, L, LANES), lambda b, c: (b, c, 0)),
            pl.BlockSpec((1, L, SSD_D_INNER), lambda b, c: (b, c, 0)),
        ] + [_const_spec(c.shape) for c in consts],
        out_specs=pl.BlockSpec((1, L, SSD_D_INNER), lambda b, c: (b, c, 0)),
        out_shape=jax.ShapeDtypeStruct((B, S, SSD_D_INNER), BF16),
        scratch_shapes=[pltpu.VMEM((SSD_GROUPS, SSD_STATE, SSD_D_INNER // SSD_GROUPS), F32)],
        compiler_params=_compiler_params(("parallel", "arbitrary")),
        name="ssd_scan",
    )(xbc, dt, z, *consts)


def _rel_bucket_np(dist):
    max_exact = REL_BUCKETS // 2
    d = np.maximum(dist, 0)
    df = np.maximum(d, 1).astype(np.float32)
    large = max_exact + (np.log(df / np.float32(max_exact)) / np.float32(math.log(REL_MAX_DIST / max_exact))
                         * np.float32(REL_BUCKETS - max_exact)).astype(np.int32)
    large = np.minimum(large, REL_BUCKETS - 1)
    return np.where(d < max_exact, d, large).astype(np.int32)


@functools.lru_cache(maxsize=None)
def _position_tables(S):
    n_cmp = (S - CMP_BLOCK) // CMP_STRIDE + 1
    n_sel = S // SEL_BLOCK

    masked_code = REL_BUCKETS

    def code(dist, ok):
        return np.where(ok, _rel_bucket_np(dist), masked_code).astype(np.int32).reshape(1, -1)

    n = np.arange(N_CMP_PAD)[None, :, None]
    t = np.arange(S // TQC)[:, None, None] * TQC + np.arange(TQC)[None, None, :]
    dist = t - (n * CMP_STRIDE + CMP_BLOCK - 1)
    cmp_code = code(dist, (dist >= 0) & (n < n_cmp))
    assert TQA == TKA and (N_SEL_VARIANTS - 1) * TQA - (TKA - 1) >= REL_MAX_DIST
    ti = np.arange(TQA)[None, :]
    j = np.arange(TKA)[:, None]
    tile_dist = [TQA * behind + ti - j for behind in range(N_WIN_TILES)]
    far = np.full((TKA, TQA), 2 * REL_MAX_DIST)
    sel_dist = np.stack(tile_dist[:N_SEL_VARIANTS - 1] + [far])
    win_dist = np.stack(tile_dist)
    attn_code = np.concatenate([code(sel_dist, sel_dist >= 0),
                                code(win_dist, (win_dist >= 0) & (win_dist < WINDOW)),
                                np.full((1, TKA * TQA), masked_code, np.int32)],
                               axis=1)
    jb = np.arange(N_SEL_PAD)[:, None]
    nn = np.arange(N_CMP_PAD)[None, :]
    overlap = ((nn * CMP_STRIDE < jb * SEL_BLOCK + SEL_BLOCK) & (nn * CMP_STRIDE + CMP_BLOCK > jb * SEL_BLOCK)
               & (nn < n_cmp) & (jb < n_sel)).astype(np.float32)
    key_block = (np.arange(S)[:, None] // SEL_BLOCK == np.arange(HEAD_DIM)[None, :]).astype(np.float32)
    hpg = SSD_HEADS // SSD_GROUPS
    lane_head = np.arange(hpg * SSD_HEAD_DIM)[None, None, :] // SSD_HEAD_DIM
    head_expand = (np.arange(LANES)[None, :, None]
                   == np.arange(SSD_GROUPS)[:, None, None] * hpg + lane_head).astype(np.float32)
    return cmp_code, attn_code, overlap, key_block, head_expand


def kernel(x, norm_mix_w, norm_mlp_w, rel_table, nsa_w_in, nsa_q_gain, nsa_k_gain, cmp_pe_k, cmp_w1_k, cmp_w2_k, cmp_pe_v, cmp_w1_v, cmp_w2_v, nsa_w_out, ssd_w_in, ssd_conv_w, ssd_conv_b, ssd_dt_bias, ssd_a_log, ssd_d, ssd_norm_w, ssd_w_out, mlp_w_up, mlp_w_down):
    B, S, D = x.shape
    T = B * S
    G, R = NSA_GROUPS, NSA_REP
    cmp_code, attn_code, overlap, key_block, head_expand = _position_tables(S)

    w_in = nsa_w_in[0].astype(BF16)
    kv0 = NSA_Q_DIM
    kv_cols = lambda c: w_in[:, kv0 + c * NSA_KV_DIM:kv0 + (c + 1) * NSA_KV_DIM]
    wq = w_in[:, :NSA_Q_DIM]
    wk = jnp.stack([kv_cols(0), kv_cols(1), kv_cols(2), kv_cols(4)])
    wvt = jnp.stack([kv_cols(3).T, kv_cols(5).T])
    wg = w_in[:, kv0 + 6 * NSA_KV_DIM:].reshape(D, 3, G, R).transpose(0, 2, 1, 3).reshape(D, G, 3 * R)
    wgt = jnp.pad(wg, ((0, 0), (0, 0), (0, 16 - 3 * R))).reshape(D, G * 16).T
    qg = jnp.tile(nsa_q_gain[0], 4)[None, :]
    kg = jnp.tile(nsa_k_gain[0], 4)[None, :]
    seg = jnp.asarray(np.kron(np.eye(4, dtype=np.float32), np.ones((HEAD_DIM, HEAD_DIM), np.float32)), BF16)
    q, kc, vc, k_sw, vt_sw, gates = _nsa_proj(
        x, jnp.asarray(key_block, BF16), norm_mix_w[0][None, :], wq, wk, wvt, wgt, qg, kg, seg, tm=512)

    kcmp, vcmpt = _nsa_compress(
        kc, vc, cmp_pe_k[0], cmp_pe_v[0], cmp_w1_k[0].astype(BF16), cmp_w1_v[0].astype(BF16),
        cmp_w2_k[0].astype(BF16), cmp_w2_v[0].astype(BF16).T, nsa_k_gain[0][None, :])

    by_code = jnp.concatenate([rel_table.T * LOG2E, jnp.full((NSA_HEADS, 1), MASKED, F32),
                               jnp.zeros((NSA_HEADS, BIAS_CODES - REL_BUCKETS - 1), F32)], axis=1)
    table3 = jnp.concatenate(_split3(by_code), axis=0)
    bcmp = _bias_table(table3, jnp.asarray(cmp_code)).reshape(NSA_HEADS, S // TQC, N_CMP_PAD, TQC)
    battn = _bias_table(table3, jnp.asarray(attn_code)).reshape(NSA_HEADS, -1, TKA, TQA)
    o_cmp, q_aug = _nsa_cmp_select(q, gates, kcmp, vcmpt, bcmp, jnp.asarray(overlap, BF16))
    o = _nsa_attn(q_aug, gates, k_sw, vt_sw, o_cmp, battn)

    def mlp_weights(li):
        return mlp_w_up[li].astype(BF16), mlp_w_down[li].astype(BF16)

    wup, wdn = mlp_weights(0)
    x2 = _mix_out_mlp(x.reshape(T, D), o.reshape(T, NSA_Q_DIM), nsa_w_out[0].astype(BF16),
                      norm_mlp_w[0][None, :], wup, wdn, tm=512)

    w_in = ssd_w_in[0]
    def col_chunks(w):
        return w.reshape(D, -1, SSD_NCHUNK).transpose(1, 0, 2).astype(BF16)

    wz = col_chunks(w_in[:, :SSD_D_INNER])
    wx = col_chunks(w_in[:, SSD_D_INNER:SSD_D_INNER + SSD_CONV_DIM])
    lane_pad = LANES - SSD_HEADS
    wdt = jnp.pad(w_in[:, SSD_D_INNER + SSD_CONV_DIM:], ((0, 0), (0, lane_pad))).astype(BF16)
    pad_heads = lambda v: jnp.pad(v, (0, lane_pad))[None, :]
    z, xbc, dt = _ssd_proj(x2.reshape(B, S, D), norm_mix_w[1][None, :], wz, wx, wdt,
                           ssd_conv_w[0], ssd_conv_b[0][None, :], pad_heads(ssd_dt_bias[0]), tm=512)
    tril = jnp.asarray(np.tril(np.ones((SSD_CHUNK, SSD_CHUNK), np.float32)), BF16)
    dskip = jnp.repeat(ssd_d[0], SSD_HEAD_DIM)[None, :]
    y = _ssd_scan(xbc, dt, z, pad_heads(ssd_a_log[0]), dskip, ssd_norm_w[0][None, :], tril,
                  jnp.asarray(head_expand, BF16))

    wup, wdn = mlp_weights(1)
    x4 = _mix_out_mlp(x2, y.reshape(T, SSD_D_INNER), ssd_w_out[0].astype(BF16),
                      norm_mlp_w[1][None, :], wup, wdn, tm=512)
    return x4.reshape(B, S, D)
```

```python
import functools
import math

import numpy as np
import jax
import jax.numpy as jnp
from jax import lax
from jax.experimental import pallas as pl
from jax.experimental.pallas import tpu as pltpu

F32 = jnp.float32
BF16 = jnp.bfloat16

D_MODEL = 1024
EPS = 1e-6

NSA_HEADS = 16
NSA_GROUPS = 4
NSA_REP = NSA_HEADS // NSA_GROUPS
HEAD_DIM = 64
NSA_Q_DIM = NSA_HEADS * HEAD_DIM
NSA_KV_DIM = NSA_GROUPS * HEAD_DIM
CMP_BLOCK = 32
CMP_STRIDE = 16
CMP_HIDDEN = 256
SEL_BLOCK = 64
SEL_TOP_N = 16
WINDOW = 512
REL_BUCKETS = 32
REL_MAX_DIST = 128

SSD_D_INNER = 2 * D_MODEL
SSD_HEAD_DIM = 64
SSD_HEADS = SSD_D_INNER // SSD_HEAD_DIM
SSD_GROUPS = 8
SSD_STATE = 128
SSD_CONV = 4
SSD_CHUNK = 128
SSD_CONV_DIM = SSD_D_INNER + 2 * SSD_GROUPS * SSD_STATE
MLP_HIDDEN = 4 * D_MODEL

LANES = 128
SUBLANES = 8
VMEM_LIMIT_BYTES = 56 * 1024 * 1024

VT_BLOCK = 128
TQC = 512
TQA = 256
TKA = 256
N_CMP_PAD = 128
N_SEL_PAD = 128
MASKED = -1e30
LOG2E = math.log2(math.e)
SCORE_BIG = 3e38
N_SEL_VARIANTS = 3
N_WIN_TILES = WINDOW // TKA + 1
ACC_ROWS = HEAD_DIM + 16

NT_DIMS = (((1,), (1,)), ((), ()))


def _dot(a, b):
    return jnp.dot(a, b, preferred_element_type=F32)


def _dot_nt(a, b):
    return lax.dot_general(a, b, NT_DIMS, preferred_element_type=F32)


def _split3(x):
    hi = x.astype(BF16)
    r1 = x - hi.astype(F32)
    mid = r1.astype(BF16)
    lo = (r1 - mid.astype(F32)).astype(BF16)
    return hi, mid, lo


def _rms_rows(x, w):
    return x * lax.rsqrt(jnp.mean(x * x, axis=-1, keepdims=True) + EPS) * w


def _silu(x):
    h = 0.5 * x
    return h + h * jnp.tanh(h)


def _const_spec(shape):
    nd = len(shape)
    return pl.BlockSpec(shape, lambda *_: (0,) * nd, pipeline_mode=pl.Buffered(1))


def _compiler_params(semantics):
    return pltpu.CompilerParams(dimension_semantics=semantics, vmem_limit_bytes=VMEM_LIMIT_BYTES)


def _nsa_proj_kernel(x_ref, blk_ref, nw_ref, wq_ref, wk_ref, wvt_ref, wgt_ref, qg_ref, kg_ref, seg_ref,
                     q_ref, kc_ref, vc_ref, k_ref, vt_ref, gate_ref):
    tm = x_ref.shape[1]
    h = _rms_rows(x_ref[0], nw_ref[...]).astype(BF16)
    seg = seg_ref[...]

    def head_rms(y, gain):
        sq = y * y
        hi = sq.astype(BF16)
        lo = (sq - hi.astype(F32)).astype(BF16)
        ss = _dot(hi, seg) + _dot(lo, seg)
        return y * lax.rsqrt(ss * (1.0 / HEAD_DIM) + EPS) * gain

    def emit_q(c, y):
        y = (head_rms(y, qg_ref[...]) * (HEAD_DIM ** -0.5 * LOG2E)).astype(BF16)
        for r in range(4):
            q_ref[0, 4 * c + r] = y[:, 64 * r:64 * r + 64]

    def emit_raw(c, y):
        raw_ref = (kc_ref, vc_ref)[c]
        for g in range(NSA_GROUPS):
            raw_ref[0, g] = y[:, 64 * g:64 * g + 64]

    def emit_k(br, y):
        y = head_rms(y, kg_ref[...]).astype(BF16)
        tag = blk_ref[...] if br == 0 else jnp.zeros(blk_ref.shape, BF16)
        for g in range(NSA_GROUPS):
            k_ref[0, g, br, :, 0:HEAD_DIM] = y[:, 64 * g:64 * g + 64]
            k_ref[0, g, br, :, HEAD_DIM:2 * HEAD_DIM] = tag

    def emit_vt(br, yt):
        yt = yt.astype(BF16)
        for g in range(NSA_GROUPS):
            for kb in range(tm // VT_BLOCK):
                vt_ref[0, g, br, kb] = yt[64 * g:64 * g + 64, VT_BLOCK * kb:VT_BLOCK * (kb + 1)]

    def emit_gates(_, yt):
        gate_ref[0] = jax.nn.sigmoid(yt)

    stages = ([(lambda c=c: _dot(h, wq_ref[:, 256 * c:256 * (c + 1)]), emit_q, c) for c in range(NSA_Q_DIM // 256)]
              + [(lambda c=c: _dot(h, wk_ref[c]), emit_raw, c) for c in range(2)]
              + [(lambda br=br: _dot(h, wk_ref[2 + br]), emit_k, br) for br in range(2)]
              + [(lambda br=br: _dot_nt(wvt_ref[br], h), emit_vt, br) for br in range(2)]
              + [(lambda: _dot_nt(wgt_ref[...], h), emit_gates, 0)])
    pending = stages[0][0]()
    for n, (_, emit, arg) in enumerate(stages):
        y = pending
        if n + 1 < len(stages):
            pending = stages[n + 1][0]()
        emit(arg, y)


def _nsa_proj(x, key_block, nw, wq, wk, wvt, wgt, qg, kg, seg, tm):
    B, S, D = x.shape
    G = NSA_GROUPS
    raw_block = pl.BlockSpec((1, G, tm, HEAD_DIM), lambda b, i: (b, 0, i, 0))
    raw_shape = jax.ShapeDtypeStruct((B, G, S, HEAD_DIM), F32)
    consts = (nw, wq, wk, wvt, wgt, qg, kg, seg)
    return pl.pallas_call(
        _nsa_proj_kernel,
        grid=(B, S // tm),
        in_specs=[pl.BlockSpec((1, tm, D), lambda b, i: (b, i, 0)),
                  pl.BlockSpec((tm, HEAD_DIM), lambda b, i: (i, 0))] + [_const_spec(c.shape) for c in consts],
        out_specs=[
            pl.BlockSpec((1, NSA_HEADS, tm, HEAD_DIM), lambda b, i: (b, 0, i, 0)),
            raw_block, raw_block,
            pl.BlockSpec((1, G, 2, tm, 2 * HEAD_DIM), lambda b, i: (b, 0, 0, i, 0)),
            pl.BlockSpec((1, G, 2, tm // VT_BLOCK, HEAD_DIM, VT_BLOCK), lambda b, i: (b, 0, 0, i, 0, 0)),
            pl.BlockSpec((1, G * 16, tm), lambda b, i: (b, 0, i)),
        ],
        out_shape=[
            jax.ShapeDtypeStruct((B, NSA_HEADS, S, HEAD_DIM), BF16),
            raw_shape, raw_shape,
            jax.ShapeDtypeStruct((B, G, 2, S, 2 * HEAD_DIM), BF16),
            jax.ShapeDtypeStruct((B, G, 2, S // VT_BLOCK, HEAD_DIM, VT_BLOCK), BF16),
            jax.ShapeDtypeStruct((B, G * 16, S), F32),
        ],
        compiler_params=_compiler_params(("parallel", "parallel")),
        name="nsa_proj",
    )(x, key_block, *consts)


def _nsa_compress_kernel(kc_ref, vc_ref, pek_ref, pev_ref, w1k_ref, w1v_ref, w2k_ref, w2vt_ref, kg_ref,
                         kcmp_ref, vcmp_ref):
    half = CMP_BLOCK // 2

    def hidden(src_ref, pe_ref, w1_ref):
        acc_a = jnp.zeros((N_CMP_PAD, CMP_HIDDEN), F32)
        acc_b = jnp.zeros((N_CMP_PAD, CMP_HIDDEN), F32)
        for l in range(half):
            rows = src_ref[0, 0, pl.ds(l, N_CMP_PAD, stride=CMP_STRIDE), :]
            acc_a += _dot((rows + pe_ref[l:l + 1, :]).astype(BF16), w1_ref[l])
            acc_b += _dot((rows + pe_ref[half + l:half + l + 1, :]).astype(BF16), w1_ref[half + l])
        hid = acc_a + pltpu.roll(acc_b, N_CMP_PAD - 1, axis=0)
        return _silu(hid).astype(BF16)

    k = _dot(hidden(kc_ref, pek_ref, w1k_ref), w2k_ref[...])
    kcmp_ref[0, 0] = _rms_rows(k, kg_ref[...]).astype(BF16)
    vcmp_ref[0, 0] = _dot_nt(w2vt_ref[...], hidden(vc_ref, pev_ref, w1v_ref)).astype(BF16)


def _nsa_compress(kc, vc, pek, pev, w1k, w1v, w2k, w2vt, kg):
    B, G, S, Dh = kc.shape
    src = pl.BlockSpec((1, 1, S, Dh), lambda b, g: (b, g, 0, 0))
    consts = (pek, pev, w1k, w1v, w2k, w2vt, kg)
    return pl.pallas_call(
        _nsa_compress_kernel,
        grid=(B, G),
        in_specs=[src, src] + [_const_spec(c.shape) for c in consts],
        out_specs=[pl.BlockSpec((1, 1, N_CMP_PAD, Dh), lambda b, g: (b, g, 0, 0)),
                   pl.BlockSpec((1, 1, Dh, N_CMP_PAD), lambda b, g: (b, g, 0, 0))],
        out_shape=[jax.ShapeDtypeStruct((B, G, N_CMP_PAD, Dh), BF16),
                   jax.ShapeDtypeStruct((B, G, Dh, N_CMP_PAD), BF16)],
        compiler_params=_compiler_params(("parallel", "parallel")),
        name="nsa_compress",
    )(kc, vc, *consts)


BIAS_CHUNK = 8192
BIAS_CODES = 64


def _bias_table_kernel(t3_ref, code_ref, out_ref):
    code = code_ref[...]
    ids = lax.broadcasted_iota(jnp.int32, (BIAS_CODES, code.shape[1]), 0)
    onehot = jnp.where(ids == code, 1.0, 0.0).astype(BF16)
    y = _dot(t3_ref[...], onehot)
    out_ref[...] = y[0:NSA_HEADS] + y[NSA_HEADS:2 * NSA_HEADS] + y[2 * NSA_HEADS:3 * NSA_HEADS]


def _bias_table(table3, code):
    n = code.shape[1]
    return pl.pallas_call(
        _bias_table_kernel,
        grid=(n // BIAS_CHUNK,),
        in_specs=[_const_spec(table3.shape), pl.BlockSpec((1, BIAS_CHUNK), lambda i: (0, i))],
        out_specs=pl.BlockSpec((NSA_HEADS, BIAS_CHUNK), lambda i: (0, i)),
        out_shape=jax.ShapeDtypeStruct((NSA_HEADS, n), F32),
        compiler_params=_compiler_params(("parallel",)),
        name="rel_bias_table",
    )(table3, code)


def _nsa_cmp_select_kernel(q_ref, gate_ref, kcmp_ref, vcmpt_ref, bias_ref, ovl_ref, o_ref, qa_ref):
    i = pl.program_id(2)
    kcmp = kcmp_ref[0, 0]
    vcmpt = vcmpt_ref[0, 0]

    scores = [_dot_nt(kcmp, q_ref[0, r]) for r in range(NSA_REP)]
    probs = []
    for r in range(NSA_REP):
        s = scores[r] + bias_ref[r, 0]
        m = jnp.max(s, axis=0, keepdims=True)
        m = jnp.where(m > 0.5 * MASKED, m, 0.0)
        e = jnp.exp2(s - m)
        den = jnp.sum(e, axis=0, keepdims=True)
        probs.append(e / jnp.where(den > 0.0, den, 1.0))
    outs = [_dot(vcmpt, p.astype(BF16)) for p in probs]

    ph, pm, plo = _split3(probs[0] + probs[1] + probs[2] + probs[3])
    ovl = ovl_ref[...]
    n_sel = 32
    imp = (_dot(ovl, ph) + _dot(ovl, pm) + _dot(ovl, plo))[0:n_sel]
    tok = i * TQC + lax.broadcasted_iota(jnp.int32, (n_sel, TQC), 1)
    blk = lax.broadcasted_iota(jnp.int32, (n_sel, TQC), 0)
    cur = tok // SEL_BLOCK
    forced = (blk == 0) | (blk == cur) | (blk == cur - 1)
    valid = blk * SEL_BLOCK <= tok
    score = jnp.where(forced, SCORE_BIG, imp)
    score = jnp.where(valid, score, -SCORE_BIG)
    groups = [slice(SUBLANES * v, SUBLANES * (v + 1)) for v in range(n_sel // SUBLANES)]
    ranks = [jnp.zeros((SUBLANES, TQC), F32) for _ in groups]
    for jp in range(n_sel):
        other = score[jp:jp + 1, :]
        for v, rows in enumerate(groups):
            mine = score[rows]
            if rows.start > jp:
                beats = other >= mine
            elif rows.stop - 1 <= jp:
                beats = other > mine
            else:
                later = lax.broadcasted_iota(jnp.int32, (SUBLANES, TQC), 0) > jp - rows.start
                beats = (other > mine) | ((other == mine) & later)
            ranks[v] = ranks[v] + jnp.where(beats, 1.0, 0.0)
    rank = jnp.concatenate(ranks, axis=0)
    blk_mask = jnp.where(rank < SEL_TOP_N, 0.0, MASKED)
    blk_mask = jnp.concatenate([blk_mask, jnp.zeros((HEAD_DIM - n_sel, TQC), F32)], axis=0).T.astype(BF16)
    for r in range(NSA_REP):
        qa_ref[0, r, :, 0:HEAD_DIM] = q_ref[0, r]
        qa_ref[0, r, :, HEAD_DIM:2 * HEAD_DIM] = blk_mask

    gate = gate_ref[0]
    for r in range(NSA_REP):
        o_ref[0, :, r * HEAD_DIM:(r + 1) * HEAD_DIM] = (outs[r] * gate[r:r + 1]).T


def _nsa_cmp_select(q, gates, kcmp, vcmpt, bcmp, ovl):
    B, H, S, Dh = q.shape
    G = NSA_GROUPS
    return pl.pallas_call(
        _nsa_cmp_select_kernel,
        grid=(G, B, S // TQC),
        in_specs=[
            pl.BlockSpec((1, NSA_REP, TQC, Dh), lambda g, b, i: (b, g, i, 0)),
            pl.BlockSpec((1, 16, TQC), lambda g, b, i: (b, g, i)),
            pl.BlockSpec((1, 1, N_CMP_PAD, Dh), lambda g, b, i: (b, g, 0, 0)),
            pl.BlockSpec((1, 1, Dh, N_CMP_PAD), lambda g, b, i: (b, g, 0, 0)),
            pl.BlockSpec((NSA_REP, 1, N_CMP_PAD, TQC), lambda g, b, i: (g, i, 0, 0)),
            _const_spec(ovl.shape),
        ],
        out_specs=[
            pl.BlockSpec((1, TQC, NSA_REP * Dh), lambda g, b, i: (b, i, g)),
            pl.BlockSpec((1, NSA_REP, TQC, 2 * Dh), lambda g, b, i: (b, g, i, 0)),
        ],
        out_shape=[
            jax.ShapeDtypeStruct((B, S, H * Dh), F32),
            jax.ShapeDtypeStruct((B, H, S, 2 * Dh), BF16),
        ],
        compiler_params=_compiler_params(("parallel", "parallel", "arbitrary")),
        name="nsa_cmp_select",
    )(q, gates, kcmp, vcmpt, bcmp, ovl)


def _nsa_attn_kernel(q_ref, gate_ref, k_ref, vt_ref, ocmp_ref, bias_ref,
                     o_ref, s0_ref, s1_ref, x0_ref, x1_ref, p0_ref, p1_ref, a0_ref, a1_ref,
                     m_ref, acc_ref):
    i = pl.program_id(2)
    last_step = pl.num_programs(2) - 1
    s_bufs, x_bufs = (s0_ref, s1_ref), (x0_ref, x1_ref)
    p_bufs, a_bufs = (p0_ref, p1_ref), (a0_ref, a1_ref)
    masked_variant = N_SEL_VARIANTS + N_WIN_TILES

    def n_tasks_of(step):
        n_sel = step + 1
        return n_sel, n_sel + jnp.minimum(N_WIN_TILES, n_sel)

    def task(t, step):
        n_sel, n_tasks = n_tasks_of(step)
        is_win = t >= n_sel
        c = jnp.clip(jnp.where(is_win, step - (t - n_sel), t), 0, step)
        behind = step - c
        variant = jnp.where(is_win, N_SEL_VARIANTS + behind, jnp.minimum(behind, N_SEL_VARIANTS - 1))
        variant = jnp.where(t < n_tasks, variant, masked_variant)
        return is_win.astype(jnp.int32), c, variant

    def score_pair(u, step):
        pair = [task(2 * u, step), task(2 * u + 1, step)]
        q0 = pl.multiple_of(step * TQA, TQA)
        k = jnp.concatenate([k_ref[0, 0, br, pl.ds(pl.multiple_of(c * TKA, TKA), TKA), :] for br, c, _ in pair],
                            axis=0)
        for r in range(NSA_REP):
            s2 = _dot_nt(k, q_ref[0, r, pl.ds(q0, TQA), :])
            for slot, (_, _, variant) in enumerate(pair):
                s = s2[slot * TKA:(slot + 1) * TKA] + bias_ref[r, variant]
                s_bufs[slot][r] = s
                x_bufs[slot][r] = jnp.max(s, axis=0, keepdims=True)

    def softmax_stage(t, slot):
        br, _, _ = task(t, i)
        for r in range(NSA_REP):
            m_old = m_ref[br, r]
            m_new = jnp.maximum(m_old, x_bufs[slot][r])
            m_ref[br, r] = m_new
            a_bufs[slot][r] = jnp.exp2(m_old - m_new)
            p_bufs[slot][r] = jnp.exp2(s_bufs[slot][r] - m_new).astype(BF16)

    ones_rows = jnp.where(lax.broadcasted_iota(jnp.int32, (ACC_ROWS - HEAD_DIM, TKA), 0) == 0, 1.0, 0.0).astype(BF16)

    def value_stage(t, slot):
        br, c, _ = task(t, i)
        blocks = TKA // VT_BLOCK
        vt = jnp.concatenate([vt_ref[0, 0, br, blocks * c + u] for u in range(blocks)], axis=1)
        vt = jnp.concatenate([vt, ones_rows], axis=0)
        for r in range(NSA_REP):
            acc_ref[br, r] = a_bufs[slot][r] * acc_ref[br, r] + _dot(vt, p_bufs[slot][r])

    m_ref[...] = jnp.full(m_ref.shape, MASKED, F32)
    acc_ref[...] = jnp.zeros(acc_ref.shape, F32)

    @pl.when(i == 0)
    def _():
        score_pair(0, i)

    n_trips = (n_tasks_of(i)[1] + 1) // 2

    def softmax_and_next_scores(u):
        softmax_stage(2 * u, 0)
        softmax_stage(2 * u + 1, 1)
        in_tile = u + 1 < n_trips
        score_pair(jnp.where(in_tile, u + 1, 0), jnp.where(in_tile, i, jnp.minimum(i + 1, last_step)))

    def trip(u, carry):
        value_stage(2 * u - 2, 0)
        value_stage(2 * u - 1, 1)
        softmax_and_next_scores(u)
        return carry

    softmax_and_next_scores(0)
    lax.fori_loop(1, n_trips, trip, 0)
    value_stage(2 * n_trips - 2, 0)
    value_stage(2 * n_trips - 1, 1)

    gate = gate_ref[0]
    for r in range(NSA_REP):
        w_sel = gate[4 + r:5 + r] / acc_ref[0, r, HEAD_DIM:HEAD_DIM + 1, :]
        w_win = gate[8 + r:9 + r] / acc_ref[1, r, HEAD_DIM:HEAD_DIM + 1, :]
        o_t = w_sel * acc_ref[0, r, 0:HEAD_DIM, :] + w_win * acc_ref[1, r, 0:HEAD_DIM, :]
        cols = slice(r * HEAD_DIM, (r + 1) * HEAD_DIM)
        o_ref[0, :, cols] = (ocmp_ref[0, :, cols] + o_t.T).astype(BF16)


def _nsa_attn(q_aug, gates, k_sw, vt_sw, o_cmp, bias):
    B, H, S, Dq = q_aug.shape
    Dh = HEAD_DIM
    G = NSA_GROUPS
    n_vt = S // VT_BLOCK
    n_var = bias.shape[1]
    return pl.pallas_call(
        _nsa_attn_kernel,
        grid=(G, B, S // TQA),
        in_specs=[
            pl.BlockSpec((1, NSA_REP, S, Dq), lambda g, b, i: (b, g, 0, 0)),
            pl.BlockSpec((1, 16, TQA), lambda g, b, i: (b, g, i)),
            pl.BlockSpec((1, 1, 2, S, Dq), lambda g, b, i: (b, g, 0, 0, 0)),
            pl.BlockSpec((1, 1, 2, n_vt, Dh, VT_BLOCK), lambda g, b, i: (b, g, 0, 0, 0, 0)),
            pl.BlockSpec((1, TQA, NSA_REP * Dh), lambda g, b, i: (b, i, g)),
            pl.BlockSpec((NSA_REP, n_var, TKA, TQA), lambda g, b, i: (g, 0, 0, 0), pipeline_mode=pl.Buffered(1)),
        ],
        out_specs=pl.BlockSpec((1, TQA, NSA_REP * Dh), lambda g, b, i: (b, i, g)),
        out_shape=jax.ShapeDtypeStruct((B, S, H * Dh), BF16),
        scratch_shapes=[
            pltpu.VMEM((NSA_REP, TKA, TQA), F32),
            pltpu.VMEM((NSA_REP, TKA, TQA), F32),
            pltpu.VMEM((NSA_REP, 1, TQA), F32),
            pltpu.VMEM((NSA_REP, 1, TQA), F32),
            pltpu.VMEM((NSA_REP, TKA, TQA), BF16),
            pltpu.VMEM((NSA_REP, TKA, TQA), BF16),
            pltpu.VMEM((NSA_REP, 1, TQA), F32),
            pltpu.VMEM((NSA_REP, 1, TQA), F32),
            pltpu.VMEM((2, NSA_REP, 1, TQA), F32),
            pltpu.VMEM((2, NSA_REP, ACC_ROWS, TQA), F32),
        ],
        compiler_params=_compiler_params(("parallel", "parallel", "arbitrary")),
        name="nsa_attn",
    )(q_aug, gates, k_sw, vt_sw, o_cmp, bias)


MLP_CHUNK = 1024


def _mix_out_mlp_kernel(x_ref, a_ref, wo_ref, nw_ref, wup_ref, wdn_ref, o_ref):
    x1 = x_ref[...] + _dot(a_ref[...], wo_ref[...])
    h = _rms_rows(x1, nw_ref[...]).astype(BF16)

    acc = None
    for c in range(wup_ref.shape[1] // MLP_CHUNK):
        hid = slice(c * MLP_CHUNK, (c + 1) * MLP_CHUNK)
        u = jnp.maximum(_dot(h, wup_ref[:, hid]), 0.0)
        d = _dot((u * u).astype(BF16), wdn_ref[hid, :])
        acc = d if acc is None else acc + d
    o_ref[...] = x1 + acc


def _mix_out_mlp(x, a, wo, nw, wup, wdn, tm):
    T, D = x.shape
    K = a.shape[1]
    return pl.pallas_call(
        _mix_out_mlp_kernel,
        grid=(T // tm,),
        in_specs=[
            pl.BlockSpec((tm, D), lambda i: (i, 0)),
            pl.BlockSpec((tm, K), lambda i: (i, 0)),
            _const_spec(wo.shape), _const_spec(nw.shape), _const_spec(wup.shape), _const_spec(wdn.shape),
        ],
        out_specs=pl.BlockSpec((tm, D), lambda i: (i, 0)),
        out_shape=jax.ShapeDtypeStruct((T, D), F32),
        compiler_params=_compiler_params(("parallel",)),
        name="mix_out_mlp",
    )(x, a, wo, nw, wup, wdn)


SSD_NCHUNK = 512


def _ssd_proj_kernel(x_ref, nw_ref, wz_ref, wx_ref, wdt_ref, cw_ref, cb_ref, dtb_ref,
                     z_ref, xbc_ref, dt_ref, buf_ref):
    i = pl.program_id(1)
    tm = x_ref.shape[1]
    @pl.when(i == 0)
    def _():
        buf_ref[0:SUBLANES, :] = jnp.zeros((SUBLANES, buf_ref.shape[1]), F32)

    @pl.when(i > 0)
    def _():
        buf_ref[0:SUBLANES, :] = buf_ref[tm:tm + SUBLANES, :]

    h = _rms_rows(x_ref[0], nw_ref[...]).astype(BF16)

    for c in range(wz_ref.shape[0]):
        z_ref[0, :, c * SSD_NCHUNK:(c + 1) * SSD_NCHUNK] = _dot(h, wz_ref[c])

    for c in range(wx_ref.shape[0]):
        cols = slice(c * SSD_NCHUNK, (c + 1) * SSD_NCHUNK)
        buf_ref[SUBLANES:SUBLANES + tm, cols] = _dot(h, wx_ref[c])
        acc = cb_ref[:, cols] + buf_ref[SUBLANES:SUBLANES + tm, cols] * cw_ref[SSD_CONV - 1:SSD_CONV, cols]
        for k in range(SSD_CONV - 1):
            off = SUBLANES - (SSD_CONV - 1) + k
            acc += buf_ref[off:off + tm, cols] * cw_ref[k:k + 1, cols]
        xbc_ref[0, :, cols] = _silu(acc)

    dt_ref[0] = jax.nn.softplus(_dot(h, wdt_ref[...]) + dtb_ref[...])


def _ssd_proj(x, nw, wz, wx, wdt, cw, cb, dtb, tm):
    B, S, D = x.shape
    consts = (nw, wz, wx, wdt, cw, cb, dtb)
    return pl.pallas_call(
        _ssd_proj_kernel,
        grid=(B, S // tm),
        in_specs=[pl.BlockSpec((1, tm, D), lambda b, i: (b, i, 0))] + [_const_spec(c.shape) for c in consts],
        out_specs=[
            pl.BlockSpec((1, tm, SSD_D_INNER), lambda b, i: (b, i, 0)),
            pl.BlockSpec((1, tm, SSD_CONV_DIM), lambda b, i: (b, i, 0)),
            pl.BlockSpec((1, tm, LANES), lambda b, i: (b, i, 0)),
        ],
        out_shape=[
            jax.ShapeDtypeStruct((B, S, SSD_D_INNER), F32),
            jax.ShapeDtypeStruct((B, S, SSD_CONV_DIM), F32),
            jax.ShapeDtypeStruct((B, S, LANES), F32),
        ],
        scratch_shapes=[pltpu.VMEM((tm + SUBLANES, SSD_CONV_DIM), F32)],
        compiler_params=_compiler_params(("parallel", "arbitrary")),
        name="ssd_proj",
    )(x, *consts)


def _ssd_scan_kernel(xbc_ref, dt_ref, z_ref, alog_ref, dskip_ref, nw_ref, tril_ref, hexp_ref,
                     y_ref, state_ref, acs_ref, acst_ref, dtt_ref, e3_ref, wt_ref):
    L = SSD_CHUNK
    P = SSD_HEAD_DIM
    N = SSD_STATE
    hpg = SSD_HEADS // SSD_GROUPS
    gw = hpg * P
    chunk = pl.program_id(1)

    def chunk_decays(ci):
        dt = dt_ref[0, pl.ds(pl.multiple_of(ci * L, L), L), :]
        a = dt * (-jnp.exp(alog_ref[...]))
        tril = tril_ref[...]
        ah, am, al = _split3(a)
        acs = _dot(tril, ah) + _dot(tril, am) + _dot(tril, al)
        acs_t = acs.T
        dt_t = dt.T
        acs_ref[...] = acs
        acst_ref[...] = acs_t
        dtt_ref[...] = dt_t
        for n, term in enumerate(_split3(jnp.exp(acs))):
            e3_ref[n] = term
        wt_ref[...] = jnp.exp(acs_t[:, L - 1:L] - acs_t) * dt_t

    @pl.when(chunk == 0)
    def _():
        state_ref[...] = jnp.zeros(state_ref.shape, F32)
        chunk_decays(0)

    acs, acs_t, dt_t, w_t = acs_ref[...], acst_ref[...], dtt_ref[...], wt_ref[...]
    eh, em, el = e3_ref[0], e3_ref[1], e3_ref[2]
    causal = (lax.broadcasted_iota(jnp.int32, (L, L), 0) >= lax.broadcasted_iota(jnp.int32, (L, L), 1))
    head_of_lane = lax.broadcasted_iota(jnp.int32, (L, gw), 1) // P

    for g in range(SSD_GROUPS):
        b_off = SSD_D_INNER + g * N
        c_off = SSD_D_INNER + SSD_GROUPS * N + g * N
        bm = xbc_ref[0, :, b_off:b_off + N]
        cm = xbc_ref[0, :, c_off:c_off + N].astype(BF16)
        cb = _dot_nt(cm, bm.astype(BF16))
        bm_t = bm.T
        xg = xbc_ref[0, :, g * gw:(g + 1) * gw]
        hexp = hexp_ref[g]
        e_exp = _dot(eh, hexp) + _dot(em, hexp) + _dot(el, hexp)
        st = state_ref[g]
        y = _dot(cm, st.astype(BF16)) * e_exp
        st = st * e_exp[L - 1:L, :]
        for j in range(hpg):
            hd = g * hpg + j
            xj = jnp.where(head_of_lane == j, xg, 0.0).astype(BF16)
            diff = acs[:, hd:hd + 1] - acs_t[hd:hd + 1, :]
            decay = jnp.exp(jnp.where(causal, diff, -jnp.inf))
            y = y + _dot((cb * decay * dt_t[hd:hd + 1, :]).astype(BF16), xj)
            st = st + _dot((bm_t * w_t[hd:hd + 1, :]).astype(BF16), xj)
        state_ref[g] = st
        cols = slice(g * gw, (g + 1) * gw)
        zg = z_ref[0, :, cols]
        y = (y + xg * dskip_ref[:, cols]) * _silu(zg)
        y_ref[0, :, cols] = _rms_rows(y, nw_ref[:, cols]).astype(BF16)

    chunk_decays(jnp.minimum(chunk + 1, pl.num_programs(1) - 1))


def _ssd_scan(xbc, dt, z, alog, dskip, nw, tril, hexp):
    B, S, _ = xbc.shape
    L = SSD_CHUNK
    consts = (alog, dskip, nw, tril, hexp)
    return pl.pallas_call(
        _ssd_scan_kernel,
        grid=(B, S // L),
        in_specs=[
            pl.BlockSpec((1, L, SSD_CONV_DIM), lambda b, c: (b, c, 0)),
            pl.BlockSpec((1, S, LANES), lambda b, c: (b, 0, 0)),
            pl.BlockSpec((1, L, SSD_D_INNER), lambda b, c: (b, c, 0)),
        ] + [_const_spec(c.shape) for c in consts],
        out_specs=pl.BlockSpec((1, L, SSD_D_INNER), lambda b, c: (b, c, 0)),
        out_shape=jax.ShapeDtypeStruct((B, S, SSD_D_INNER), BF16),
        scratch_shapes=[
            pltpu.VMEM((SSD_GROUPS, SSD_STATE, SSD_D_INNER // SSD_GROUPS), F32),
            pltpu.VMEM((L, LANES), F32),
            pltpu.VMEM((LANES, L), F32),
            pltpu.VMEM((LANES, L), F32),
            pltpu.VMEM((3, L, LANES), BF16),
            pltpu.VMEM((LANES, L), F32),
        ],
        compiler_params=_compiler_params(("parallel", "arbitrary")),
        name="ssd_scan",
    )(xbc, dt, z, *consts)


def _rel_bucket_np(dist):
    max_exact = REL_BUCKETS // 2
    d = np.maximum(dist, 0)
    df = np.maximum(d, 1).astype(np.float32)
    large = max_exact + (np.log(df / np.float32(max_exact)) / np.float32(math.log(REL_MAX_DIST / max_exact))
                         * np.float32(REL_BUCKETS - max_exact)).astype(np.int32)
    large = np.minimum(large, REL_BUCKETS - 1)
    return np.where(d < max_exact, d, large).astype(np.int32)


@functools.lru_cache(maxsize=None)
def _position_tables(S):
    n_cmp = (S - CMP_BLOCK) // CMP_STRIDE + 1
    n_sel = S // SEL_BLOCK

    masked_code = REL_BUCKETS

    def code(dist, ok):
        return np.where(ok, _rel_bucket_np(dist), masked_code).astype(np.int32).reshape(1, -1)

    n = np.arange(N_CMP_PAD)[None, :, None]
    t = np.arange(S // TQC)[:, None, None] * TQC + np.arange(TQC)[None, None, :]
    dist = t - (n * CMP_STRIDE + CMP_BLOCK - 1)
    cmp_code = code(dist, (dist >= 0) & (n < n_cmp))
    assert TQA == TKA and (N_SEL_VARIANTS - 1) * TQA - (TKA - 1) >= REL_MAX_DIST
    ti = np.arange(TQA)[None, :]
    j = np.arange(TKA)[:, None]
    tile_dist = [TQA * behind + ti - j for behind in range(N_WIN_TILES)]
    far = np.full((TKA, TQA), 2 * REL_MAX_DIST)
    sel_dist = np.stack(tile_dist[:N_SEL_VARIANTS - 1] + [far])
    win_dist = np.stack(tile_dist)
    attn_code = np.concatenate([code(sel_dist, sel_dist >= 0),
                                code(win_dist, (win_dist >= 0) & (win_dist < WINDOW)),
                                np.full((1, TKA * TQA), masked_code, np.int32)],
                               axis=1)
    jb = np.arange(N_SEL_PAD)[:, None]
    nn = np.arange(N_CMP_PAD)[None, :]
    overlap = ((nn * CMP_STRIDE < jb * SEL_BLOCK + SEL_BLOCK) & (nn * CMP_STRIDE + CMP_BLOCK > jb * SEL_BLOCK)
               & (nn < n_cmp) & (jb < n_sel)).astype(np.float32)
    key_block = (np.arange(S)[:, None] // SEL_BLOCK == np.arange(HEAD_DIM)[None, :]).astype(np.float32)
    hpg = SSD_HEADS // SSD_GROUPS
    lane_head = np.arange(hpg * SSD_HEAD_DIM)[None, None, :] // SSD_HEAD_DIM
    head_expand = (np.arange(LANES)[None, :, None]
                   == np.arange(SSD_GROUPS)[:, None, None] * hpg + lane_head).astype(np.float32)
    return cmp_code, attn_code, overlap, key_block, head_expand


def kernel(x, norm_mix_w, norm_mlp_w, rel_table, nsa_w_in, nsa_q_gain, nsa_k_gain, cmp_pe_k, cmp_w1_k, cmp_w2_k, cmp_pe_v, cmp_w1_v, cmp_w2_v, nsa_w_out, ssd_w_in, ssd_conv_w, ssd_conv_b, ssd_dt_bias, ssd_a_log, ssd_d, ssd_norm_w, ssd_w_out, mlp_w_up, mlp_w_down):
    B, S, D = x.shape
    T = B * S
    G, R = NSA_GROUPS, NSA_REP
    cmp_code, attn_code, overlap, key_block, head_expand = _position_tables(S)

    w_in = nsa_w_in[0].astype(BF16)
    kv0 = NSA_Q_DIM
    kv_cols = lambda c: w_in[:, kv0 + c * NSA_KV_DIM:kv0 + (c + 1) * NSA_KV_DIM]
    wq = w_in[:, :NSA_Q_DIM]
    wk = jnp.stack([kv_cols(0), kv_cols(1), kv_cols(2), kv_cols(4)])
    wvt = jnp.stack([kv_cols(3).T, kv_cols(5).T])
    wg = w_in[:, kv0 + 6 * NSA_KV_DIM:].reshape(D, 3, G, R).transpose(0, 2, 1, 3).reshape(D, G, 3 * R)
    wgt = jnp.pad(wg, ((0, 0), (0, 0), (0, 16 - 3 * R))).reshape(D, G * 16).T
    qg = jnp.tile(nsa_q_gain[0], 4)[None, :]
    kg = jnp.tile(nsa_k_gain[0], 4)[None, :]
    seg = jnp.asarray(np.kron(np.eye(4, dtype=np.float32), np.ones((HEAD_DIM, HEAD_DIM), np.float32)), BF16)
    q, kc, vc, k_sw, vt_sw, gates = _nsa_proj(
        x, jnp.asarray(key_block, BF16), norm_mix_w[0][None, :], wq, wk, wvt, wgt, qg, kg, seg, tm=512)

    kcmp, vcmpt = _nsa_compress(
        kc, vc, cmp_pe_k[0], cmp_pe_v[0], cmp_w1_k[0].astype(BF16), cmp_w1_v[0].astype(BF16),
        cmp_w2_k[0].astype(BF16), cmp_w2_v[0].astype(BF16).T, nsa_k_gain[0][None, :])

    by_code = jnp.concatenate([rel_table.T * LOG2E, jnp.full((NSA_HEADS, 1), MASKED, F32),
                               jnp.zeros((NSA_HEADS, BIAS_CODES - REL_BUCKETS - 1), F32)], axis=1)
    table3 = jnp.concatenate(_split3(by_code), axis=0)
    bcmp = _bias_table(table3, jnp.asarray(cmp_code)).reshape(NSA_HEADS, S // TQC, N_CMP_PAD, TQC)
    battn = _bias_table(table3, jnp.asarray(attn_code)).reshape(NSA_HEADS, -1, TKA, TQA)
    o_cmp, q_aug = _nsa_cmp_select(q, gates, kcmp, vcmpt, bcmp, jnp.asarray(overlap, BF16))
    o = _nsa_attn(q_aug, gates, k_sw, vt_sw, o_cmp, battn)

    def mlp_weights(li):
        return mlp_w_up[li].astype(BF16), mlp_w_down[li].astype(BF16)

    wup, wdn = mlp_weights(0)
    x2 = _mix_out_mlp(x.reshape(T, D), o.reshape(T, NSA_Q_DIM), nsa_w_out[0].astype(BF16),
                      norm_mlp_w[0][None, :], wup, wdn, tm=512)

    w_in = ssd_w_in[0]
    def col_chunks(w):
        return w.reshape(D, -1, SSD_NCHUNK).transpose(1, 0, 2).astype(BF16)

    wz = col_chunks(w_in[:, :SSD_D_INNER])
    wx = col_chunks(w_in[:, SSD_D_INNER:SSD_D_INNER + SSD_CONV_DIM])
    lane_pad = LANES - SSD_HEADS
    wdt = jnp.pad(w_in[:, SSD_D_INNER + SSD_CONV_DIM:], ((0, 0), (0, lane_pad))).astype(BF16)
    pad_heads = lambda v: jnp.pad(v, (0, lane_pad))[None, :]
    z, xbc, dt = _ssd_proj(x2.reshape(B, S, D), norm_mix_w[1][None, :], wz, wx, wdt,
                           ssd_conv_w[0], ssd_conv_b[0][None, :], pad_heads(ssd_dt_bias[0]), tm=512)
    tril = jnp.asarray(np.tril(np.ones((SSD_CHUNK, SSD_CHUNK), np.float32)), BF16)
    dskip = jnp.repeat(ssd_d[0], SSD_HEAD_DIM)[None, :]
    y = _ssd_scan(xbc, dt, z, pad_heads(ssd_a_log[0]), dskip, ssd_norm_w[0][None, :], tril,
                  jnp.asarray(head_expand, BF16))

    wup, wdn = mlp_weights(1)
    x4 = _mix_out_mlp(x2, y.reshape(T, SSD_D_INNER), ssd_w_out[0].astype(BF16),
                      norm_mlp_w[1][None, :], wup, wdn, tm=512)
    return x4.reshape(B, S, D)
```

```python
import functools
import math

import numpy as np
import jax
import jax.numpy as jnp
from jax import lax
from jax.experimental import pallas as pl
from jax.experimental.pallas import tpu as pltpu

F32 = jnp.float32
BF16 = jnp.bfloat16

D_MODEL = 1024
EPS = 1e-6

NSA_HEADS = 16
NSA_GROUPS = 4
NSA_REP = NSA_HEADS // NSA_GROUPS
HEAD_DIM = 64
NSA_Q_DIM = NSA_HEADS * HEAD_DIM
NSA_KV_DIM = NSA_GROUPS * HEAD_DIM
CMP_BLOCK = 32
CMP_STRIDE = 16
CMP_HIDDEN = 256
SEL_BLOCK = 64
SEL_TOP_N = 16
WINDOW = 512
REL_BUCKETS = 32
REL_MAX_DIST = 128

SSD_D_INNER = 2 * D_MODEL
SSD_HEAD_DIM = 64
SSD_HEADS = SSD_D_INNER // SSD_HEAD_DIM
SSD_GROUPS = 8
SSD_STATE = 128
SSD_CONV = 4
SSD_CHUNK = 128
SSD_CONV_DIM = SSD_D_INNER + 2 * SSD_GROUPS * SSD_STATE
MLP_HIDDEN = 4 * D_MODEL

LANES = 128
SUBLANES = 8
VMEM_LIMIT_BYTES = 56 * 1024 * 1024

VT_BLOCK = 128
TQC = 1024
TQA = 256
TKA = 256
N_CMP_PAD = 128
N_SEL_PAD = 128
MASKED = -1e30
LOG2E = math.log2(math.e)
SCORE_BIG = 3e38
N_SEL_VARIANTS = 3
N_WIN_TILES = WINDOW // TKA + 1
ACC_ROWS = HEAD_DIM + 16

NT_DIMS = (((1,), (1,)), ((), ()))


def _dot(a, b):
    return jnp.dot(a, b, preferred_element_type=F32)


def _dot_nt(a, b):
    return lax.dot_general(a, b, NT_DIMS, preferred_element_type=F32)


def _split3(x):
    hi = x.astype(BF16)
    r1 = x - hi.astype(F32)
    mid = r1.astype(BF16)
    lo = (r1 - mid.astype(F32)).astype(BF16)
    return hi, mid, lo


def _rms_rows(x, w):
    return x * lax.rsqrt(jnp.mean(x * x, axis=-1, keepdims=True) + EPS) * w


def _silu(x):
    h = 0.5 * x
    return h + h * jnp.tanh(h)


def _const_spec(shape):
    nd = len(shape)
    return pl.BlockSpec(shape, lambda *_: (0,) * nd, pipeline_mode=pl.Buffered(1))


def _compiler_params(semantics):
    return pltpu.CompilerParams(dimension_semantics=semantics, vmem_limit_bytes=VMEM_LIMIT_BYTES)


def _nsa_proj_kernel(x_ref, blk_ref, nw_ref, wq_ref, wk_ref, wvt_ref, wgt_ref, qg_ref, kg_ref, seg_ref,
                     q_ref, kc_ref, vc_ref, k_ref, vt_ref, gate_ref):
    tm = x_ref.shape[1]
    h = _rms_rows(x_ref[0], nw_ref[...]).astype(BF16)
    seg = seg_ref[...]

    def head_rms(y, gain):
        sq = y * y
        hi = sq.astype(BF16)
        lo = (sq - hi.astype(F32)).astype(BF16)
        ss = _dot(hi, seg) + _dot(lo, seg)
        return y * lax.rsqrt(ss * (1.0 / HEAD_DIM) + EPS) * gain

    def emit_q(c, y):
        y = (head_rms(y, qg_ref[...]) * (HEAD_DIM ** -0.5 * LOG2E)).astype(BF16)
        for r in range(4):
            q_ref[0, 4 * c + r] = y[:, 64 * r:64 * r + 64]

    def emit_raw(c, y):
        raw_ref = (kc_ref, vc_ref)[c]
        for g in range(NSA_GROUPS):
            raw_ref[0, g] = y[:, 64 * g:64 * g + 64]

    def emit_k(br, y):
        y = head_rms(y, kg_ref[...]).astype(BF16)
        tag = blk_ref[...] if br == 0 else jnp.zeros(blk_ref.shape, BF16)
        for g in range(NSA_GROUPS):
            k_ref[0, g, br, :, 0:HEAD_DIM] = y[:, 64 * g:64 * g + 64]
            k_ref[0, g, br, :, HEAD_DIM:2 * HEAD_DIM] = tag

    def emit_vt(br, yt):
        yt = yt.astype(BF16)
        for g in range(NSA_GROUPS):
            for kb in range(tm // VT_BLOCK):
                vt_ref[0, g, br, kb] = yt[64 * g:64 * g + 64, VT_BLOCK * kb:VT_BLOCK * (kb + 1)]

    def emit_gates(_, yt):
        gate_ref[0] = jax.nn.sigmoid(yt)

    stages = ([(lambda c=c: _dot(h, wq_ref[:, 256 * c:256 * (c + 1)]), emit_q, c) for c in range(NSA_Q_DIM // 256)]
              + [(lambda c=c: _dot(h, wk_ref[c]), emit_raw, c) for c in range(2)]
              + [(lambda br=br: _dot(h, wk_ref[2 + br]), emit_k, br) for br in range(2)]
              + [(lambda br=br: _dot_nt(wvt_ref[br], h), emit_vt, br) for br in range(2)]
              + [(lambda: _dot_nt(wgt_ref[...], h), emit_gates, 0)])
    pending = stages[0][0]()
    for n, (_, emit, arg) in enumerate(stages):
        y = pending
        if n + 1 < len(stages):
            pending = stages[n + 1][0]()
        emit(arg, y)


def _nsa_proj(x, key_block, nw, wq, wk, wvt, wgt, qg, kg, seg, tm):
    B, S, D = x.shape
    G = NSA_GROUPS
    raw_block = pl.BlockSpec((1, G, tm, HEAD_DIM), lambda b, i: (b, 0, i, 0))
    raw_shape = jax.ShapeDtypeStruct((B, G, S, HEAD_DIM), F32)
    consts = (nw, wq, wk, wvt, wgt, qg, kg, seg)
    return pl.pallas_call(
        _nsa_proj_kernel,
        grid=(B, S // tm),
        in_specs=[pl.BlockSpec((1, tm, D), lambda b, i: (b, i, 0)),
                  pl.BlockSpec((tm, HEAD_DIM), lambda b, i: (i, 0))] + [_const_spec(c.shape) for c in consts],
        out_specs=[
            pl.BlockSpec((1, NSA_HEADS, tm, HEAD_DIM), lambda b, i: (b, 0, i, 0)),
            raw_block, raw_block,
            pl.BlockSpec((1, G, 2, tm, 2 * HEAD_DIM), lambda b, i: (b, 0, 0, i, 0)),
            pl.BlockSpec((1, G, 2, tm // VT_BLOCK, HEAD_DIM, VT_BLOCK), lambda b, i: (b, 0, 0, i, 0, 0)),
            pl.BlockSpec((1, G * 16, tm), lambda b, i: (b, 0, i)),
        ],
        out_shape=[
            jax.ShapeDtypeStruct((B, NSA_HEADS, S, HEAD_DIM), BF16),
            raw_shape, raw_shape,
            jax.ShapeDtypeStruct((B, G, 2, S, 2 * HEAD_DIM), BF16),
            jax.ShapeDtypeStruct((B, G, 2, S // VT_BLOCK, HEAD_DIM, VT_BLOCK), BF16),
            jax.ShapeDtypeStruct((B, G * 16, S), F32),
        ],
        compiler_params=_compiler_params(("parallel", "parallel")),
        name="nsa_proj",
    )(x, key_block, *consts)


def _nsa_compress_kernel(kc_ref, vc_ref, pek_ref, pev_ref, w1k_ref, w1v_ref, w2k_ref, w2vt_ref, kg_ref,
                         kcmp_ref, vcmp_ref):
    half = CMP_BLOCK // 2

    def hidden(src_ref, pe_ref, w1_ref):
        acc_a = jnp.zeros((N_CMP_PAD, CMP_HIDDEN), F32)
        acc_b = jnp.zeros((N_CMP_PAD, CMP_HIDDEN), F32)
        for l in range(half):
            rows = src_ref[0, 0, pl.ds(l, N_CMP_PAD, stride=CMP_STRIDE), :]
            acc_a += _dot((rows + pe_ref[l:l + 1, :]).astype(BF16), w1_ref[l])
            acc_b += _dot((rows + pe_ref[half + l:half + l + 1, :]).astype(BF16), w1_ref[half + l])
        hid = acc_a + pltpu.roll(acc_b, N_CMP_PAD - 1, axis=0)
        return _silu(hid).astype(BF16)

    k = _dot(hidden(kc_ref, pek_ref, w1k_ref), w2k_ref[...])
    kcmp_ref[0, 0] = _rms_rows(k, kg_ref[...]).astype(BF16)
    vcmp_ref[0, 0] = _dot_nt(w2vt_ref[...], hidden(vc_ref, pev_ref, w1v_ref)).astype(BF16)


def _nsa_compress(kc, vc, pek, pev, w1k, w1v, w2k, w2vt, kg):
    B, G, S, Dh = kc.shape
    src = pl.BlockSpec((1, 1, S, Dh), lambda b, g: (b, g, 0, 0))
    consts = (pek, pev, w1k, w1v, w2k, w2vt, kg)
    return pl.pallas_call(
        _nsa_compress_kernel,
        grid=(B, G),
        in_specs=[src, src] + [_const_spec(c.shape) for c in consts],
        out_specs=[pl.BlockSpec((1, 1, N_CMP_PAD, Dh), lambda b, g: (b, g, 0, 0)),
                   pl.BlockSpec((1, 1, Dh, N_CMP_PAD), lambda b, g: (b, g, 0, 0))],
        out_shape=[jax.ShapeDtypeStruct((B, G, N_CMP_PAD, Dh), BF16),
                   jax.ShapeDtypeStruct((B, G, Dh, N_CMP_PAD), BF16)],
        compiler_params=_compiler_params(("parallel", "parallel")),
        name="nsa_compress",
    )(kc, vc, *consts)


BIAS_CHUNK = 8192
BIAS_CODES = 64


def _bias_table_kernel(t3_ref, code_ref, out_ref):
    code = code_ref[...]
    ids = lax.broadcasted_iota(jnp.int32, (BIAS_CODES, code.shape[1]), 0)
    onehot = jnp.where(ids == code, 1.0, 0.0).astype(BF16)
    y = _dot(t3_ref[...], onehot)
    out_ref[...] = y[0:NSA_HEADS] + y[NSA_HEADS:2 * NSA_HEADS] + y[2 * NSA_HEADS:3 * NSA_HEADS]


def _bias_table(table3, code):
    n = code.shape[1]
    return pl.pallas_call(
        _bias_table_kernel,
        grid=(n // BIAS_CHUNK,),
        in_specs=[_const_spec(table3.shape), pl.BlockSpec((1, BIAS_CHUNK), lambda i: (0, i))],
        out_specs=pl.BlockSpec((NSA_HEADS, BIAS_CHUNK), lambda i: (0, i)),
        out_shape=jax.ShapeDtypeStruct((NSA_HEADS, n), F32),
        compiler_params=_compiler_params(("parallel",)),
        name="rel_bias_table",
    )(table3, code)


def _nsa_cmp_select_kernel(q_ref, gate_ref, kcmp_ref, vcmpt_ref, bias_ref, ovl_ref, o_ref, qa_ref):
    i = pl.program_id(2)
    kcmp = kcmp_ref[0, 0]
    vcmpt = vcmpt_ref[0, 0]

    scores = [_dot_nt(kcmp, q_ref[0, r]) for r in range(NSA_REP)]
    probs = []
    for r in range(NSA_REP):
        s = scores[r] + bias_ref[r, 0]
        m = jnp.max(s, axis=0, keepdims=True)
        m = jnp.where(m > 0.5 * MASKED, m, 0.0)
        e = jnp.exp2(s - m)
        den = jnp.sum(e, axis=0, keepdims=True)
        probs.append(e / jnp.where(den > 0.0, den, 1.0))
    outs = [_dot(vcmpt, p.astype(BF16)) for p in probs]

    ph, pm, plo = _split3(probs[0] + probs[1] + probs[2] + probs[3])
    ovl = ovl_ref[...]
    n_sel = 32
    imp = (_dot(ovl, ph) + _dot(ovl, pm) + _dot(ovl, plo))[0:n_sel]
    tok = i * TQC + lax.broadcasted_iota(jnp.int32, (n_sel, TQC), 1)
    blk = lax.broadcasted_iota(jnp.int32, (n_sel, TQC), 0)
    cur = tok // SEL_BLOCK
    forced = (blk == 0) | (blk == cur) | (blk == cur - 1)
    valid = blk * SEL_BLOCK <= tok
    score = jnp.where(forced, SCORE_BIG, imp)
    score = jnp.where(valid, score, -SCORE_BIG)
    groups = [slice(SUBLANES * v, SUBLANES * (v + 1)) for v in range(n_sel // SUBLANES)]
    ranks = [jnp.zeros((SUBLANES, TQC), F32) for _ in groups]
    for jp in range(n_sel):
        other = score[jp:jp + 1, :]
        for v, rows in enumerate(groups):
            mine = score[rows]
            if rows.start > jp:
                beats = other >= mine
            elif rows.stop - 1 <= jp:
                beats = other > mine
            else:
                later = lax.broadcasted_iota(jnp.int32, (SUBLANES, TQC), 0) > jp - rows.start
                beats = (other > mine) | ((other == mine) & later)
            ranks[v] = ranks[v] + jnp.where(beats, 1.0, 0.0)
    rank = jnp.concatenate(ranks, axis=0)
    blk_mask = jnp.where(rank < SEL_TOP_N, 0.0, MASKED)
    blk_mask = jnp.concatenate([blk_mask, jnp.zeros((HEAD_DIM - n_sel, TQC), F32)], axis=0).T.astype(BF16)
    for r in range(NSA_REP):
        qa_ref[0, r, :, 0:HEAD_DIM] = q_ref[0, r]
        qa_ref[0, r, :, HEAD_DIM:2 * HEAD_DIM] = blk_mask

    gate = gate_ref[0]
    for r in range(NSA_REP):
        o_ref[0, :, r * HEAD_DIM:(r + 1) * HEAD_DIM] = (outs[r] * gate[r:r + 1]).T


def _nsa_cmp_select(q, gates, kcmp, vcmpt, bcmp, ovl):
    B, H, S, Dh = q.shape
    G = NSA_GROUPS
    return pl.pallas_call(
        _nsa_cmp_select_kernel,
        grid=(G, B, S // TQC),
        in_specs=[
            pl.BlockSpec((1, NSA_REP, TQC, Dh), lambda g, b, i: (b, g, i, 0)),
            pl.BlockSpec((1, 16, TQC), lambda g, b, i: (b, g, i)),
            pl.BlockSpec((1, 1, N_CMP_PAD, Dh), lambda g, b, i: (b, g, 0, 0)),
            pl.BlockSpec((1, 1, Dh, N_CMP_PAD), lambda g, b, i: (b, g, 0, 0)),
            pl.BlockSpec((NSA_REP, 1, N_CMP_PAD, TQC), lambda g, b, i: (g, i, 0, 0)),
            _const_spec(ovl.shape),
        ],
        out_specs=[
            pl.BlockSpec((1, TQC, NSA_REP * Dh), lambda g, b, i: (b, i, g)),
            pl.BlockSpec((1, NSA_REP, TQC, 2 * Dh), lambda g, b, i: (b, g, i, 0)),
        ],
        out_shape=[
            jax.ShapeDtypeStruct((B, S, H * Dh), F32),
            jax.ShapeDtypeStruct((B, H, S, 2 * Dh), BF16),
        ],
        compiler_params=_compiler_params(("parallel", "parallel", "arbitrary")),
        name="nsa_cmp_select",
    )(q, gates, kcmp, vcmpt, bcmp, ovl)


def _nsa_attn_kernel(q_ref, gate_ref, k_ref, vt_ref, ocmp_ref, bias_ref,
                     o_ref, s0_ref, s1_ref, x0_ref, x1_ref, p0_ref, p1_ref, a0_ref, a1_ref,
                     m_ref, acc_ref):
    i = pl.program_id(2)
    last_step = pl.num_programs(2) - 1
    s_bufs, x_bufs = (s0_ref, s1_ref), (x0_ref, x1_ref)
    p_bufs, a_bufs = (p0_ref, p1_ref), (a0_ref, a1_ref)
    masked_variant = N_SEL_VARIANTS + N_WIN_TILES

    def n_tasks_of(step):
        n_sel = step + 1
        return n_sel, n_sel + jnp.minimum(N_WIN_TILES, n_sel)

    def task(t, step):
        n_sel, n_tasks = n_tasks_of(step)
        is_win = t >= n_sel
        c = jnp.clip(jnp.where(is_win, step - (t - n_sel), t), 0, step)
        behind = step - c
        variant = jnp.where(is_win, N_SEL_VARIANTS + behind, jnp.minimum(behind, N_SEL_VARIANTS - 1))
        variant = jnp.where(t < n_tasks, variant, masked_variant)
        return is_win.astype(jnp.int32), c, variant

    def score_pair(u, step):
        pair = [task(2 * u, step), task(2 * u + 1, step)]
        q0 = pl.multiple_of(step * TQA, TQA)
        k = jnp.concatenate([k_ref[0, 0, br, pl.ds(pl.multiple_of(c * TKA, TKA), TKA), :] for br, c, _ in pair],
                            axis=0)
        for r in range(NSA_REP):
            s2 = _dot_nt(k, q_ref[0, r, pl.ds(q0, TQA), :])
            for slot, (_, _, variant) in enumerate(pair):
                s = s2[slot * TKA:(slot + 1) * TKA] + bias_ref[r, variant]
                s_bufs[slot][r] = s
                x_bufs[slot][r] = jnp.max(s, axis=0, keepdims=True)

    def softmax_stage(t, slot):
        br, _, _ = task(t, i)
        for r in range(NSA_REP):
            m_old = m_ref[br, r]
            m_new = jnp.maximum(m_old, x_bufs[slot][r])
            m_ref[br, r] = m_new
            a_bufs[slot][r] = jnp.exp2(m_old - m_new)
            p_bufs[slot][r] = jnp.exp2(s_bufs[slot][r] - m_new).astype(BF16)

    ones_rows = jnp.where(lax.broadcasted_iota(jnp.int32, (ACC_ROWS - HEAD_DIM, TKA), 0) == 0, 1.0, 0.0).astype(BF16)

    def value_stage(t, slot):
        br, c, _ = task(t, i)
        blocks = TKA // VT_BLOCK
        vt = jnp.concatenate([vt_ref[0, 0, br, blocks * c + u] for u in range(blocks)], axis=1)
        vt = jnp.concatenate([vt, ones_rows], axis=0)
        for r in range(NSA_REP):
            acc_ref[br, r] = a_bufs[slot][r] * acc_ref[br, r] + _dot(vt, p_bufs[slot][r])

    m_ref[...] = jnp.full(m_ref.shape, MASKED, F32)
    acc_ref[...] = jnp.zeros(acc_ref.shape, F32)

    @pl.when(i == 0)
    def _():
        score_pair(0, i)

    n_trips = (n_tasks_of(i)[1] + 1) // 2

    def softmax_and_next_scores(u):
        softmax_stage(2 * u, 0)
        softmax_stage(2 * u + 1, 1)
        in_tile = u + 1 < n_trips
        score_pair(jnp.where(in_tile, u + 1, 0), jnp.where(in_tile, i, jnp.minimum(i + 1, last_step)))

    def trip(u, carry):
        value_stage(2 * u - 2, 0)
        value_stage(2 * u - 1, 1)
        softmax_and_next_scores(u)
        return carry

    softmax_and_next_scores(0)
    lax.fori_loop(1, n_trips, trip, 0)
    value_stage(2 * n_trips - 2, 0)
    value_stage(2 * n_trips - 1, 1)

    gate = gate_ref[0]
    for r in range(NSA_REP):
        w_sel = gate[4 + r:5 + r] / acc_ref[0, r, HEAD_DIM:HEAD_DIM + 1, :]
        w_win = gate[8 + r:9 + r] / acc_ref[1, r, HEAD_DIM:HEAD_DIM + 1, :]
        o_t = w_sel * acc_ref[0, r, 0:HEAD_DIM, :] + w_win * acc_ref[1, r, 0:HEAD_DIM, :]
        cols = slice(r * HEAD_DIM, (r + 1) * HEAD_DIM)
        o_ref[0, :, cols] = (ocmp_ref[0, :, cols] + o_t.T).astype(BF16)


def _nsa_attn(q_aug, gates, k_sw, vt_sw, o_cmp, bias):
    B, H, S, Dq = q_aug.shape
    Dh = HEAD_DIM
    G = NSA_GROUPS
    n_vt = S // VT_BLOCK
    n_var = bias.shape[1]
    return pl.pallas_call(
        _nsa_attn_kernel,
        grid=(G, B, S // TQA),
        in_specs=[
            pl.BlockSpec((1, NSA_REP, S, Dq), lambda g, b, i: (b, g, 0, 0)),
            pl.BlockSpec((1, 16, TQA), lambda g, b, i: (b, g, i)),
            pl.BlockSpec((1, 1, 2, S, Dq), lambda g, b, i: (b, g, 0, 0, 0)),
            pl.BlockSpec((1, 1, 2, n_vt, Dh, VT_BLOCK), lambda g, b, i: (b, g, 0, 0, 0, 0)),
            pl.BlockSpec((1, TQA, NSA_REP * Dh), lambda g, b, i: (b, i, g)),
            pl.BlockSpec((NSA_REP, n_var, TKA, TQA), lambda g, b, i: (g, 0, 0, 0), pipeline_mode=pl.Buffered(1)),
        ],
        out_specs=pl.BlockSpec((1, TQA, NSA_REP * Dh), lambda g, b, i: (b, i, g)),
        out_shape=jax.ShapeDtypeStruct((B, S, H * Dh), BF16),
        scratch_shapes=[
            pltpu.VMEM((NSA_REP, TKA, TQA), F32),
            pltpu.VMEM((NSA_REP, TKA, TQA), F32),
            pltpu.VMEM((NSA_REP, 1, TQA), F32),
            pltpu.VMEM((NSA_REP, 1, TQA), F32),
            pltpu.VMEM((NSA_REP, TKA, TQA), BF16),
            pltpu.VMEM((NSA_REP, TKA, TQA), BF16),
            pltpu.VMEM((NSA_REP, 1, TQA), F32),
            pltpu.VMEM((NSA_REP, 1, TQA), F32),
            pltpu.VMEM((2, NSA_REP, 1, TQA), F32),
            pltpu.VMEM((2, NSA_REP, ACC_ROWS, TQA), F32),
        ],
        compiler_params=_compiler_params(("parallel", "parallel", "arbitrary")),
        name="nsa_attn",
    )(q_aug, gates, k_sw, vt_sw, o_cmp, bias)


MLP_CHUNK = 1024


def _mix_out_mlp_kernel(x_ref, a_ref, wo_ref, nw_ref, wup_ref, wdn_ref, o_ref):
    x1 = x_ref[...] + _dot(a_ref[...], wo_ref[...])
    h = _rms_rows(x1, nw_ref[...]).astype(BF16)

    acc = None
    for c in range(wup_ref.shape[1] // MLP_CHUNK):
        hid = slice(c * MLP_CHUNK, (c + 1) * MLP_CHUNK)
        u = jnp.maximum(_dot(h, wup_ref[:, hid]), 0.0)
        d = _dot((u * u).astype(BF16), wdn_ref[hid, :])
        acc = d if acc is None else acc + d
    o_ref[...] = x1 + acc


def _mix_out_mlp(x, a, wo, nw, wup, wdn, tm):
    T, D = x.shape
    K = a.shape[1]
    return pl.pallas_call(
        _mix_out_mlp_kernel,
        grid=(T // tm,),
        in_specs=[
            pl.BlockSpec((tm, D), lambda i: (i, 0)),
            pl.BlockSpec((tm, K), lambda i: (i, 0)),
            _const_spec(wo.shape), _const_spec(nw.shape), _const_spec(wup.shape), _const_spec(wdn.shape),
        ],
        out_specs=pl.BlockSpec((tm, D), lambda i: (i, 0)),
        out_shape=jax.ShapeDtypeStruct((T, D), F32),
        compiler_params=_compiler_params(("parallel",)),
        name="mix_out_mlp",
    )(x, a, wo, nw, wup, wdn)


SSD_NCHUNK = 512


def _ssd_proj_kernel(x_ref, nw_ref, wz_ref, wx_ref, wdt_ref, cw_ref, cb_ref, dtb_ref,
                     z_ref, xbc_ref, dt_ref, buf_ref):
    i = pl.program_id(1)
    tm = x_ref.shape[1]
    @pl.when(i == 0)
    def _():
        buf_ref[0:SUBLANES, :] = jnp.zeros((SUBLANES, buf_ref.shape[1]), F32)

    @pl.when(i > 0)
    def _():
        buf_ref[0:SUBLANES, :] = buf_ref[tm:tm + SUBLANES, :]

    h = _rms_rows(x_ref[0], nw_ref[...]).astype(BF16)

    for c in range(wz_ref.shape[0]):
        z_ref[0, :, c * SSD_NCHUNK:(c + 1) * SSD_NCHUNK] = _dot(h, wz_ref[c])

    for c in range(wx_ref.shape[0]):
        cols = slice(c * SSD_NCHUNK, (c + 1) * SSD_NCHUNK)
        buf_ref[SUBLANES:SUBLANES + tm, cols] = _dot(h, wx_ref[c])
        acc = cb_ref[:, cols] + buf_ref[SUBLANES:SUBLANES + tm, cols] * cw_ref[SSD_CONV - 1:SSD_CONV, cols]
        for k in range(SSD_CONV - 1):
            off = SUBLANES - (SSD_CONV - 1) + k
            acc += buf_ref[off:off + tm, cols] * cw_ref[k:k + 1, cols]
        xbc_ref[0, :, cols] = _silu(acc)

    dt_ref[0] = jax.nn.softplus(_dot(h, wdt_ref[...]) + dtb_ref[...])


def _ssd_proj(x, nw, wz, wx, wdt, cw, cb, dtb, tm):
    B, S, D = x.shape
    consts = (nw, wz, wx, wdt, cw, cb, dtb)
    return pl.pallas_call(
        _ssd_proj_kernel,
        grid=(B, S // tm),
        in_specs=[pl.BlockSpec((1, tm, D), lambda b, i: (b, i, 0))] + [_const_spec(c.shape) for c in consts],
        out_specs=[
            pl.BlockSpec((1, tm, SSD_D_INNER), lambda b, i: (b, i, 0)),
            pl.BlockSpec((1, tm, SSD_CONV_DIM), lambda b, i: (b, i, 0)),
            pl.BlockSpec((1, tm, LANES), lambda b, i: (b, i, 0)),
        ],
        out_shape=[
            jax.ShapeDtypeStruct((B, S, SSD_D_INNER), F32),
            jax.ShapeDtypeStruct((B, S, SSD_CONV_DIM), F32),
            jax.ShapeDtypeStruct((B, S, LANES), F32),
        ],
        scratch_shapes=[pltpu.VMEM((tm + SUBLANES, SSD_CONV_DIM), F32)],
        compiler_params=_compiler_params(("parallel", "arbitrary")),
        name="ssd_proj",
    )(x, *consts)


def _ssd_scan_kernel(xbc_ref, dt_ref, z_ref, alog_ref, dskip_ref, nw_ref, tril_ref, hexp_ref,
                     y_ref, state_ref, acs_ref, acst_ref, dtt_ref, e3_ref, wt_ref):
    L = SSD_CHUNK
    P = SSD_HEAD_DIM
    N = SSD_STATE
    hpg = SSD_HEADS // SSD_GROUPS
    gw = hpg * P
    chunk = pl.program_id(1)

    def chunk_decays(ci):
        dt = dt_ref[0, pl.ds(pl.multiple_of(ci * L, L), L), :]
        a = dt * (-jnp.exp(alog_ref[...]))
        tril = tril_ref[...]
        ah, am, al = _split3(a)
        acs = _dot(tril, ah) + _dot(tril, am) + _dot(tril, al)
        acs_t = acs.T
        dt_t = dt.T
        acs_ref[...] = acs
        acst_ref[...] = acs_t
        dtt_ref[...] = dt_t
        for n, term in enumerate(_split3(jnp.exp(acs))):
            e3_ref[n] = term
        wt_ref[...] = jnp.exp(acs_t[:, L - 1:L] - acs_t) * dt_t

    @pl.when(chunk == 0)
    def _():
        state_ref[...] = jnp.zeros(state_ref.shape, F32)
        chunk_decays(0)

    acs, acs_t, dt_t, w_t = acs_ref[...], acst_ref[...], dtt_ref[...], wt_ref[...]
    eh, em, el = e3_ref[0], e3_ref[1], e3_ref[2]
    causal = (lax.broadcasted_iota(jnp.int32, (L, L), 0) >= lax.broadcasted_iota(jnp.int32, (L, L), 1))
    head_of_lane = lax.broadcasted_iota(jnp.int32, (L, gw), 1) // P

    for g in range(SSD_GROUPS):
        b_off = SSD_D_INNER + g * N
        c_off = SSD_D_INNER + SSD_GROUPS * N + g * N
        bm = xbc_ref[0, :, b_off:b_off + N]
        cm = xbc_ref[0, :, c_off:c_off + N].astype(BF16)
        cb = _dot_nt(cm, bm.astype(BF16))
        bm_t = bm.T
        xg = xbc_ref[0, :, g * gw:(g + 1) * gw]
        hexp = hexp_ref[g]
        e_exp = _dot(eh, hexp) + _dot(em, hexp) + _dot(el, hexp)
        st = state_ref[g]
        y = _dot(cm, st.astype(BF16)) * e_exp
        st = st * e_exp[L - 1:L, :]
        for j in range(hpg):
            hd = g * hpg + j
            xj = jnp.where(head_of_lane == j, xg, 0.0).astype(BF16)
            diff = acs[:, hd:hd + 1] - acs_t[hd:hd + 1, :]
            decay = jnp.exp(jnp.where(causal, diff, -jnp.inf))
            y = y + _dot((cb * decay * dt_t[hd:hd + 1, :]).astype(BF16), xj)
            st = st + _dot((bm_t * w_t[hd:hd + 1, :]).astype(BF16), xj)
        state_ref[g] = st
        cols = slice(g * gw, (g + 1) * gw)
        zg = z_ref[0, :, cols]
        y = (y + xg * dskip_ref[:, cols]) * _silu(zg)
        y_ref[0, :, cols] = _rms_rows(y, nw_ref[:, cols]).astype(BF16)

    chunk_decays(jnp.minimum(chunk + 1, pl.num_programs(1) - 1))


def _ssd_scan(xbc, dt, z, alog, dskip, nw, tril, hexp):
    B, S, _ = xbc.shape
    L = SSD_CHUNK
    consts = (alog, dskip, nw, tril, hexp)
    return pl.pallas_call(
        _ssd_scan_kernel,
        grid=(B, S // L),
        in_specs=[
            pl.BlockSpec((1, L, SSD_CONV_DIM), lambda b, c: (b, c, 0)),
            pl.BlockSpec((1, S, LANES), lambda b, c: (b, 0, 0)),
            pl.BlockSpec((1, L, SSD_D_INNER), lambda b, c: (b, c, 0)),
        ] + [_const_spec(c.shape) for c in consts],
        out_specs=pl.BlockSpec((1, L, SSD_D_INNER), lambda b, c: (b, c, 0)),
        out_shape=jax.ShapeDtypeStruct((B, S, SSD_D_INNER), BF16),
        scratch_shapes=[
            pltpu.VMEM((SSD_GROUPS, SSD_STATE, SSD_D_INNER // SSD_GROUPS), F32),
            pltpu.VMEM((L, LANES), F32),
            pltpu.VMEM((LANES, L), F32),
            pltpu.VMEM((LANES, L), F32),
            pltpu.VMEM((3, L, LANES), BF16),
            pltpu.VMEM((LANES, L), F32),
        ],
        compiler_params=_compiler_params(("parallel", "arbitrary")),
        name="ssd_scan",
    )(xbc, dt, z, *consts)


def _rel_bucket_np(dist):
    max_exact = REL_BUCKETS // 2
    d = np.maximum(dist, 0)
    df = np.maximum(d, 1).astype(np.float32)
    large = max_exact + (np.log(df / np.float32(max_exact)) / np.float32(math.log(REL_MAX_DIST / max_exact))
                         * np.float32(REL_BUCKETS - max_exact)).astype(np.int32)
    large = np.minimum(large, REL_BUCKETS - 1)
    return np.where(d < max_exact, d, large).astype(np.int32)


@functools.lru_cache(maxsize=None)
def _position_tables(S):
    n_cmp = (S - CMP_BLOCK) // CMP_STRIDE + 1
    n_sel = S // SEL_BLOCK

    masked_code = REL_BUCKETS

    def code(dist, ok):
        return np.where(ok, _rel_bucket_np(dist), masked_code).astype(np.int32).reshape(1, -1)

    n = np.arange(N_CMP_PAD)[None, :, None]
    t = np.arange(S // TQC)[:, None, None] * TQC + np.arange(TQC)[None, None, :]
    dist = t - (n * CMP_STRIDE + CMP_BLOCK - 1)
    cmp_code = code(dist, (dist >= 0) & (n < n_cmp))
    assert TQA == TKA and (N_SEL_VARIANTS - 1) * TQA - (TKA - 1) >= REL_MAX_DIST
    ti = np.arange(TQA)[None, :]
    j = np.arange(TKA)[:, None]
    tile_dist = [TQA * behind + ti - j for behind in range(N_WIN_TILES)]
    far = np.full((TKA, TQA), 2 * REL_MAX_DIST)
    sel_dist = np.stack(tile_dist[:N_SEL_VARIANTS - 1] + [far])
    win_dist = np.stack(tile_dist)
    attn_code = np.concatenate([code(sel_dist, sel_dist >= 0),
                                code(win_dist, (win_dist >= 0) & (win_dist < WINDOW)),
                                np.full((1, TKA * TQA), masked_code, np.int32)],
                               axis=1)
    jb = np.arange(N_SEL_PAD)[:, None]
    nn = np.arange(N_CMP_PAD)[None, :]
    overlap = ((nn * CMP_STRIDE < jb * SEL_BLOCK + SEL_BLOCK) & (nn * CMP_STRIDE + CMP_BLOCK > jb * SEL_BLOCK)
               & (nn < n_cmp) & (jb < n_sel)).astype(np.float32)
    key_block = (np.arange(S)[:, None] // SEL_BLOCK == np.arange(HEAD_DIM)[None, :]).astype(np.float32)
    hpg = SSD_HEADS // SSD_GROUPS
    lane_head = np.arange(hpg * SSD_HEAD_DIM)[None, None, :] // SSD_HEAD_DIM
    head_expand = (np.arange(LANES)[None, :, None]
                   == np.arange(SSD_GROUPS)[:, None, None] * hpg + lane_head).astype(np.float32)
    return cmp_code, attn_code, overlap, key_block, head_expand


def kernel(x, norm_mix_w, norm_mlp_w, rel_table, nsa_w_in, nsa_q_gain, nsa_k_gain, cmp_pe_k, cmp_w1_k, cmp_w2_k, cmp_pe_v, cmp_w1_v, cmp_w2_v, nsa_w_out, ssd_w_in, ssd_conv_w, ssd_conv_b, ssd_dt_bias, ssd_a_log, ssd_d, ssd_norm_w, ssd_w_out, mlp_w_up, mlp_w_down):
    B, S, D = x.shape
    T = B * S
    G, R = NSA_GROUPS, NSA_REP
    cmp_code, attn_code, overlap, key_block, head_expand = _position_tables(S)

    w_in = nsa_w_in[0].astype(BF16)
    kv0 = NSA_Q_DIM
    kv_cols = lambda c: w_in[:, kv0 + c * NSA_KV_DIM:kv0 + (c + 1) * NSA_KV_DIM]
    wq = w_in[:, :NSA_Q_DIM]
    wk = jnp.stack([kv_cols(0), kv_cols(1), kv_cols(2), kv_cols(4)])
    wvt = jnp.stack([kv_cols(3).T, kv_cols(5).T])
    wg = w_in[:, kv0 + 6 * NSA_KV_DIM:].reshape(D, 3, G, R).transpose(0, 2, 1, 3).reshape(D, G, 3 * R)
    wgt = jnp.pad(wg, ((0, 0), (0, 0), (0, 16 - 3 * R))).reshape(D, G * 16).T
    qg = jnp.tile(nsa_q_gain[0], 4)[None, :]
    kg = jnp.tile(nsa_k_gain[0], 4)[None, :]
    seg = jnp.asarray(np.kron(np.eye(4, dtype=np.float32), np.ones((HEAD_DIM, HEAD_DIM), np.float32)), BF16)
    q, kc, vc, k_sw, vt_sw, gates = _nsa_proj(
        x, jnp.asarray(key_block, BF16), norm_mix_w[0][None, :], wq, wk, wvt, wgt, qg, kg, seg, tm=512)

    kcmp, vcmpt = _nsa_compress(
        kc, vc, cmp_pe_k[0], cmp_pe_v[0], cmp_w1_k[0].astype(BF16), cmp_w1_v[0].astype(BF16),
        cmp_w2_k[0].astype(BF16), cmp_w2_v[0].astype(BF16).T, nsa_k_gain[0][None, :])

    by_code = jnp.concatenate([rel_table.T * LOG2E, jnp.full((NSA_HEADS, 1), MASKED, F32),
                               jnp.zeros((NSA_HEADS, BIAS_CODES - REL_BUCKETS - 1), F32)], axis=1)
    table3 = jnp.concatenate(_split3(by_code), axis=0)
    bcmp = _bias_table(table3, jnp.asarray(cmp_code)).reshape(NSA_HEADS, S // TQC, N_CMP_PAD, TQC)
    battn = _bias_table(table3, jnp.asarray(attn_code)).reshape(NSA_HEADS, -1, TKA, TQA)
    o_cmp, q_aug = _nsa_cmp_select(q, gates, kcmp, vcmpt, bcmp, jnp.asarray(overlap, BF16))
    o = _nsa_attn(q_aug, gates, k_sw, vt_sw, o_cmp, battn)

    def mlp_weights(li):
        return mlp_w_up[li].astype(BF16), mlp_w_down[li].astype(BF16)

    wup, wdn = mlp_weights(0)
    x2 = _mix_out_mlp(x.reshape(T, D), o.reshape(T, NSA_Q_DIM), nsa_w_out[0].astype(BF16),
                      norm_mlp_w[0][None, :], wup, wdn, tm=512)

    w_in = ssd_w_in[0]
    def col_chunks(w):
        return w.reshape(D, -1, SSD_NCHUNK).transpose(1, 0, 2).astype(BF16)

    wz = col_chunks(w_in[:, :SSD_D_INNER])
    wx = col_chunks(w_in[:, SSD_D_INNER:SSD_D_INNER + SSD_CONV_DIM])
    lane_pad = LANES - SSD_HEADS
    wdt = jnp.pad(w_in[:, SSD_D_INNER + SSD_CONV_DIM:], ((0, 0), (0, lane_pad))).astype(BF16)
    pad_heads = lambda v: jnp.pad(v, (0, lane_pad))[None, :]
    z, xbc, dt = _ssd_proj(x2.reshape(B, S, D), norm_mix_w[1][None, :], wz, wx, wdt,
                           ssd_conv_w[0], ssd_conv_b[0][None, :], pad_heads(ssd_dt_bias[0]), tm=512)
    tril = jnp.asarray(np.tril(np.ones((SSD_CHUNK, SSD_CHUNK), np.float32)), BF16)
    dskip = jnp.repeat(ssd_d[0], SSD_HEAD_DIM)[None, :]
    y = _ssd_scan(xbc, dt, z, pad_heads(ssd_a_log[0]), dskip, ssd_norm_w[0][None, :], tril,
                  jnp.asarray(head_expand, BF16))

    wup, wdn = mlp_weights(1)
    x4 = _mix_out_mlp(x2, y.reshape(T, SSD_D_INNER), ssd_w_out[0].astype(BF16),
                      norm_mlp_w[1][None, :], wup, wdn, tm=512)
    return x4.reshape(B, S, D)
```

```python
import functools
import math

import numpy as np
import jax
import jax.numpy as jnp
from jax import lax
from jax.experimental import pallas as pl
from jax.experimental.pallas import tpu as pltpu

F32 = jnp.float32
BF16 = jnp.bfloat16

D_MODEL = 1024
EPS = 1e-6

NSA_HEADS = 16
NSA_GROUPS = 4
NSA_REP = NSA_HEADS // NSA_GROUPS
HEAD_DIM = 64
NSA_Q_DIM = NSA_HEADS * HEAD_DIM
NSA_KV_DIM = NSA_GROUPS * HEAD_DIM
CMP_BLOCK = 32
CMP_STRIDE = 16
CMP_HIDDEN = 256
SEL_BLOCK = 64
SEL_TOP_N = 16
WINDOW = 512
REL_BUCKETS = 32
REL_MAX_DIST = 128

SSD_D_INNER = 2 * D_MODEL
SSD_HEAD_DIM = 64
SSD_HEADS = SSD_D_INNER // SSD_HEAD_DIM
SSD_GROUPS = 8
SSD_STATE = 128
SSD_CONV = 4
SSD_CHUNK = 128
SSD_CONV_DIM = SSD_D_INNER + 2 * SSD_GROUPS * SSD_STATE
MLP_HIDDEN = 4 * D_MODEL

LANES = 128
SUBLANES = 8
VMEM_LIMIT_BYTES = 56 * 1024 * 1024

VT_BLOCK = 128
TQC = 2048
TQA = 256
TKA = 256
ATTN_TILES_PER_STEP = 2
N_CMP_PAD = 128
N_SEL_PAD = 128
MASKED = -1e30
LOG2E = math.log2(math.e)
SCORE_BIG = 3e38
N_SEL_VARIANTS = 3
N_WIN_TILES = WINDOW // TKA + 1
ACC_ROWS = HEAD_DIM + 16

NT_DIMS = (((1,), (1,)), ((), ()))


def _dot(a, b):
    return jnp.dot(a, b, preferred_element_type=F32)


def _dot_nt(a, b):
    return lax.dot_general(a, b, NT_DIMS, preferred_element_type=F32)


def _split3(x):
    hi = x.astype(BF16)
    r1 = x - hi.astype(F32)
    mid = r1.astype(BF16)
    lo = (r1 - mid.astype(F32)).astype(BF16)
    return hi, mid, lo


def _rms_rows(x, w):
    return x * lax.rsqrt(jnp.mean(x * x, axis=-1, keepdims=True) + EPS) * w


def _silu(x):
    h = 0.5 * x
    return h + h * jnp.tanh(h)


def _const_spec(shape):
    nd = len(shape)
    return pl.BlockSpec(shape, lambda *_: (0,) * nd, pipeline_mode=pl.Buffered(1))


def _compiler_params(semantics):
    return pltpu.CompilerParams(dimension_semantics=semantics, vmem_limit_bytes=VMEM_LIMIT_BYTES)


def _nsa_proj_kernel(x_ref, blk_ref, nw_ref, wq_ref, wk_ref, wvt_ref, wgt_ref, qg_ref, kg_ref, seg_ref,
                     q_ref, kc_ref, vc_ref, k_ref, vt_ref, gate_ref):
    tm = x_ref.shape[1]
    h = _rms_rows(x_ref[0], nw_ref[...]).astype(BF16)
    seg = seg_ref[...]

    def head_rms(y, gain):
        sq = y * y
        hi = sq.astype(BF16)
        lo = (sq - hi.astype(F32)).astype(BF16)
        ss = _dot(hi, seg) + _dot(lo, seg)
        return y * lax.rsqrt(ss * (1.0 / HEAD_DIM) + EPS) * gain

    def emit_q(c, y):
        y = (head_rms(y, qg_ref[...]) * (HEAD_DIM ** -0.5 * LOG2E)).astype(BF16)
        for r in range(4):
            q_ref[0, 4 * c + r] = y[:, 64 * r:64 * r + 64]

    def emit_raw(c, y):
        raw_ref = (kc_ref, vc_ref)[c]
        for g in range(NSA_GROUPS):
            raw_ref[0, g] = y[:, 64 * g:64 * g + 64]

    def emit_k(br, y):
        y = head_rms(y, kg_ref[...]).astype(BF16)
        tag = blk_ref[...] if br == 0 else jnp.zeros(blk_ref.shape, BF16)
        for g in range(NSA_GROUPS):
            k_ref[0, g, br, :, 0:HEAD_DIM] = y[:, 64 * g:64 * g + 64]
            k_ref[0, g, br, :, HEAD_DIM:2 * HEAD_DIM] = tag

    def emit_vt(br, yt):
        yt = yt.astype(BF16)
        for g in range(NSA_GROUPS):
            for kb in range(tm // VT_BLOCK):
                vt_ref[0, g, br, kb] = yt[64 * g:64 * g + 64, VT_BLOCK * kb:VT_BLOCK * (kb + 1)]

    def emit_gates(_, yt):
        gate_ref[0] = jax.nn.sigmoid(yt)

    stages = ([(lambda c=c: _dot(h, wq_ref[:, 256 * c:256 * (c + 1)]), emit_q, c) for c in range(NSA_Q_DIM // 256)]
              + [(lambda c=c: _dot(h, wk_ref[c]), emit_raw, c) for c in range(2)]
              + [(lambda br=br: _dot(h, wk_ref[2 + br]), emit_k, br) for br in range(2)]
              + [(lambda br=br: _dot_nt(wvt_ref[br], h), emit_vt, br) for br in range(2)]
              + [(lambda: _dot_nt(wgt_ref[...], h), emit_gates, 0)])
    pending = stages[0][0]()
    for n, (_, emit, arg) in enumerate(stages):
        y = pending
        if n + 1 < len(stages):
            pending = stages[n + 1][0]()
        emit(arg, y)


def _nsa_proj(x, key_block, nw, wq, wk, wvt, wgt, qg, kg, seg, tm):
    B, S, D = x.shape
    G = NSA_GROUPS
    raw_block = pl.BlockSpec((1, G, tm, HEAD_DIM), lambda b, i: (b, 0, i, 0))
    raw_shape = jax.ShapeDtypeStruct((B, G, S, HEAD_DIM), F32)
    consts = (nw, wq, wk, wvt, wgt, qg, kg, seg)
    return pl.pallas_call(
        _nsa_proj_kernel,
        grid=(B, S // tm),
        in_specs=[pl.BlockSpec((1, tm, D), lambda b, i: (b, i, 0)),
                  pl.BlockSpec((tm, HEAD_DIM), lambda b, i: (i, 0))] + [_const_spec(c.shape) for c in consts],
        out_specs=[
            pl.BlockSpec((1, NSA_HEADS, tm, HEAD_DIM), lambda b, i: (b, 0, i, 0)),
            raw_block, raw_block,
            pl.BlockSpec((1, G, 2, tm, 2 * HEAD_DIM), lambda b, i: (b, 0, 0, i, 0)),
            pl.BlockSpec((1, G, 2, tm // VT_BLOCK, HEAD_DIM, VT_BLOCK), lambda b, i: (b, 0, 0, i, 0, 0)),
            pl.BlockSpec((1, G * 16, tm), lambda b, i: (b, 0, i)),
        ],
        out_shape=[
            jax.ShapeDtypeStruct((B, NSA_HEADS, S, HEAD_DIM), BF16),
            raw_shape, raw_shape,
            jax.ShapeDtypeStruct((B, G, 2, S, 2 * HEAD_DIM), BF16),
            jax.ShapeDtypeStruct((B, G, 2, S // VT_BLOCK, HEAD_DIM, VT_BLOCK), BF16),
            jax.ShapeDtypeStruct((B, G * 16, S), F32),
        ],
        compiler_params=_compiler_params(("parallel", "parallel")),
        name="nsa_proj",
    )(x, key_block, *consts)


def _nsa_compress_kernel(kc_ref, vc_ref, pek_ref, pev_ref, w1k_ref, w1v_ref, w2k_ref, w2vt_ref, kg_ref,
                         kcmp_ref, vcmp_ref):
    half = CMP_BLOCK // 2

    def hidden(src_ref, pe_ref, w1_ref):
        acc_a = jnp.zeros((N_CMP_PAD, CMP_HIDDEN), F32)
        acc_b = jnp.zeros((N_CMP_PAD, CMP_HIDDEN), F32)
        for l in range(half):
            rows = src_ref[0, 0, pl.ds(l, N_CMP_PAD, stride=CMP_STRIDE), :]
            acc_a += _dot((rows + pe_ref[l:l + 1, :]).astype(BF16), w1_ref[l])
            acc_b += _dot((rows + pe_ref[half + l:half + l + 1, :]).astype(BF16), w1_ref[half + l])
        hid = acc_a + pltpu.roll(acc_b, N_CMP_PAD - 1, axis=0)
        return _silu(hid).astype(BF16)

    k = _dot(hidden(kc_ref, pek_ref, w1k_ref), w2k_ref[...])
    kcmp_ref[0, 0] = _rms_rows(k, kg_ref[...]).astype(BF16)
    vcmp_ref[0, 0] = _dot_nt(w2vt_ref[...], hidden(vc_ref, pev_ref, w1v_ref)).astype(BF16)


def _nsa_compress(kc, vc, pek, pev, w1k, w1v, w2k, w2vt, kg):
    B, G, S, Dh = kc.shape
    src = pl.BlockSpec((1, 1, S, Dh), lambda b, g: (b, g, 0, 0))
    consts = (pek, pev, w1k, w1v, w2k, w2vt, kg)
    return pl.pallas_call(
        _nsa_compress_kernel,
        grid=(B, G),
        in_specs=[src, src] + [_const_spec(c.shape) for c in consts],
        out_specs=[pl.BlockSpec((1, 1, N_CMP_PAD, Dh), lambda b, g: (b, g, 0, 0)),
                   pl.BlockSpec((1, 1, Dh, N_CMP_PAD), lambda b, g: (b, g, 0, 0))],
        out_shape=[jax.ShapeDtypeStruct((B, G, N_CMP_PAD, Dh), BF16),
                   jax.ShapeDtypeStruct((B, G, Dh, N_CMP_PAD), BF16)],
        compiler_params=_compiler_params(("parallel", "parallel")),
        name="nsa_compress",
    )(kc, vc, *consts)


BIAS_CHUNK = 8192
BIAS_CODES = 64


def _bias_table_kernel(t3_ref, code_ref, out_ref):
    code = code_ref[...]
    ids = lax.broadcasted_iota(jnp.int32, (BIAS_CODES, code.shape[1]), 0)
    onehot = jnp.where(ids == code, 1.0, 0.0).astype(BF16)
    y = _dot(t3_ref[...], onehot)
    out_ref[...] = y[0:NSA_HEADS] + y[NSA_HEADS:2 * NSA_HEADS] + y[2 * NSA_HEADS:3 * NSA_HEADS]


def _bias_table(table3, code):
    n = code.shape[1]
    return pl.pallas_call(
        _bias_table_kernel,
        grid=(n // BIAS_CHUNK,),
        in_specs=[_const_spec(table3.shape), pl.BlockSpec((1, BIAS_CHUNK), lambda i: (0, i))],
        out_specs=pl.BlockSpec((NSA_HEADS, BIAS_CHUNK), lambda i: (0, i)),
        out_shape=jax.ShapeDtypeStruct((NSA_HEADS, n), F32),
        compiler_params=_compiler_params(("parallel",)),
        name="rel_bias_table",
    )(table3, code)


def _nsa_cmp_select_kernel(q_ref, gate_ref, kcmp_ref, vcmpt_ref, bias_ref, ovl_ref, o_ref, qa_ref):
    i = pl.program_id(2)
    kcmp = kcmp_ref[0, 0]
    vcmpt = vcmpt_ref[0, 0]

    scores = [_dot_nt(kcmp, q_ref[0, r]) for r in range(NSA_REP)]
    probs = []
    for r in range(NSA_REP):
        s = scores[r] + bias_ref[r, 0]
        m = jnp.max(s, axis=0, keepdims=True)
        m = jnp.where(m > 0.5 * MASKED, m, 0.0)
        e = jnp.exp2(s - m)
        den = jnp.sum(e, axis=0, keepdims=True)
        probs.append(e / jnp.where(den > 0.0, den, 1.0))
    outs = [_dot(vcmpt, p.astype(BF16)) for p in probs]

    ph, pm, plo = _split3(probs[0] + probs[1] + probs[2] + probs[3])
    ovl = ovl_ref[...]
    n_sel = 32
    imp = (_dot(ovl, ph) + _dot(ovl, pm) + _dot(ovl, plo))[0:n_sel]
    tok = i * TQC + lax.broadcasted_iota(jnp.int32, (n_sel, TQC), 1)
    blk = lax.broadcasted_iota(jnp.int32, (n_sel, TQC), 0)
    cur = tok // SEL_BLOCK
    forced = (blk == 0) | (blk == cur) | (blk == cur - 1)
    valid = blk * SEL_BLOCK <= tok
    score = jnp.where(forced, SCORE_BIG, imp)
    score = jnp.where(valid, score, -SCORE_BIG)
    groups = [slice(SUBLANES * v, SUBLANES * (v + 1)) for v in range(n_sel // SUBLANES)]
    ranks = [jnp.zeros((SUBLANES, TQC), F32) for _ in groups]
    for jp in range(n_sel):
        other = score[jp:jp + 1, :]
        for v, rows in enumerate(groups):
            mine = score[rows]
            if rows.start > jp:
                beats = other >= mine
            elif rows.stop - 1 <= jp:
                beats = other > mine
            else:
                later = lax.broadcasted_iota(jnp.int32, (SUBLANES, TQC), 0) > jp - rows.start
                beats = (other > mine) | ((other == mine) & later)
            ranks[v] = ranks[v] + jnp.where(beats, 1.0, 0.0)
    rank = jnp.concatenate(ranks, axis=0)
    blk_mask = jnp.where(rank < SEL_TOP_N, 0.0, MASKED)
    blk_mask = jnp.concatenate([blk_mask, jnp.zeros((HEAD_DIM - n_sel, TQC), F32)], axis=0).T.astype(BF16)
    for r in range(NSA_REP):
        qa_ref[0, r, :, 0:HEAD_DIM] = q_ref[0, r]
        qa_ref[0, r, :, HEAD_DIM:2 * HEAD_DIM] = blk_mask

    gate = gate_ref[0]
    for r in range(NSA_REP):
        o_ref[0, :, r * HEAD_DIM:(r + 1) * HEAD_DIM] = (outs[r] * gate[r:r + 1]).T


def _nsa_cmp_select(q, gates, kcmp, vcmpt, bcmp, ovl):
    B, H, S, Dh = q.shape
    G = NSA_GROUPS
    return pl.pallas_call(
        _nsa_cmp_select_kernel,
        grid=(G, B, S // TQC),
        in_specs=[
            pl.BlockSpec((1, NSA_REP, TQC, Dh), lambda g, b, i: (b, g, i, 0)),
            pl.BlockSpec((1, 16, TQC), lambda g, b, i: (b, g, i)),
            pl.BlockSpec((1, 1, N_CMP_PAD, Dh), lambda g, b, i: (b, g, 0, 0)),
            pl.BlockSpec((1, 1, Dh, N_CMP_PAD), lambda g, b, i: (b, g, 0, 0)),
            pl.BlockSpec((NSA_REP, 1, N_CMP_PAD, TQC), lambda g, b, i: (g, i, 0, 0)),
            _const_spec(ovl.shape),
        ],
        out_specs=[
            pl.BlockSpec((1, TQC, NSA_REP * Dh), lambda g, b, i: (b, i, g)),
            pl.BlockSpec((1, NSA_REP, TQC, 2 * Dh), lambda g, b, i: (b, g, i, 0)),
        ],
        out_shape=[
            jax.ShapeDtypeStruct((B, S, H * Dh), F32),
            jax.ShapeDtypeStruct((B, H, S, 2 * Dh), BF16),
        ],
        compiler_params=_compiler_params(("parallel", "parallel", "arbitrary")),
        name="nsa_cmp_select",
    )(q, gates, kcmp, vcmpt, bcmp, ovl)


def _nsa_attn_kernel(*refs):
    for inner in range(ATTN_TILES_PER_STEP):
        _nsa_attn_tile(inner, *refs)


def _nsa_attn_tile(inner, q_ref, gate_ref, k_ref, vt_ref, ocmp_ref, bias_ref,
                   o_ref, s0_ref, s1_ref, x0_ref, x1_ref, p0_ref, p1_ref, a0_ref, a1_ref,
                   m_ref, acc_ref):
    i = pl.program_id(2) * ATTN_TILES_PER_STEP + inner
    last_step = pl.num_programs(2) * ATTN_TILES_PER_STEP - 1
    rows = slice(inner * TQA, (inner + 1) * TQA)
    s_bufs, x_bufs = (s0_ref, s1_ref), (x0_ref, x1_ref)
    p_bufs, a_bufs = (p0_ref, p1_ref), (a0_ref, a1_ref)
    masked_variant = N_SEL_VARIANTS + N_WIN_TILES

    def n_tasks_of(step):
        n_sel = step + 1
        return n_sel, n_sel + jnp.minimum(N_WIN_TILES, n_sel)

    def task(t, step):
        n_sel, n_tasks = n_tasks_of(step)
        is_win = t >= n_sel
        c = jnp.clip(jnp.where(is_win, step - (t - n_sel), t), 0, step)
        behind = step - c
        variant = jnp.where(is_win, N_SEL_VARIANTS + behind, jnp.minimum(behind, N_SEL_VARIANTS - 1))
        variant = jnp.where(t < n_tasks, variant, masked_variant)
        return is_win.astype(jnp.int32), c, variant

    def score_pair(u, step):
        pair = [task(2 * u, step), task(2 * u + 1, step)]
        q0 = pl.multiple_of(step * TQA, TQA)
        k = jnp.concatenate([k_ref[0, 0, br, pl.ds(pl.multiple_of(c * TKA, TKA), TKA), :] for br, c, _ in pair],
                            axis=0)
        for r in range(NSA_REP):
            s2 = _dot_nt(k, q_ref[0, r, pl.ds(q0, TQA), :])
            for slot, (_, _, variant) in enumerate(pair):
                s = s2[slot * TKA:(slot + 1) * TKA] + bias_ref[r, variant]
                s_bufs[slot][r] = s
                x_bufs[slot][r] = jnp.max(s, axis=0, keepdims=True)

    def softmax_stage(t, slot):
        br, _, _ = task(t, i)
        for r in range(NSA_REP):
            m_old = m_ref[br, r]
            m_new = jnp.maximum(m_old, x_bufs[slot][r])
            m_ref[br, r] = m_new
            a_bufs[slot][r] = jnp.exp2(m_old - m_new)
            p_bufs[slot][r] = jnp.exp2(s_bufs[slot][r] - m_new).astype(BF16)

    ones_rows = jnp.where(lax.broadcasted_iota(jnp.int32, (ACC_ROWS - HEAD_DIM, TKA), 0) == 0, 1.0, 0.0).astype(BF16)

    def value_stage(t, slot):
        br, c, _ = task(t, i)
        blocks = TKA // VT_BLOCK
        vt = jnp.concatenate([vt_ref[0, 0, br, blocks * c + u] for u in range(blocks)], axis=1)
        vt = jnp.concatenate([vt, ones_rows], axis=0)
        for r in range(NSA_REP):
            acc_ref[br, r] = a_bufs[slot][r] * acc_ref[br, r] + _dot(vt, p_bufs[slot][r])

    m_ref[...] = jnp.full(m_ref.shape, MASKED, F32)
    acc_ref[...] = jnp.zeros(acc_ref.shape, F32)

    if inner == 0:
        @pl.when(i == 0)
        def _():
            score_pair(0, i)

    n_trips = (n_tasks_of(i)[1] + 1) // 2

    def softmax_and_next_scores(u):
        softmax_stage(2 * u, 0)
        softmax_stage(2 * u + 1, 1)
        in_tile = u + 1 < n_trips
        score_pair(jnp.where(in_tile, u + 1, 0), jnp.where(in_tile, i, jnp.minimum(i + 1, last_step)))

    def trip(u, carry):
        value_stage(2 * u - 2, 0)
        value_stage(2 * u - 1, 1)
        softmax_and_next_scores(u)
        return carry

    softmax_and_next_scores(0)
    lax.fori_loop(1, n_trips, trip, 0)
    value_stage(2 * n_trips - 2, 0)
    value_stage(2 * n_trips - 1, 1)

    gate = gate_ref[0, :, rows]
    for r in range(NSA_REP):
        w_sel = gate[4 + r:5 + r] / acc_ref[0, r, HEAD_DIM:HEAD_DIM + 1, :]
        w_win = gate[8 + r:9 + r] / acc_ref[1, r, HEAD_DIM:HEAD_DIM + 1, :]
        o_t = w_sel * acc_ref[0, r, 0:HEAD_DIM, :] + w_win * acc_ref[1, r, 0:HEAD_DIM, :]
        cols = slice(r * HEAD_DIM, (r + 1) * HEAD_DIM)
        o_ref[0, rows, cols] = (ocmp_ref[0, rows, cols] + o_t.T).astype(BF16)


def _nsa_attn(q_aug, gates, k_sw, vt_sw, o_cmp, bias):
    B, H, S, Dq = q_aug.shape
    Dh = HEAD_DIM
    G = NSA_GROUPS
    n_vt = S // VT_BLOCK
    n_var = bias.shape[1]
    step_rows = ATTN_TILES_PER_STEP * TQA
    return pl.pallas_call(
        _nsa_attn_kernel,
        grid=(G, B, S // step_rows),
        in_specs=[
            pl.BlockSpec((1, NSA_REP, S, Dq), lambda g, b, i: (b, g, 0, 0)),
            pl.BlockSpec((1, 16, step_rows), lambda g, b, i: (b, g, i)),
            pl.BlockSpec((1, 1, 2, S, Dq), lambda g, b, i: (b, g, 0, 0, 0)),
            pl.BlockSpec((1, 1, 2, n_vt, Dh, VT_BLOCK), lambda g, b, i: (b, g, 0, 0, 0, 0)),
            pl.BlockSpec((1, step_rows, NSA_REP * Dh), lambda g, b, i: (b, i, g)),
            pl.BlockSpec((NSA_REP, n_var, TKA, TQA), lambda g, b, i: (g, 0, 0, 0), pipeline_mode=pl.Buffered(1)),
        ],
        out_specs=pl.BlockSpec((1, step_rows, NSA_REP * Dh), lambda g, b, i: (b, i, g)),
        out_shape=jax.ShapeDtypeStruct((B, S, H * Dh), BF16),
        scratch_shapes=[
            pltpu.VMEM((NSA_REP, TKA, TQA), F32),
            pltpu.VMEM((NSA_REP, TKA, TQA), F32),
            pltpu.VMEM((NSA_REP, 1, TQA), F32),
            pltpu.VMEM((NSA_REP, 1, TQA), F32),
            pltpu.VMEM((NSA_REP, TKA, TQA), BF16),
            pltpu.VMEM((NSA_REP, TKA, TQA), BF16),
            pltpu.VMEM((NSA_REP, 1, TQA), F32),
            pltpu.VMEM((NSA_REP, 1, TQA), F32),
            pltpu.VMEM((2, NSA_REP, 1, TQA), F32),
            pltpu.VMEM((2, NSA_REP, ACC_ROWS, TQA), F32),
        ],
        compiler_params=_compiler_params(("parallel", "parallel", "arbitrary")),
        name="nsa_attn",
    )(q_aug, gates, k_sw, vt_sw, o_cmp, bias)


MLP_CHUNK = 1024


def _mix_out_mlp_kernel(x_ref, a_ref, wo_ref, nw_ref, wup_ref, wdn_ref, o_ref):
    x1 = x_ref[...] + _dot(a_ref[...], wo_ref[...])
    h = _rms_rows(x1, nw_ref[...]).astype(BF16)

    acc = None
    for c in range(wup_ref.shape[1] // MLP_CHUNK):
        hid = slice(c * MLP_CHUNK, (c + 1) * MLP_CHUNK)
        u = jnp.maximum(_dot(h, wup_ref[:, hid]), 0.0)
        d = _dot((u * u).astype(BF16), wdn_ref[hid, :])
        acc = d if acc is None else acc + d
    o_ref[...] = x1 + acc


def _mix_out_mlp(x, a, wo, nw, wup, wdn, tm):
    T, D = x.shape
    K = a.shape[1]
    return pl.pallas_call(
        _mix_out_mlp_kernel,
        grid=(T // tm,),
        in_specs=[
            pl.BlockSpec((tm, D), lambda i: (i, 0)),
            pl.BlockSpec((tm, K), lambda i: (i, 0)),
            _const_spec(wo.shape), _const_spec(nw.shape), _const_spec(wup.shape), _const_spec(wdn.shape),
        ],
        out_specs=pl.BlockSpec((tm, D), lambda i: (i, 0)),
        out_shape=jax.ShapeDtypeStruct((T, D), F32),
        compiler_params=_compiler_params(("parallel",)),
        name="mix_out_mlp",
    )(x, a, wo, nw, wup, wdn)


SSD_NCHUNK = 512


def _ssd_proj_kernel(x_ref, nw_ref, wz_ref, wx_ref, wdt_ref, cw_ref, cb_ref, dtb_ref,
                     z_ref, xbc_ref, dt_ref, buf_ref):
    i = pl.program_id(1)
    tm = x_ref.shape[1]
    @pl.when(i == 0)
    def _():
        buf_ref[0:SUBLANES, :] = jnp.zeros((SUBLANES, buf_ref.shape[1]), F32)

    @pl.when(i > 0)
    def _():
        buf_ref[0:SUBLANES, :] = buf_ref[tm:tm + SUBLANES, :]

    h = _rms_rows(x_ref[0], nw_ref[...]).astype(BF16)

    for c in range(wz_ref.shape[0]):
        z_ref[0, :, c * SSD_NCHUNK:(c + 1) * SSD_NCHUNK] = _dot(h, wz_ref[c])

    for c in range(wx_ref.shape[0]):
        cols = slice(c * SSD_NCHUNK, (c + 1) * SSD_NCHUNK)
        buf_ref[SUBLANES:SUBLANES + tm, cols] = _dot(h, wx_ref[c])
        acc = cb_ref[:, cols] + buf_ref[SUBLANES:SUBLANES + tm, cols] * cw_ref[SSD_CONV - 1:SSD_CONV, cols]
        for k in range(SSD_CONV - 1):
            off = SUBLANES - (SSD_CONV - 1) + k
            acc += buf_ref[off:off + tm, cols] * cw_ref[k:k + 1, cols]
        xbc_ref[0, :, cols] = _silu(acc)

    dt_ref[0] = jax.nn.softplus(_dot(h, wdt_ref[...]) + dtb_ref[...])


def _ssd_proj(x, nw, wz, wx, wdt, cw, cb, dtb, tm):
    B, S, D = x.shape
    consts = (nw, wz, wx, wdt, cw, cb, dtb)
    return pl.pallas_call(
        _ssd_proj_kernel,
        grid=(B, S // tm),
        in_specs=[pl.BlockSpec((1, tm, D), lambda b, i: (b, i, 0))] + [_const_spec(c.shape) for c in consts],
        out_specs=[
            pl.BlockSpec((1, tm, SSD_D_INNER), lambda b, i: (b, i, 0)),
            pl.BlockSpec((1, tm, SSD_CONV_DIM), lambda b, i: (b, i, 0)),
            pl.BlockSpec((1, tm, LANES), lambda b, i: (b, i, 0)),
        ],
        out_shape=[
            jax.ShapeDtypeStruct((B, S, SSD_D_INNER), F32),
            jax.ShapeDtypeStruct((B, S, SSD_CONV_DIM), F32),
            jax.ShapeDtypeStruct((B, S, LANES), F32),
        ],
        scratch_shapes=[pltpu.VMEM((tm + SUBLANES, SSD_CONV_DIM), F32)],
        compiler_params=_compiler_params(("parallel", "arbitrary")),
        name="ssd_proj",
    )(x, *consts)


SCAN_CHUNKS_PER_STEP = 2


def _ssd_scan_kernel(*refs):
    for inner in range(SCAN_CHUNKS_PER_STEP):
        _ssd_scan_chunk(inner, *refs)


def _ssd_scan_chunk(inner, xbc_ref, dt_ref, z_ref, alog_ref, dskip_ref, nw_ref, tril_ref, hexp_ref,
                    y_ref, state_ref, acs_ref, acst_ref, dtt_ref, e3_ref, wt_ref):
    L = SSD_CHUNK
    P = SSD_HEAD_DIM
    N = SSD_STATE
    hpg = SSD_HEADS // SSD_GROUPS
    gw = hpg * P
    chunk = pl.program_id(1) * SCAN_CHUNKS_PER_STEP + inner
    last_chunk = pl.num_programs(1) * SCAN_CHUNKS_PER_STEP - 1
    rows = slice(inner * L, (inner + 1) * L)

    def chunk_decays(ci):
        dt = dt_ref[0, pl.ds(pl.multiple_of(ci * L, L), L), :]
        a = dt * (-jnp.exp(alog_ref[...]))
        tril = tril_ref[...]
        ah, am, al = _split3(a)
        acs = _dot(tril, ah) + _dot(tril, am) + _dot(tril, al)
        acs_t = acs.T
        dt_t = dt.T
        acs_ref[...] = acs
        acst_ref[...] = acs_t
        dtt_ref[...] = dt_t
        for n, term in enumerate(_split3(jnp.exp(acs))):
            e3_ref[n] = term
        wt_ref[...] = jnp.exp(acs_t[:, L - 1:L] - acs_t) * dt_t

    if inner == 0:
        @pl.when(chunk == 0)
        def _():
            state_ref[...] = jnp.zeros(state_ref.shape, F32)
            chunk_decays(0)

    acs, acs_t, dt_t, w_t = acs_ref[...], acst_ref[...], dtt_ref[...], wt_ref[...]
    eh, em, el = e3_ref[0], e3_ref[1], e3_ref[2]
    causal = (lax.broadcasted_iota(jnp.int32, (L, L), 0) >= lax.broadcasted_iota(jnp.int32, (L, L), 1))
    head_of_lane = lax.broadcasted_iota(jnp.int32, (L, gw), 1) // P

    for g in range(SSD_GROUPS):
        b_off = SSD_D_INNER + g * N
        c_off = SSD_D_INNER + SSD_GROUPS * N + g * N
        bm = xbc_ref[0, rows, b_off:b_off + N]
        cm = xbc_ref[0, rows, c_off:c_off + N].astype(BF16)
        cb = _dot_nt(cm, bm.astype(BF16))
        bm_t = bm.T
        xg = xbc_ref[0, rows, g * gw:(g + 1) * gw]
        hexp = hexp_ref[g]
        e_exp = _dot(eh, hexp) + _dot(em, hexp) + _dot(el, hexp)
        st = state_ref[g]
        y = _dot(cm, st.astype(BF16)) * e_exp
        st = st * e_exp[L - 1:L, :]
        for j in range(hpg):
            hd = g * hpg + j
            xj = jnp.where(head_of_lane == j, xg, 0.0).astype(BF16)
            diff = acs[:, hd:hd + 1] - acs_t[hd:hd + 1, :]
            decay = jnp.exp(jnp.where(causal, diff, -jnp.inf))
            y = y + _dot((cb * decay * dt_t[hd:hd + 1, :]).astype(BF16), xj)
            st = st + _dot((bm_t * w_t[hd:hd + 1, :]).astype(BF16), xj)
        state_ref[g] = st
        cols = slice(g * gw, (g + 1) * gw)
        zg = z_ref[0, rows, cols]
        y = (y + xg * dskip_ref[:, cols]) * _silu(zg)
        y_ref[0, rows, cols] = _rms_rows(y, nw_ref[:, cols]).astype(BF16)

    chunk_decays(jnp.minimum(chunk + 1, last_chunk))


def _ssd_scan(xbc, dt, z, alog, dskip, nw, tril, hexp):
    B, S, _ = xbc.shape
    L = SSD_CHUNK
    consts = (alog, dskip, nw, tril, hexp)
    step_rows = SCAN_CHUNKS_PER_STEP * L
    return pl.pallas_call(
        _ssd_scan_kernel,
        grid=(B, S // step_rows),
        in_specs=[
            pl.BlockSpec((1, step_rows, SSD_CONV_DIM), lambda b, c: (b, c, 0)),
            pl.BlockSpec((1, S, LANES), lambda b, c: (b, 0, 0)),
            pl.BlockSpec((1, step_rows, SSD_D_INNER), lambda b, c: (b, c, 0)),
        ] + [_const_spec(c.shape) for c in consts],
        out_specs=pl.BlockSpec((1, step_rows, SSD_D_INNER), lambda b, c: (b, c, 0)),
        out_shape=jax.ShapeDtypeStruct((B, S, SSD_D_INNER), BF16),
        scratch_shapes=[
            pltpu.VMEM((SSD_GROUPS, SSD_STATE, SSD_D_INNER // SSD_GROUPS), F32),
            pltpu.VMEM((L, LANES), F32),
            pltpu.VMEM((LANES, L), F32),
            pltpu.VMEM((LANES, L), F32),
            pltpu.VMEM((3, L, LANES), BF16),
            pltpu.VMEM((LANES, L), F32),
        ],
        compiler_params=_compiler_params(("parallel", "arbitrary")),
        name="ssd_scan",
    )(xbc, dt, z, *consts)


def _rel_bucket_np(dist):
    max_exact = REL_BUCKETS // 2
    d = np.maximum(dist, 0)
    df = np.maximum(d, 1).astype(np.float32)
    large = max_exact + (np.log(df / np.float32(max_exact)) / np.float32(math.log(REL_MAX_DIST / max_exact))
                         * np.float32(REL_BUCKETS - max_exact)).astype(np.int32)
    large = np.minimum(large, REL_BUCKETS - 1)
    return np.where(d < max_exact, d, large).astype(np.int32)


@functools.lru_cache(maxsize=None)
def _position_tables(S):
    n_cmp = (S - CMP_BLOCK) // CMP_STRIDE + 1
    n_sel = S // SEL_BLOCK

    masked_code = REL_BUCKETS

    def code(dist, ok):
        return np.where(ok, _rel_bucket_np(dist), masked_code).astype(np.int32).reshape(1, -1)

    n = np.arange(N_CMP_PAD)[None, :, None]
    t = np.arange(S // TQC)[:, None, None] * TQC + np.arange(TQC)[None, None, :]
    dist = t - (n * CMP_STRIDE + CMP_BLOCK - 1)
    cmp_code = code(dist, (dist >= 0) & (n < n_cmp))
    assert TQA == TKA and (N_SEL_VARIANTS - 1) * TQA - (TKA - 1) >= REL_MAX_DIST
    ti = np.arange(TQA)[None, :]
    j = np.arange(TKA)[:, None]
    tile_dist = [TQA * behind + ti - j for behind in range(N_WIN_TILES)]
    far = np.full((TKA, TQA), 2 * REL_MAX_DIST)
    sel_dist = np.stack(tile_dist[:N_SEL_VARIANTS - 1] + [far])
    win_dist = np.stack(tile_dist)
    attn_code = np.concatenate([code(sel_dist, sel_dist >= 0),
                                code(win_dist, (win_dist >= 0) & (win_dist < WINDOW)),
                                np.full((1, TKA * TQA), masked_code, np.int32)],
                               axis=1)
    jb = np.arange(N_SEL_PAD)[:, None]
    nn = np.arange(N_CMP_PAD)[None, :]
    overlap = ((nn * CMP_STRIDE < jb * SEL_BLOCK + SEL_BLOCK) & (nn * CMP_STRIDE + CMP_BLOCK > jb * SEL_BLOCK)
               & (nn < n_cmp) & (jb < n_sel)).astype(np.float32)
    key_block = (np.arange(S)[:, None] // SEL_BLOCK == np.arange(HEAD_DIM)[None, :]).astype(np.float32)
    hpg = SSD_HEADS // SSD_GROUPS
    lane_head = np.arange(hpg * SSD_HEAD_DIM)[None, None, :] // SSD_HEAD_DIM
    head_expand = (np.arange(LANES)[None, :, None]
                   == np.arange(SSD_GROUPS)[:, None, None] * hpg + lane_head).astype(np.float32)
    return cmp_code, attn_code, overlap, key_block, head_expand


def kernel(x, norm_mix_w, norm_mlp_w, rel_table, nsa_w_in, nsa_q_gain, nsa_k_gain, cmp_pe_k, cmp_w1_k, cmp_w2_k, cmp_pe_v, cmp_w1_v, cmp_w2_v, nsa_w_out, ssd_w_in, ssd_conv_w, ssd_conv_b, ssd_dt_bias, ssd_a_log, ssd_d, ssd_norm_w, ssd_w_out, mlp_w_up, mlp_w_down):
    B, S, D = x.shape
    T = B * S
    G, R = NSA_GROUPS, NSA_REP
    cmp_code, attn_code, overlap, key_block, head_expand = _position_tables(S)

    w_in = nsa_w_in[0].astype(BF16)
    kv0 = NSA_Q_DIM
    kv_cols = lambda c: w_in[:, kv0 + c * NSA_KV_DIM:kv0 + (c + 1) * NSA_KV_DIM]
    wq = w_in[:, :NSA_Q_DIM]
    wk = jnp.stack([kv_cols(0), kv_cols(1), kv_cols(2), kv_cols(4)])
    wvt = jnp.stack([kv_cols(3).T, kv_cols(5).T])
    wg = w_in[:, kv0 + 6 * NSA_KV_DIM:].reshape(D, 3, G, R).transpose(0, 2, 1, 3).reshape(D, G, 3 * R)
    wgt = jnp.pad(wg, ((0, 0), (0, 0), (0, 16 - 3 * R))).reshape(D, G * 16).T
    qg = jnp.tile(nsa_q_gain[0], 4)[None, :]
    kg = jnp.tile(nsa_k_gain[0], 4)[None, :]
    seg = jnp.asarray(np.kron(np.eye(4, dtype=np.float32), np.ones((HEAD_DIM, HEAD_DIM), np.float32)), BF16)
    q, kc, vc, k_sw, vt_sw, gates = _nsa_proj(
        x, jnp.asarray(key_block, BF16), norm_mix_w[0][None, :], wq, wk, wvt, wgt, qg, kg, seg, tm=512)

    kcmp, vcmpt = _nsa_compress(
        kc, vc, cmp_pe_k[0], cmp_pe_v[0], cmp_w1_k[0].astype(BF16), cmp_w1_v[0].astype(BF16),
        cmp_w2_k[0].astype(BF16), cmp_w2_v[0].astype(BF16).T, nsa_k_gain[0][None, :])

    by_code = jnp.concatenate([rel_table.T * LOG2E, jnp.full((NSA_HEADS, 1), MASKED, F32),
                               jnp.zeros((NSA_HEADS, BIAS_CODES - REL_BUCKETS - 1), F32)], axis=1)
    table3 = jnp.concatenate(_split3(by_code), axis=0)
    bcmp = _bias_table(table3, jnp.asarray(cmp_code)).reshape(NSA_HEADS, S // TQC, N_CMP_PAD, TQC)
    battn = _bias_table(table3, jnp.asarray(attn_code)).reshape(NSA_HEADS, -1, TKA, TQA)
    o_cmp, q_aug = _nsa_cmp_select(q, gates, kcmp, vcmpt, bcmp, jnp.asarray(overlap, BF16))
    o = _nsa_attn(q_aug, gates, k_sw, vt_sw, o_cmp, battn)

    def mlp_weights(li):
        return mlp_w_up[li].astype(BF16), mlp_w_down[li].astype(BF16)

    wup, wdn = mlp_weights(0)
    x2 = _mix_out_mlp(x.reshape(T, D), o.reshape(T, NSA_Q_DIM), nsa_w_out[0].astype(BF16),
                      norm_mlp_w[0][None, :], wup, wdn, tm=512)

    w_in = ssd_w_in[0]
    def col_chunks(w):
        return w.reshape(D, -1, SSD_NCHUNK).transpose(1, 0, 2).astype(BF16)

    wz = col_chunks(w_in[:, :SSD_D_INNER])
    wx = col_chunks(w_in[:, SSD_D_INNER:SSD_D_INNER + SSD_CONV_DIM])
    lane_pad = LANES - SSD_HEADS
    wdt = jnp.pad(w_in[:, SSD_D_INNER + SSD_CONV_DIM:], ((0, 0), (0, lane_pad))).astype(BF16)
    pad_heads = lambda v: jnp.pad(v, (0, lane_pad))[None, :]
    z, xbc, dt = _ssd_proj(x2.reshape(B, S, D), norm_mix_w[1][None, :], wz, wx, wdt,
                           ssd_conv_w[0], ssd_conv_b[0][None, :], pad_heads(ssd_dt_bias[0]), tm=512)
    tril = jnp.asarray(np.tril(np.ones((SSD_CHUNK, SSD_CHUNK), np.float32)), BF16)
    dskip = jnp.repeat(ssd_d[0], SSD_HEAD_DIM)[None, :]
    y = _ssd_scan(xbc, dt, z, pad_heads(ssd_a_log[0]), dskip, ssd_norm_w[0][None, :], tril,
                  jnp.asarray(head_expand, BF16))

    wup, wdn = mlp_weights(1)
    x4 = _mix_out_mlp(x2, y.reshape(T, SSD_D_INNER), ssd_w_out[0].astype(BF16),
                      norm_mlp_w[1][None, :], wup, wdn, tm=512)
    return x4.reshape(B, S, D)
```

```python
import functools
import math

import numpy as np
import jax
import jax.numpy as jnp
from jax import lax
from jax.experimental import pallas as pl
from jax.experimental.pallas import tpu as pltpu

F32 = jnp.float32
BF16 = jnp.bfloat16

D_MODEL = 1024
EPS = 1e-6

NSA_HEADS = 16
NSA_GROUPS = 4
NSA_REP = NSA_HEADS // NSA_GROUPS
HEAD_DIM = 64
NSA_Q_DIM = NSA_HEADS * HEAD_DIM
NSA_KV_DIM = NSA_GROUPS * HEAD_DIM
CMP_BLOCK = 32
CMP_STRIDE = 16
CMP_HIDDEN = 256
SEL_BLOCK = 64
SEL_TOP_N = 16
WINDOW = 512
REL_BUCKETS = 32
REL_MAX_DIST = 128

SSD_D_INNER = 2 * D_MODEL
SSD_HEAD_DIM = 64
SSD_HEADS = SSD_D_INNER // SSD_HEAD_DIM
SSD_GROUPS = 8
SSD_STATE = 128
SSD_CONV = 4
SSD_CHUNK = 128
SSD_CONV_DIM = SSD_D_INNER + 2 * SSD_GROUPS * SSD_STATE
MLP_HIDDEN = 4 * D_MODEL

LANES = 128
SUBLANES = 8
VMEM_LIMIT_BYTES = 56 * 1024 * 1024

VT_BLOCK = 128
TQC = 2048
TQA = 256
TKA = 256
ATTN_TILES_PER_STEP = 8
N_CMP_PAD = 128
N_SEL_PAD = 128
MASKED = -1e30
LOG2E = math.log2(math.e)
SCORE_BIG = 3e38
N_SEL_VARIANTS = 3
N_WIN_TILES = WINDOW // TKA + 1
ACC_ROWS = HEAD_DIM + 16

NT_DIMS = (((1,), (1,)), ((), ()))


def _dot(a, b):
    return jnp.dot(a, b, preferred_element_type=F32)


def _dot_nt(a, b):
    return lax.dot_general(a, b, NT_DIMS, preferred_element_type=F32)


def _split3(x):
    hi = x.astype(BF16)
    r1 = x - hi.astype(F32)
    mid = r1.astype(BF16)
    lo = (r1 - mid.astype(F32)).astype(BF16)
    return hi, mid, lo


def _rms_rows(x, w):
    return x * lax.rsqrt(jnp.mean(x * x, axis=-1, keepdims=True) + EPS) * w


def _silu(x):
    h = 0.5 * x
    return h + h * jnp.tanh(h)


def _const_spec(shape):
    nd = len(shape)
    return pl.BlockSpec(shape, lambda *_: (0,) * nd, pipeline_mode=pl.Buffered(1))


def _compiler_params(semantics):
    return pltpu.CompilerParams(dimension_semantics=semantics, vmem_limit_bytes=VMEM_LIMIT_BYTES)


def _nsa_proj_kernel(x_ref, blk_ref, nw_ref, wq_ref, wk_ref, wvt_ref, wgt_ref, qg_ref, kg_ref, seg_ref,
                     q_ref, kc_ref, vc_ref, k_ref, vt_ref, gate_ref):
    tm = x_ref.shape[1]
    h = _rms_rows(x_ref[0], nw_ref[...]).astype(BF16)
    seg = seg_ref[...]

    def head_rms(y, gain):
        sq = y * y
        hi = sq.astype(BF16)
        lo = (sq - hi.astype(F32)).astype(BF16)
        ss = _dot(hi, seg) + _dot(lo, seg)
        return y * lax.rsqrt(ss * (1.0 / HEAD_DIM) + EPS) * gain

    def emit_q(c, y):
        y = (head_rms(y, qg_ref[...]) * (HEAD_DIM ** -0.5 * LOG2E)).astype(BF16)
        for r in range(4):
            q_ref[0, 4 * c + r] = y[:, 64 * r:64 * r + 64]

    def emit_raw(c, y):
        raw_ref = (kc_ref, vc_ref)[c]
        for g in range(NSA_GROUPS):
            raw_ref[0, g] = y[:, 64 * g:64 * g + 64]

    def emit_k(br, y):
        y = head_rms(y, kg_ref[...]).astype(BF16)
        tag = blk_ref[...] if br == 0 else jnp.zeros(blk_ref.shape, BF16)
        for g in range(NSA_GROUPS):
            k_ref[0, g, br, :, 0:HEAD_DIM] = y[:, 64 * g:64 * g + 64]
            k_ref[0, g, br, :, HEAD_DIM:2 * HEAD_DIM] = tag

    def emit_vt(br, yt):
        yt = yt.astype(BF16)
        for g in range(NSA_GROUPS):
            for kb in range(tm // VT_BLOCK):
                vt_ref[0, g, br, kb] = yt[64 * g:64 * g + 64, VT_BLOCK * kb:VT_BLOCK * (kb + 1)]

    def emit_gates(_, yt):
        gate_ref[0] = jax.nn.sigmoid(yt)

    stages = ([(lambda c=c: _dot(h, wq_ref[:, 256 * c:256 * (c + 1)]), emit_q, c) for c in range(NSA_Q_DIM // 256)]
              + [(lambda c=c: _dot(h, wk_ref[c]), emit_raw, c) for c in range(2)]
              + [(lambda br=br: _dot(h, wk_ref[2 + br]), emit_k, br) for br in range(2)]
              + [(lambda br=br: _dot_nt(wvt_ref[br], h), emit_vt, br) for br in range(2)]
              + [(lambda: _dot_nt(wgt_ref[...], h), emit_gates, 0)])
    pending = stages[0][0]()
    for n, (_, emit, arg) in enumerate(stages):
        y = pending
        if n + 1 < len(stages):
            pending = stages[n + 1][0]()
        emit(arg, y)


def _nsa_proj(x, key_block, nw, wq, wk, wvt, wgt, qg, kg, seg, tm):
    B, S, D = x.shape
    G = NSA_GROUPS
    raw_block = pl.BlockSpec((1, G, tm, HEAD_DIM), lambda b, i: (b, 0, i, 0))
    raw_shape = jax.ShapeDtypeStruct((B, G, S, HEAD_DIM), F32)
    consts = (nw, wq, wk, wvt, wgt, qg, kg, seg)
    return pl.pallas_call(
        _nsa_proj_kernel,
        grid=(B, S // tm),
        in_specs=[pl.BlockSpec((1, tm, D), lambda b, i: (b, i, 0)),
                  pl.BlockSpec((tm, HEAD_DIM), lambda b, i: (i, 0))] + [_const_spec(c.shape) for c in consts],
        out_specs=[
            pl.BlockSpec((1, NSA_HEADS, tm, HEAD_DIM), lambda b, i: (b, 0, i, 0)),
            raw_block, raw_block,
            pl.BlockSpec((1, G, 2, tm, 2 * HEAD_DIM), lambda b, i: (b, 0, 0, i, 0)),
            pl.BlockSpec((1, G, 2, tm // VT_BLOCK, HEAD_DIM, VT_BLOCK), lambda b, i: (b, 0, 0, i, 0, 0)),
            pl.BlockSpec((1, G * 16, tm), lambda b, i: (b, 0, i)),
        ],
        out_shape=[
            jax.ShapeDtypeStruct((B, NSA_HEADS, S, HEAD_DIM), BF16),
            raw_shape, raw_shape,
            jax.ShapeDtypeStruct((B, G, 2, S, 2 * HEAD_DIM), BF16),
            jax.ShapeDtypeStruct((B, G, 2, S // VT_BLOCK, HEAD_DIM, VT_BLOCK), BF16),
            jax.ShapeDtypeStruct((B, G * 16, S), F32),
        ],
        compiler_params=_compiler_params(("parallel", "parallel")),
        name="nsa_proj",
    )(x, key_block, *consts)


def _nsa_compress_kernel(kc_ref, vc_ref, pek_ref, pev_ref, w1k_ref, w1v_ref, w2k_ref, w2vt_ref, kg_ref,
                         kcmp_ref, vcmp_ref):
    half = CMP_BLOCK // 2

    def hidden(src_ref, pe_ref, w1_ref):
        acc_a = jnp.zeros((N_CMP_PAD, CMP_HIDDEN), F32)
        acc_b = jnp.zeros((N_CMP_PAD, CMP_HIDDEN), F32)
        for l in range(half):
            rows = src_ref[0, 0, pl.ds(l, N_CMP_PAD, stride=CMP_STRIDE), :]
            acc_a += _dot((rows + pe_ref[l:l + 1, :]).astype(BF16), w1_ref[l])
            acc_b += _dot((rows + pe_ref[half + l:half + l + 1, :]).astype(BF16), w1_ref[half + l])
        hid = acc_a + pltpu.roll(acc_b, N_CMP_PAD - 1, axis=0)
        return _silu(hid).astype(BF16)

    k = _dot(hidden(kc_ref, pek_ref, w1k_ref), w2k_ref[...])
    kcmp_ref[0, 0] = _rms_rows(k, kg_ref[...]).astype(BF16)
    vcmp_ref[0, 0] = _dot_nt(w2vt_ref[...], hidden(vc_ref, pev_ref, w1v_ref)).astype(BF16)


def _nsa_compress(kc, vc, pek, pev, w1k, w1v, w2k, w2vt, kg):
    B, G, S, Dh = kc.shape
    src = pl.BlockSpec((1, 1, S, Dh), lambda b, g: (b, g, 0, 0))
    consts = (pek, pev, w1k, w1v, w2k, w2vt, kg)
    return pl.pallas_call(
        _nsa_compress_kernel,
        grid=(B, G),
        in_specs=[src, src] + [_const_spec(c.shape) for c in consts],
        out_specs=[pl.BlockSpec((1, 1, N_CMP_PAD, Dh), lambda b, g: (b, g, 0, 0)),
                   pl.BlockSpec((1, 1, Dh, N_CMP_PAD), lambda b, g: (b, g, 0, 0))],
        out_shape=[jax.ShapeDtypeStruct((B, G, N_CMP_PAD, Dh), BF16),
                   jax.ShapeDtypeStruct((B, G, Dh, N_CMP_PAD), BF16)],
        compiler_params=_compiler_params(("parallel", "parallel")),
        name="nsa_compress",
    )(kc, vc, *consts)


BIAS_CHUNK = 8192
BIAS_CODES = 64


def _bias_table_kernel(t3_ref, code_ref, out_ref):
    code = code_ref[...]
    ids = lax.broadcasted_iota(jnp.int32, (BIAS_CODES, code.shape[1]), 0)
    onehot = jnp.where(ids == code, 1.0, 0.0).astype(BF16)
    y = _dot(t3_ref[...], onehot)
    out_ref[...] = y[0:NSA_HEADS] + y[NSA_HEADS:2 * NSA_HEADS] + y[2 * NSA_HEADS:3 * NSA_HEADS]


def _bias_table(table3, code):
    n = code.shape[1]
    return pl.pallas_call(
        _bias_table_kernel,
        grid=(n // BIAS_CHUNK,),
        in_specs=[_const_spec(table3.shape), pl.BlockSpec((1, BIAS_CHUNK), lambda i: (0, i))],
        out_specs=pl.BlockSpec((NSA_HEADS, BIAS_CHUNK), lambda i: (0, i)),
        out_shape=jax.ShapeDtypeStruct((NSA_HEADS, n), F32),
        compiler_params=_compiler_params(("parallel",)),
        name="rel_bias_table",
    )(table3, code)


def _nsa_cmp_select_kernel(q_ref, gate_ref, kcmp_ref, vcmpt_ref, bias_ref, ovl_ref, o_ref, qa_ref):
    i = pl.program_id(2)
    kcmp = kcmp_ref[0, 0]
    vcmpt = vcmpt_ref[0, 0]

    scores = [_dot_nt(kcmp, q_ref[0, r]) for r in range(NSA_REP)]
    probs = []
    for r in range(NSA_REP):
        s = scores[r] + bias_ref[r, 0]
        m = jnp.max(s, axis=0, keepdims=True)
        m = jnp.where(m > 0.5 * MASKED, m, 0.0)
        e = jnp.exp2(s - m)
        den = jnp.sum(e, axis=0, keepdims=True)
        probs.append(e / jnp.where(den > 0.0, den, 1.0))
    outs = [_dot(vcmpt, p.astype(BF16)) for p in probs]

    ph, pm, plo = _split3(probs[0] + probs[1] + probs[2] + probs[3])
    ovl = ovl_ref[...]
    n_sel = 32
    imp = (_dot(ovl, ph) + _dot(ovl, pm) + _dot(ovl, plo))[0:n_sel]
    tok = i * TQC + lax.broadcasted_iota(jnp.int32, (n_sel, TQC), 1)
    blk = lax.broadcasted_iota(jnp.int32, (n_sel, TQC), 0)
    cur = tok // SEL_BLOCK
    forced = (blk == 0) | (blk == cur) | (blk == cur - 1)
    valid = blk * SEL_BLOCK <= tok
    score = jnp.where(forced, SCORE_BIG, imp)
    score = jnp.where(valid, score, -SCORE_BIG)
    groups = [slice(SUBLANES * v, SUBLANES * (v + 1)) for v in range(n_sel // SUBLANES)]
    ranks = [jnp.zeros((SUBLANES, TQC), F32) for _ in groups]
    for jp in range(n_sel):
        other = score[jp:jp + 1, :]
        for v, rows in enumerate(groups):
            mine = score[rows]
            if rows.start > jp:
                beats = other >= mine
            elif rows.stop - 1 <= jp:
                beats = other > mine
            else:
                later = lax.broadcasted_iota(jnp.int32, (SUBLANES, TQC), 0) > jp - rows.start
                beats = (other > mine) | ((other == mine) & later)
            ranks[v] = ranks[v] + jnp.where(beats, 1.0, 0.0)
    rank = jnp.concatenate(ranks, axis=0)
    blk_mask = jnp.where(rank < SEL_TOP_N, 0.0, MASKED)
    blk_mask = jnp.concatenate([blk_mask, jnp.zeros((HEAD_DIM - n_sel, TQC), F32)], axis=0).T.astype(BF16)
    for r in range(NSA_REP):
        qa_ref[0, r, :, 0:HEAD_DIM] = q_ref[0, r]
        qa_ref[0, r, :, HEAD_DIM:2 * HEAD_DIM] = blk_mask

    gate = gate_ref[0]
    for r in range(NSA_REP):
        o_ref[0, :, r * HEAD_DIM:(r + 1) * HEAD_DIM] = (outs[r] * gate[r:r + 1]).T


def _nsa_cmp_select(q, gates, kcmp, vcmpt, bcmp, ovl):
    B, H, S, Dh = q.shape
    G = NSA_GROUPS
    return pl.pallas_call(
        _nsa_cmp_select_kernel,
        grid=(G, B, S // TQC),
        in_specs=[
            pl.BlockSpec((1, NSA_REP, TQC, Dh), lambda g, b, i: (b, g, i, 0)),
            pl.BlockSpec((1, 16, TQC), lambda g, b, i: (b, g, i)),
            pl.BlockSpec((1, 1, N_CMP_PAD, Dh), lambda g, b, i: (b, g, 0, 0)),
            pl.BlockSpec((1, 1, Dh, N_CMP_PAD), lambda g, b, i: (b, g, 0, 0)),
            pl.BlockSpec((NSA_REP, 1, N_CMP_PAD, TQC), lambda g, b, i: (g, i, 0, 0)),
            _const_spec(ovl.shape),
        ],
        out_specs=[
            pl.BlockSpec((1, TQC, NSA_REP * Dh), lambda g, b, i: (b, i, g)),
            pl.BlockSpec((1, NSA_REP, TQC, 2 * Dh), lambda g, b, i: (b, g, i, 0)),
        ],
        out_shape=[
            jax.ShapeDtypeStruct((B, S, H * Dh), F32),
            jax.ShapeDtypeStruct((B, H, S, 2 * Dh), BF16),
        ],
        compiler_params=_compiler_params(("parallel", "parallel", "arbitrary")),
        name="nsa_cmp_select",
    )(q, gates, kcmp, vcmpt, bcmp, ovl)


def _nsa_attn_kernel(*refs):
    for inner in range(ATTN_TILES_PER_STEP):
        _nsa_attn_tile(inner, *refs)


def _nsa_attn_tile(inner, q_ref, gate_ref, k_ref, vt_ref, ocmp_ref, bias_ref,
                   o_ref, s0_ref, s1_ref, x0_ref, x1_ref, p0_ref, p1_ref, a0_ref, a1_ref,
                   m_ref, acc_ref):
    i = pl.program_id(2) * ATTN_TILES_PER_STEP + inner
    last_step = pl.num_programs(2) * ATTN_TILES_PER_STEP - 1
    rows = slice(inner * TQA, (inner + 1) * TQA)
    s_bufs, x_bufs = (s0_ref, s1_ref), (x0_ref, x1_ref)
    p_bufs, a_bufs = (p0_ref, p1_ref), (a0_ref, a1_ref)
    masked_variant = N_SEL_VARIANTS + N_WIN_TILES

    def n_tasks_of(step):
        n_sel = step + 1
        return n_sel, n_sel + jnp.minimum(N_WIN_TILES, n_sel)

    def task(t, step):
        n_sel, n_tasks = n_tasks_of(step)
        is_win = t >= n_sel
        c = jnp.clip(jnp.where(is_win, step - (t - n_sel), t), 0, step)
        behind = step - c
        variant = jnp.where(is_win, N_SEL_VARIANTS + behind, jnp.minimum(behind, N_SEL_VARIANTS - 1))
        variant = jnp.where(t < n_tasks, variant, masked_variant)
        return is_win.astype(jnp.int32), c, variant

    def score_pair(u, step):
        pair = [task(2 * u, step), task(2 * u + 1, step)]
        q0 = pl.multiple_of(step * TQA, TQA)
        k = jnp.concatenate([k_ref[0, 0, br, pl.ds(pl.multiple_of(c * TKA, TKA), TKA), :] for br, c, _ in pair],
                            axis=0)
        for r in range(NSA_REP):
            s2 = _dot_nt(k, q_ref[0, r, pl.ds(q0, TQA), :])
            for slot, (_, _, variant) in enumerate(pair):
                s = s2[slot * TKA:(slot + 1) * TKA] + bias_ref[r, variant]
                s_bufs[slot][r] = s
                x_bufs[slot][r] = jnp.max(s, axis=0, keepdims=True)

    def softmax_stage(t, slot):
        br, _, _ = task(t, i)
        for r in range(NSA_REP):
            m_old = m_ref[br, r]
            m_new = jnp.maximum(m_old, x_bufs[slot][r])
            m_ref[br, r] = m_new
            a_bufs[slot][r] = jnp.exp2(m_old - m_new)
            p_bufs[slot][r] = jnp.exp2(s_bufs[slot][r] - m_new).astype(BF16)

    ones_rows = jnp.where(lax.broadcasted_iota(jnp.int32, (ACC_ROWS - HEAD_DIM, TKA), 0) == 0, 1.0, 0.0).astype(BF16)

    def value_stage(t, slot):
        br, c, _ = task(t, i)
        blocks = TKA // VT_BLOCK
        vt = jnp.concatenate([vt_ref[0, 0, br, blocks * c + u] for u in range(blocks)], axis=1)
        vt = jnp.concatenate([vt, ones_rows], axis=0)
        for r in range(NSA_REP):
            acc_ref[br, r] = a_bufs[slot][r] * acc_ref[br, r] + _dot(vt, p_bufs[slot][r])

    m_ref[...] = jnp.full(m_ref.shape, MASKED, F32)
    acc_ref[...] = jnp.zeros(acc_ref.shape, F32)

    if inner == 0:
        @pl.when(i == 0)
        def _():
            score_pair(0, i)

    n_trips = (n_tasks_of(i)[1] + 1) // 2

    def softmax_and_next_scores(u):
        softmax_stage(2 * u, 0)
        softmax_stage(2 * u + 1, 1)
        in_tile = u + 1 < n_trips
        score_pair(jnp.where(in_tile, u + 1, 0), jnp.where(in_tile, i, jnp.minimum(i + 1, last_step)))

    def trip(u, carry):
        value_stage(2 * u - 2, 0)
        value_stage(2 * u - 1, 1)
        softmax_and_next_scores(u)
        return carry

    softmax_and_next_scores(0)
    lax.fori_loop(1, n_trips, trip, 0)
    value_stage(2 * n_trips - 2, 0)
    value_stage(2 * n_trips - 1, 1)

    gate = gate_ref[0, :, rows]
    for r in range(NSA_REP):
        w_sel = gate[4 + r:5 + r] / acc_ref[0, r, HEAD_DIM:HEAD_DIM + 1, :]
        w_win = gate[8 + r:9 + r] / acc_ref[1, r, HEAD_DIM:HEAD_DIM + 1, :]
        o_t = w_sel * acc_ref[0, r, 0:HEAD_DIM, :] + w_win * acc_ref[1, r, 0:HEAD_DIM, :]
        cols = slice(r * HEAD_DIM, (r + 1) * HEAD_DIM)
        o_ref[0, rows, cols] = (ocmp_ref[0, rows, cols] + o_t.T).astype(BF16)


def _nsa_attn(q_aug, gates, k_sw, vt_sw, o_cmp, bias):
    B, H, S, Dq = q_aug.shape
    Dh = HEAD_DIM
    G = NSA_GROUPS
    n_vt = S // VT_BLOCK
    n_var = bias.shape[1]
    step_rows = ATTN_TILES_PER_STEP * TQA
    return pl.pallas_call(
        _nsa_attn_kernel,
        grid=(G, B, S // step_rows),
        in_specs=[
            pl.BlockSpec((1, NSA_REP, S, Dq), lambda g, b, i: (b, g, 0, 0)),
            pl.BlockSpec((1, 16, step_rows), lambda g, b, i: (b, g, i)),
            pl.BlockSpec((1, 1, 2, S, Dq), lambda g, b, i: (b, g, 0, 0, 0)),
            pl.BlockSpec((1, 1, 2, n_vt, Dh, VT_BLOCK), lambda g, b, i: (b, g, 0, 0, 0, 0)),
            pl.BlockSpec((1, step_rows, NSA_REP * Dh), lambda g, b, i: (b, i, g)),
            pl.BlockSpec((NSA_REP, n_var, TKA, TQA), lambda g, b, i: (g, 0, 0, 0), pipeline_mode=pl.Buffered(1)),
        ],
        out_specs=pl.BlockSpec((1, step_rows, NSA_REP * Dh), lambda g, b, i: (b, i, g)),
        out_shape=jax.ShapeDtypeStruct((B, S, H * Dh), BF16),
        scratch_shapes=[
            pltpu.VMEM((NSA_REP, TKA, TQA), F32),
            pltpu.VMEM((NSA_REP, TKA, TQA), F32),
            pltpu.VMEM((NSA_REP, 1, TQA), F32),
            pltpu.VMEM((NSA_REP, 1, TQA), F32),
            pltpu.VMEM((NSA_REP, TKA, TQA), BF16),
            pltpu.VMEM((NSA_REP, TKA, TQA), BF16),
            pltpu.VMEM((NSA_REP, 1, TQA), F32),
            pltpu.VMEM((NSA_REP, 1, TQA), F32),
            pltpu.VMEM((2, NSA_REP, 1, TQA), F32),
            pltpu.VMEM((2, NSA_REP, ACC_ROWS, TQA), F32),
        ],
        compiler_params=_compiler_params(("parallel", "parallel", "arbitrary")),
        name="nsa_attn",
    )(q_aug, gates, k_sw, vt_sw, o_cmp, bias)


MLP_CHUNK = 1024


def _mix_out_mlp_kernel(x_ref, a_ref, wo_ref, nw_ref, wup_ref, wdn_ref, o_ref):
    x1 = x_ref[...] + _dot(a_ref[...], wo_ref[...])
    h = _rms_rows(x1, nw_ref[...]).astype(BF16)

    acc = None
    for c in range(wup_ref.shape[1] // MLP_CHUNK):
        hid = slice(c * MLP_CHUNK, (c + 1) * MLP_CHUNK)
        u = jnp.maximum(_dot(h, wup_ref[:, hid]), 0.0)
        d = _dot((u * u).astype(BF16), wdn_ref[hid, :])
        acc = d if acc is None else acc + d
    o_ref[...] = x1 + acc


def _mix_out_mlp(x, a, wo, nw, wup, wdn, tm):
    T, D = x.shape
    K = a.shape[1]
    return pl.pallas_call(
        _mix_out_mlp_kernel,
        grid=(T // tm,),
        in_specs=[
            pl.BlockSpec((tm, D), lambda i: (i, 0)),
            pl.BlockSpec((tm, K), lambda i: (i, 0)),
            _const_spec(wo.shape), _const_spec(nw.shape), _const_spec(wup.shape), _const_spec(wdn.shape),
        ],
        out_specs=pl.BlockSpec((tm, D), lambda i: (i, 0)),
        out_shape=jax.ShapeDtypeStruct((T, D), F32),
        compiler_params=_compiler_params(("parallel",)),
        name="mix_out_mlp",
    )(x, a, wo, nw, wup, wdn)


SSD_NCHUNK = 512


def _ssd_proj_kernel(x_ref, nw_ref, wz_ref, wx_ref, wdt_ref, cw_ref, cb_ref, dtb_ref,
                     z_ref, xbc_ref, dt_ref, buf_ref):
    i = pl.program_id(1)
    tm = x_ref.shape[1]
    @pl.when(i == 0)
    def _():
        buf_ref[0:SUBLANES, :] = jnp.zeros((SUBLANES, buf_ref.shape[1]), F32)

    @pl.when(i > 0)
    def _():
        buf_ref[0:SUBLANES, :] = buf_ref[tm:tm + SUBLANES, :]

    h = _rms_rows(x_ref[0], nw_ref[...]).astype(BF16)

    for c in range(wz_ref.shape[0]):
        z_ref[0, :, c * SSD_NCHUNK:(c + 1) * SSD_NCHUNK] = _dot(h, wz_ref[c])

    for c in range(wx_ref.shape[0]):
        cols = slice(c * SSD_NCHUNK, (c + 1) * SSD_NCHUNK)
        buf_ref[SUBLANES:SUBLANES + tm, cols] = _dot(h, wx_ref[c])
        acc = cb_ref[:, cols] + buf_ref[SUBLANES:SUBLANES + tm, cols] * cw_ref[SSD_CONV - 1:SSD_CONV, cols]
        for k in range(SSD_CONV - 1):
            off = SUBLANES - (SSD_CONV - 1) + k
            acc += buf_ref[off:off + tm, cols] * cw_ref[k:k + 1, cols]
        xbc_ref[0, :, cols] = _silu(acc)

    dt_ref[0] = jax.nn.softplus(_dot(h, wdt_ref[...]) + dtb_ref[...])


def _ssd_proj(x, nw, wz, wx, wdt, cw, cb, dtb, tm):
    B, S, D = x.shape
    consts = (nw, wz, wx, wdt, cw, cb, dtb)
    return pl.pallas_call(
        _ssd_proj_kernel,
        grid=(B, S // tm),
        in_specs=[pl.BlockSpec((1, tm, D), lambda b, i: (b, i, 0))] + [_const_spec(c.shape) for c in consts],
        out_specs=[
            pl.BlockSpec((1, tm, SSD_D_INNER), lambda b, i: (b, i, 0)),
            pl.BlockSpec((1, tm, SSD_CONV_DIM), lambda b, i: (b, i, 0)),
            pl.BlockSpec((1, tm, LANES), lambda b, i: (b, i, 0)),
        ],
        out_shape=[
            jax.ShapeDtypeStruct((B, S, SSD_D_INNER), F32),
            jax.ShapeDtypeStruct((B, S, SSD_CONV_DIM), F32),
            jax.ShapeDtypeStruct((B, S, LANES), F32),
        ],
        scratch_shapes=[pltpu.VMEM((tm + SUBLANES, SSD_CONV_DIM), F32)],
        compiler_params=_compiler_params(("parallel", "arbitrary")),
        name="ssd_proj",
    )(x, *consts)


SCAN_CHUNKS_PER_STEP = 4


def _ssd_scan_kernel(*refs):
    for inner in range(SCAN_CHUNKS_PER_STEP):
        _ssd_scan_chunk(inner, *refs)


def _ssd_scan_chunk(inner, xbc_ref, dt_ref, z_ref, alog_ref, dskip_ref, nw_ref, tril_ref, hexp_ref,
                    y_ref, state_ref, acs_ref, acst_ref, dtt_ref, e3_ref, wt_ref):
    L = SSD_CHUNK
    P = SSD_HEAD_DIM
    N = SSD_STATE
    hpg = SSD_HEADS // SSD_GROUPS
    gw = hpg * P
    chunk = pl.program_id(1) * SCAN_CHUNKS_PER_STEP + inner
    last_chunk = pl.num_programs(1) * SCAN_CHUNKS_PER_STEP - 1
    rows = slice(inner * L, (inner + 1) * L)

    def chunk_decays(ci):
        dt = dt_ref[0, pl.ds(pl.multiple_of(ci * L, L), L), :]
        a = dt * (-jnp.exp(alog_ref[...]))
        tril = tril_ref[...]
        ah, am, al = _split3(a)
        acs = _dot(tril, ah) + _dot(tril, am) + _dot(tril, al)
        acs_t = acs.T
        dt_t = dt.T
        acs_ref[...] = acs
        acst_ref[...] = acs_t
        dtt_ref[...] = dt_t
        for n, term in enumerate(_split3(jnp.exp(acs))):
            e3_ref[n] = term
        wt_ref[...] = jnp.exp(acs_t[:, L - 1:L] - acs_t) * dt_t

    if inner == 0:
        @pl.when(chunk == 0)
        def _():
            state_ref[...] = jnp.zeros(state_ref.shape, F32)
            chunk_decays(0)

    acs, acs_t, dt_t, w_t = acs_ref[...], acst_ref[...], dtt_ref[...], wt_ref[...]
    eh, em, el = e3_ref[0], e3_ref[1], e3_ref[2]
    causal = (lax.broadcasted_iota(jnp.int32, (L, L), 0) >= lax.broadcasted_iota(jnp.int32, (L, L), 1))
    head_of_lane = lax.broadcasted_iota(jnp.int32, (L, gw), 1) // P

    for g in range(SSD_GROUPS):
        b_off = SSD_D_INNER + g * N
        c_off = SSD_D_INNER + SSD_GROUPS * N + g * N
        bm = xbc_ref[0, rows, b_off:b_off + N]
        cm = xbc_ref[0, rows, c_off:c_off + N].astype(BF16)
        cb = _dot_nt(cm, bm.astype(BF16))
        bm_t = bm.T
        xg = xbc_ref[0, rows, g * gw:(g + 1) * gw]
        hexp = hexp_ref[g]
        e_exp = _dot(eh, hexp) + _dot(em, hexp) + _dot(el, hexp)
        st = state_ref[g]
        y = _dot(cm, st.astype(BF16)) * e_exp
        st = st * e_exp[L - 1:L, :]
        for j in range(hpg):
            hd = g * hpg + j
            xj = jnp.where(head_of_lane == j, xg, 0.0).astype(BF16)
            diff = acs[:, hd:hd + 1] - acs_t[hd:hd + 1, :]
            decay = jnp.exp(jnp.where(causal, diff, -jnp.inf))
            y = y + _dot((cb * decay * dt_t[hd:hd + 1, :]).astype(BF16), xj)
            st = st + _dot((bm_t * w_t[hd:hd + 1, :]).astype(BF16), xj)
        state_ref[g] = st
        cols = slice(g * gw, (g + 1) * gw)
        zg = z_ref[0, rows, cols]
        y = (y + xg * dskip_ref[:, cols]) * _silu(zg)
        y_ref[0, rows, cols] = _rms_rows(y, nw_ref[:, cols]).astype(BF16)

    chunk_decays(jnp.minimum(chunk + 1, last_chunk))


def _ssd_scan(xbc, dt, z, alog, dskip, nw, tril, hexp):
    B, S, _ = xbc.shape
    L = SSD_CHUNK
    consts = (alog, dskip, nw, tril, hexp)
    step_rows = SCAN_CHUNKS_PER_STEP * L
    return pl.pallas_call(
        _ssd_scan_kernel,
        grid=(B, S // step_rows),
        in_specs=[
            pl.BlockSpec((1, step_rows, SSD_CONV_DIM), lambda b, c: (b, c, 0)),
            pl.BlockSpec((1, S, LANES), lambda b, c: (b, 0, 0)),
            pl.BlockSpec((1, step_rows, SSD_D_INNER), lambda b, c: (b, c, 0)),
        ] + [_const_spec(c.shape) for c in consts],
        out_specs=pl.BlockSpec((1, step_rows, SSD_D_INNER), lambda b, c: (b, c, 0)),
        out_shape=jax.ShapeDtypeStruct((B, S, SSD_D_INNER), BF16),
        scratch_shapes=[
            pltpu.VMEM((SSD_GROUPS, SSD_STATE, SSD_D_INNER // SSD_GROUPS), F32),
            pltpu.VMEM((L, LANES), F32),
            pltpu.VMEM((LANES, L), F32),
            pltpu.VMEM((LANES, L), F32),
            pltpu.VMEM((3, L, LANES), BF16),
            pltpu.VMEM((LANES, L), F32),
        ],
        compiler_params=_compiler_params(("parallel", "arbitrary")),
        name="ssd_scan",
    )(xbc, dt, z, *consts)


def _rel_bucket_np(dist):
    max_exact = REL_BUCKETS // 2
    d = np.maximum(dist, 0)
    df = np.maximum(d, 1).astype(np.float32)
    large = max_exact + (np.log(df / np.float32(max_exact)) / np.float32(math.log(REL_MAX_DIST / max_exact))
                         * np.float32(REL_BUCKETS - max_exact)).astype(np.int32)
    large = np.minimum(large, REL_BUCKETS - 1)
    return np.where(d < max_exact, d, large).astype(np.int32)


@functools.lru_cache(maxsize=None)
def _position_tables(S):
    n_cmp = (S - CMP_BLOCK) // CMP_STRIDE + 1
    n_sel = S // SEL_BLOCK

    masked_code = REL_BUCKETS

    def code(dist, ok):
        return np.where(ok, _rel_bucket_np(dist), masked_code).astype(np.int32).reshape(1, -1)

    n = np.arange(N_CMP_PAD)[None, :, None]
    t = np.arange(S // TQC)[:, None, None] * TQC + np.arange(TQC)[None, None, :]
    dist = t - (n * CMP_STRIDE + CMP_BLOCK - 1)
    cmp_code = code(dist, (dist >= 0) & (n < n_cmp))
    assert TQA == TKA and (N_SEL_VARIANTS - 1) * TQA - (TKA - 1) >= REL_MAX_DIST
    ti = np.arange(TQA)[None, :]
    j = np.arange(TKA)[:, None]
    tile_dist = [TQA * behind + ti - j for behind in range(N_WIN_TILES)]
    far = np.full((TKA, TQA), 2 * REL_MAX_DIST)
    sel_dist = np.stack(tile_dist[:N_SEL_VARIANTS - 1] + [far])
    win_dist = np.stack(tile_dist)
    attn_code = np.concatenate([code(sel_dist, sel_dist >= 0),
                                code(win_dist, (win_dist >= 0) & (win_dist < WINDOW)),
                                np.full((1, TKA * TQA), masked_code, np.int32)],
                               axis=1)
    jb = np.arange(N_SEL_PAD)[:, None]
    nn = np.arange(N_CMP_PAD)[None, :]
    overlap = ((nn * CMP_STRIDE < jb * SEL_BLOCK + SEL_BLOCK) & (nn * CMP_STRIDE + CMP_BLOCK > jb * SEL_BLOCK)
               & (nn < n_cmp) & (jb < n_sel)).astype(np.float32)
    key_block = (np.arange(S)[:, None] // SEL_BLOCK == np.arange(HEAD_DIM)[None, :]).astype(np.float32)
    hpg = SSD_HEADS // SSD_GROUPS
    lane_head = np.arange(hpg * SSD_HEAD_DIM)[None, None, :] // SSD_HEAD_DIM
    head_expand = (np.arange(LANES)[None, :, None]
                   == np.arange(SSD_GROUPS)[:, None, None] * hpg + lane_head).astype(np.float32)
    return cmp_code, attn_code, overlap, key_block, head_expand


def kernel(x, norm_mix_w, norm_mlp_w, rel_table, nsa_w_in, nsa_q_gain, nsa_k_gain, cmp_pe_k, cmp_w1_k, cmp_w2_k, cmp_pe_v, cmp_w1_v, cmp_w2_v, nsa_w_out, ssd_w_in, ssd_conv_w, ssd_conv_b, ssd_dt_bias, ssd_a_log, ssd_d, ssd_norm_w, ssd_w_out, mlp_w_up, mlp_w_down):
    B, S, D = x.shape
    T = B * S
    G, R = NSA_GROUPS, NSA_REP
    cmp_code, attn_code, overlap, key_block, head_expand = _position_tables(S)

    w_in = nsa_w_in[0].astype(BF16)
    kv0 = NSA_Q_DIM
    kv_cols = lambda c: w_in[:, kv0 + c * NSA_KV_DIM:kv0 + (c + 1) * NSA_KV_DIM]
    wq = w_in[:, :NSA_Q_DIM]
    wk = jnp.stack([kv_cols(0), kv_cols(1), kv_cols(2), kv_cols(4)])
    wvt = jnp.stack([kv_cols(3).T, kv_cols(5).T])
    wg = w_in[:, kv0 + 6 * NSA_KV_DIM:].reshape(D, 3, G, R).transpose(0, 2, 1, 3).reshape(D, G, 3 * R)
    wgt = jnp.pad(wg, ((0, 0), (0, 0), (0, 16 - 3 * R))).reshape(D, G * 16).T
    qg = jnp.tile(nsa_q_gain[0], 4)[None, :]
    kg = jnp.tile(nsa_k_gain[0], 4)[None, :]
    seg = jnp.asarray(np.kron(np.eye(4, dtype=np.float32), np.ones((HEAD_DIM, HEAD_DIM), np.float32)), BF16)
    q, kc, vc, k_sw, vt_sw, gates = _nsa_proj(
        x, jnp.asarray(key_block, BF16), norm_mix_w[0][None, :], wq, wk, wvt, wgt, qg, kg, seg, tm=512)

    kcmp, vcmpt = _nsa_compress(
        kc, vc, cmp_pe_k[0], cmp_pe_v[0], cmp_w1_k[0].astype(BF16), cmp_w1_v[0].astype(BF16),
        cmp_w2_k[0].astype(BF16), cmp_w2_v[0].astype(BF16).T, nsa_k_gain[0][None, :])

    by_code = jnp.concatenate([rel_table.T * LOG2E, jnp.full((NSA_HEADS, 1), MASKED, F32),
                               jnp.zeros((NSA_HEADS, BIAS_CODES - REL_BUCKETS - 1), F32)], axis=1)
    table3 = jnp.concatenate(_split3(by_code), axis=0)
    bcmp = _bias_table(table3, jnp.asarray(cmp_code)).reshape(NSA_HEADS, S // TQC, N_CMP_PAD, TQC)
    battn = _bias_table(table3, jnp.asarray(attn_code)).reshape(NSA_HEADS, -1, TKA, TQA)
    o_cmp, q_aug = _nsa_cmp_select(q, gates, kcmp, vcmpt, bcmp, jnp.asarray(overlap, BF16))
    o = _nsa_attn(q_aug, gates, k_sw, vt_sw, o_cmp, battn)

    def mlp_weights(li):
        return mlp_w_up[li].astype(BF16), mlp_w_down[li].astype(BF16)

    wup, wdn = mlp_weights(0)
    x2 = _mix_out_mlp(x.reshape(T, D), o.reshape(T, NSA_Q_DIM), nsa_w_out[0].astype(BF16),
                      norm_mlp_w[0][None, :], wup, wdn, tm=512)

    w_in = ssd_w_in[0]
    def col_chunks(w):
        return w.reshape(D, -1, SSD_NCHUNK).transpose(1, 0, 2).astype(BF16)

    wz = col_chunks(w_in[:, :SSD_D_INNER])
    wx = col_chunks(w_in[:, SSD_D_INNER:SSD_D_INNER + SSD_CONV_DIM])
    lane_pad = LANES - SSD_HEADS
    wdt = jnp.pad(w_in[:, SSD_D_INNER + SSD_CONV_DIM:], ((0, 0), (0, lane_pad))).astype(BF16)
    pad_heads = lambda v: jnp.pad(v, (0, lane_pad))[None, :]
    z, xbc, dt = _ssd_proj(x2.reshape(B, S, D), norm_mix_w[1][None, :], wz, wx, wdt,
                           ssd_conv_w[0], ssd_conv_b[0][None, :], pad_heads(ssd_dt_bias[0]), tm=512)
    tril = jnp.asarray(np.tril(np.ones((SSD_CHUNK, SSD_CHUNK), np.float32)), BF16)
    dskip = jnp.repeat(ssd_d[0], SSD_HEAD_DIM)[None, :]
    y = _ssd_scan(xbc, dt, z, pad_heads(ssd_a_log[0]), dskip, ssd_norm_w[0][None, :], tril,
                  jnp.asarray(head_expand, BF16))

    wup, wdn = mlp_weights(1)
    x4 = _mix_out_mlp(x2, y.reshape(T, SSD_D_INNER), ssd_w_out[0].astype(BF16),
                      norm_mlp_w[1][None, :], wup, wdn, tm=512)
    return x4.reshape(B, S, D)
```

```python
import functools
import math

import numpy as np
import jax
import jax.numpy as jnp
from jax import lax
from jax.experimental import pallas as pl
from jax.experimental.pallas import tpu as pltpu

F32 = jnp.float32
BF16 = jnp.bfloat16

D_MODEL = 1024
EPS = 1e-6

NSA_HEADS = 16
NSA_GROUPS = 4
NSA_REP = NSA_HEADS // NSA_GROUPS
HEAD_DIM = 64
NSA_Q_DIM = NSA_HEADS * HEAD_DIM
NSA_KV_DIM = NSA_GROUPS * HEAD_DIM
CMP_BLOCK = 32
CMP_STRIDE = 16
CMP_HIDDEN = 256
SEL_BLOCK = 64
SEL_TOP_N = 16
WINDOW = 512
REL_BUCKETS = 32
REL_MAX_DIST = 128

SSD_D_INNER = 2 * D_MODEL
SSD_HEAD_DIM = 64
SSD_HEADS = SSD_D_INNER // SSD_HEAD_DIM
SSD_GROUPS = 8
SSD_STATE = 128
SSD_CONV = 4
SSD_CHUNK = 128
SSD_CONV_DIM = SSD_D_INNER + 2 * SSD_GROUPS * SSD_STATE
MLP_HIDDEN = 4 * D_MODEL

LANES = 128
SUBLANES = 8
VMEM_LIMIT_BYTES = 56 * 1024 * 1024

VT_BLOCK = 128
TQC = 512
CMP_TILES_PER_STEP = 4
TQA = 256
TKA = 256
ATTN_TILES_PER_STEP = 8
N_CMP_PAD = 128
N_SEL_PAD = 128
MASKED = -1e30
LOG2E = math.log2(math.e)
SCORE_BIG = 3e38
N_SEL_VARIANTS = 3
N_WIN_TILES = WINDOW // TKA + 1
ACC_ROWS = HEAD_DIM + 16

NT_DIMS = (((1,), (1,)), ((), ()))


def _dot(a, b):
    return jnp.dot(a, b, preferred_element_type=F32)


def _dot_nt(a, b):
    return lax.dot_general(a, b, NT_DIMS, preferred_element_type=F32)


def _split3(x):
    hi = x.astype(BF16)
    r1 = x - hi.astype(F32)
    mid = r1.astype(BF16)
    lo = (r1 - mid.astype(F32)).astype(BF16)
    return hi, mid, lo


def _rms_rows(x, w):
    return x * lax.rsqrt(jnp.mean(x * x, axis=-1, keepdims=True) + EPS) * w


def _silu(x):
    h = 0.5 * x
    return h + h * jnp.tanh(h)


def _const_spec(shape):
    nd = len(shape)
    return pl.BlockSpec(shape, lambda *_: (0,) * nd, pipeline_mode=pl.Buffered(1))


def _compiler_params(semantics):
    return pltpu.CompilerParams(dimension_semantics=semantics, vmem_limit_bytes=VMEM_LIMIT_BYTES)


def _nsa_proj_kernel(x_ref, blk_ref, nw_ref, wq_ref, wk_ref, wvt_ref, wgt_ref, qg_ref, kg_ref, seg_ref,
                     q_ref, kc_ref, vc_ref, k_ref, vt_ref, gate_ref):
    tm = x_ref.shape[1]
    h = _rms_rows(x_ref[0], nw_ref[...]).astype(BF16)
    seg = seg_ref[...]

    def head_rms(y, gain):
        sq = y * y
        hi = sq.astype(BF16)
        lo = (sq - hi.astype(F32)).astype(BF16)
        ss = _dot(hi, seg) + _dot(lo, seg)
        return y * lax.rsqrt(ss * (1.0 / HEAD_DIM) + EPS) * gain

    def emit_q(c, y):
        y = (head_rms(y, qg_ref[...]) * (HEAD_DIM ** -0.5 * LOG2E)).astype(BF16)
        for r in range(4):
            q_ref[0, 4 * c + r] = y[:, 64 * r:64 * r + 64]

    def emit_raw(c, y):
        raw_ref = (kc_ref, vc_ref)[c]
        for g in range(NSA_GROUPS):
            raw_ref[0, g] = y[:, 64 * g:64 * g + 64]

    def emit_k(br, y):
        y = head_rms(y, kg_ref[...]).astype(BF16)
        tag = blk_ref[...] if br == 0 else jnp.zeros(blk_ref.shape, BF16)
        for g in range(NSA_GROUPS):
            k_ref[0, g, br, :, 0:HEAD_DIM] = y[:, 64 * g:64 * g + 64]
            k_ref[0, g, br, :, HEAD_DIM:2 * HEAD_DIM] = tag

    def emit_vt(br, yt):
        yt = yt.astype(BF16)
        for g in range(NSA_GROUPS):
            for kb in range(tm // VT_BLOCK):
                vt_ref[0, g, br, kb] = yt[64 * g:64 * g + 64, VT_BLOCK * kb:VT_BLOCK * (kb + 1)]

    def emit_gates(_, yt):
        gate_ref[0] = jax.nn.sigmoid(yt)

    stages = ([(lambda c=c: _dot(h, wq_ref[:, 256 * c:256 * (c + 1)]), emit_q, c) for c in range(NSA_Q_DIM // 256)]
              + [(lambda c=c: _dot(h, wk_ref[c]), emit_raw, c) for c in range(2)]
              + [(lambda br=br: _dot(h, wk_ref[2 + br]), emit_k, br) for br in range(2)]
              + [(lambda br=br: _dot_nt(wvt_ref[br], h), emit_vt, br) for br in range(2)]
              + [(lambda: _dot_nt(wgt_ref[...], h), emit_gates, 0)])
    pending = stages[0][0]()
    for n, (_, emit, arg) in enumerate(stages):
        y = pending
        if n + 1 < len(stages):
            pending = stages[n + 1][0]()
        emit(arg, y)


def _nsa_proj(x, key_block, nw, wq, wk, wvt, wgt, qg, kg, seg, tm):
    B, S, D = x.shape
    G = NSA_GROUPS
    raw_block = pl.BlockSpec((1, G, tm, HEAD_DIM), lambda b, i: (b, 0, i, 0))
    raw_shape = jax.ShapeDtypeStruct((B, G, S, HEAD_DIM), F32)
    consts = (nw, wq, wk, wvt, wgt, qg, kg, seg)
    return pl.pallas_call(
        _nsa_proj_kernel,
        grid=(B, S // tm),
        in_specs=[pl.BlockSpec((1, tm, D), lambda b, i: (b, i, 0)),
                  pl.BlockSpec((tm, HEAD_DIM), lambda b, i: (i, 0))] + [_const_spec(c.shape) for c in consts],
        out_specs=[
            pl.BlockSpec((1, NSA_HEADS, tm, HEAD_DIM), lambda b, i: (b, 0, i, 0)),
            raw_block, raw_block,
            pl.BlockSpec((1, G, 2, tm, 2 * HEAD_DIM), lambda b, i: (b, 0, 0, i, 0)),
            pl.BlockSpec((1, G, 2, tm // VT_BLOCK, HEAD_DIM, VT_BLOCK), lambda b, i: (b, 0, 0, i, 0, 0)),
            pl.BlockSpec((1, G * 16, tm), lambda b, i: (b, 0, i)),
        ],
        out_shape=[
            jax.ShapeDtypeStruct((B, NSA_HEADS, S, HEAD_DIM), BF16),
            raw_shape, raw_shape,
            jax.ShapeDtypeStruct((B, G, 2, S, 2 * HEAD_DIM), BF16),
            jax.ShapeDtypeStruct((B, G, 2, S // VT_BLOCK, HEAD_DIM, VT_BLOCK), BF16),
            jax.ShapeDtypeStruct((B, G * 16, S), F32),
        ],
        compiler_params=_compiler_params(("parallel", "parallel")),
        name="nsa_proj",
    )(x, key_block, *consts)


def _nsa_compress_kernel(kc_ref, vc_ref, pek_ref, pev_ref, w1k_ref, w1v_ref, w2k_ref, w2vt_ref, kg_ref,
                         kcmp_ref, vcmp_ref):
    half = CMP_BLOCK // 2

    def hidden(src_ref, pe_ref, w1_ref):
        acc_a = jnp.zeros((N_CMP_PAD, CMP_HIDDEN), F32)
        acc_b = jnp.zeros((N_CMP_PAD, CMP_HIDDEN), F32)
        for l in range(half):
            rows = src_ref[0, 0, pl.ds(l, N_CMP_PAD, stride=CMP_STRIDE), :]
            acc_a += _dot((rows + pe_ref[l:l + 1, :]).astype(BF16), w1_ref[l])
            acc_b += _dot((rows + pe_ref[half + l:half + l + 1, :]).astype(BF16), w1_ref[half + l])
        hid = acc_a + pltpu.roll(acc_b, N_CMP_PAD - 1, axis=0)
        return _silu(hid).astype(BF16)

    k = _dot(hidden(kc_ref, pek_ref, w1k_ref), w2k_ref[...])
    kcmp_ref[0, 0] = _rms_rows(k, kg_ref[...]).astype(BF16)
    vcmp_ref[0, 0] = _dot_nt(w2vt_ref[...], hidden(vc_ref, pev_ref, w1v_ref)).astype(BF16)


def _nsa_compress(kc, vc, pek, pev, w1k, w1v, w2k, w2vt, kg):
    B, G, S, Dh = kc.shape
    src = pl.BlockSpec((1, 1, S, Dh), lambda b, g: (b, g, 0, 0))
    consts = (pek, pev, w1k, w1v, w2k, w2vt, kg)
    return pl.pallas_call(
        _nsa_compress_kernel,
        grid=(B, G),
        in_specs=[src, src] + [_const_spec(c.shape) for c in consts],
        out_specs=[pl.BlockSpec((1, 1, N_CMP_PAD, Dh), lambda b, g: (b, g, 0, 0)),
                   pl.BlockSpec((1, 1, Dh, N_CMP_PAD), lambda b, g: (b, g, 0, 0))],
        out_shape=[jax.ShapeDtypeStruct((B, G, N_CMP_PAD, Dh), BF16),
                   jax.ShapeDtypeStruct((B, G, Dh, N_CMP_PAD), BF16)],
        compiler_params=_compiler_params(("parallel", "parallel")),
        name="nsa_compress",
    )(kc, vc, *consts)


BIAS_CHUNK = 8192
BIAS_CODES = 64


def _bias_table_kernel(t3_ref, code_ref, out_ref):
    code = code_ref[...]
    ids = lax.broadcasted_iota(jnp.int32, (BIAS_CODES, code.shape[1]), 0)
    onehot = jnp.where(ids == code, 1.0, 0.0).astype(BF16)
    y = _dot(t3_ref[...], onehot)
    out_ref[...] = y[0:NSA_HEADS] + y[NSA_HEADS:2 * NSA_HEADS] + y[2 * NSA_HEADS:3 * NSA_HEADS]


def _bias_table(table3, code):
    n = code.shape[1]
    return pl.pallas_call(
        _bias_table_kernel,
        grid=(n // BIAS_CHUNK,),
        in_specs=[_const_spec(table3.shape), pl.BlockSpec((1, BIAS_CHUNK), lambda i: (0, i))],
        out_specs=pl.BlockSpec((NSA_HEADS, BIAS_CHUNK), lambda i: (0, i)),
        out_shape=jax.ShapeDtypeStruct((NSA_HEADS, n), F32),
        compiler_params=_compiler_params(("parallel",)),
        name="rel_bias_table",
    )(table3, code)


def _nsa_cmp_select_kernel(*refs):
    for inner in range(CMP_TILES_PER_STEP):
        _nsa_cmp_select_tile(inner, *refs)


def _nsa_cmp_select_tile(inner, q_ref, gate_ref, kcmp_ref, vcmpt_ref, bias_ref, ovl_ref, o_ref, qa_ref):
    i = pl.program_id(2) * CMP_TILES_PER_STEP + inner
    rows = slice(inner * TQC, (inner + 1) * TQC)
    kcmp = kcmp_ref[0, 0]
    vcmpt = vcmpt_ref[0, 0]

    scores = [_dot_nt(kcmp, q_ref[0, r, rows]) for r in range(NSA_REP)]
    probs = []
    for r in range(NSA_REP):
        s = scores[r] + bias_ref[r, inner]
        m = jnp.max(s, axis=0, keepdims=True)
        m = jnp.where(m > 0.5 * MASKED, m, 0.0)
        e = jnp.exp2(s - m)
        den = jnp.sum(e, axis=0, keepdims=True)
        probs.append(e / jnp.where(den > 0.0, den, 1.0))
    outs = [_dot(vcmpt, p.astype(BF16)) for p in probs]

    ph, pm, plo = _split3(probs[0] + probs[1] + probs[2] + probs[3])
    ovl = ovl_ref[...]
    n_sel = 32
    imp = (_dot(ovl, ph) + _dot(ovl, pm) + _dot(ovl, plo))[0:n_sel]
    tok = i * TQC + lax.broadcasted_iota(jnp.int32, (n_sel, TQC), 1)
    blk = lax.broadcasted_iota(jnp.int32, (n_sel, TQC), 0)
    cur = tok // SEL_BLOCK
    forced = (blk == 0) | (blk == cur) | (blk == cur - 1)
    valid = blk * SEL_BLOCK <= tok
    score = jnp.where(forced, SCORE_BIG, imp)
    score = jnp.where(valid, score, -SCORE_BIG)
    groups = [slice(SUBLANES * v, SUBLANES * (v + 1)) for v in range(n_sel // SUBLANES)]
    ranks = [jnp.zeros((SUBLANES, TQC), F32) for _ in groups]
    for jp in range(n_sel):
        other = score[jp:jp + 1, :]
        for v, grp in enumerate(groups):
            mine = score[grp]
            if grp.start > jp:
                beats = other >= mine
            elif grp.stop - 1 <= jp:
                beats = other > mine
            else:
                later = lax.broadcasted_iota(jnp.int32, (SUBLANES, TQC), 0) > jp - grp.start
                beats = (other > mine) | ((other == mine) & later)
            ranks[v] = ranks[v] + jnp.where(beats, 1.0, 0.0)
    rank = jnp.concatenate(ranks, axis=0)
    blk_mask = jnp.where(rank < SEL_TOP_N, 0.0, MASKED)
    blk_mask = jnp.concatenate([blk_mask, jnp.zeros((HEAD_DIM - n_sel, TQC), F32)], axis=0).T.astype(BF16)
    for r in range(NSA_REP):
        qa_ref[0, r, rows, 0:HEAD_DIM] = q_ref[0, r, rows]
        qa_ref[0, r, rows, HEAD_DIM:2 * HEAD_DIM] = blk_mask

    gate = gate_ref[0, :, rows]
    for r in range(NSA_REP):
        o_ref[0, rows, r * HEAD_DIM:(r + 1) * HEAD_DIM] = (outs[r] * gate[r:r + 1]).T


def _nsa_cmp_select(q, gates, kcmp, vcmpt, bcmp, ovl):
    B, H, S, Dh = q.shape
    G = NSA_GROUPS
    step_rows = CMP_TILES_PER_STEP * TQC
    return pl.pallas_call(
        _nsa_cmp_select_kernel,
        grid=(G, B, S // step_rows),
        in_specs=[
            pl.BlockSpec((1, NSA_REP, step_rows, Dh), lambda g, b, i: (b, g, i, 0)),
            pl.BlockSpec((1, 16, step_rows), lambda g, b, i: (b, g, i)),
            pl.BlockSpec((1, 1, N_CMP_PAD, Dh), lambda g, b, i: (b, g, 0, 0)),
            pl.BlockSpec((1, 1, Dh, N_CMP_PAD), lambda g, b, i: (b, g, 0, 0)),
            pl.BlockSpec((NSA_REP, CMP_TILES_PER_STEP, N_CMP_PAD, TQC), lambda g, b, i: (g, i, 0, 0)),
            _const_spec(ovl.shape),
        ],
        out_specs=[
            pl.BlockSpec((1, step_rows, NSA_REP * Dh), lambda g, b, i: (b, i, g)),
            pl.BlockSpec((1, NSA_REP, step_rows, 2 * Dh), lambda g, b, i: (b, g, i, 0)),
        ],
        out_shape=[
            jax.ShapeDtypeStruct((B, S, H * Dh), F32),
            jax.ShapeDtypeStruct((B, H, S, 2 * Dh), BF16),
        ],
        compiler_params=_compiler_params(("parallel", "parallel", "arbitrary")),
        name="nsa_cmp_select",
    )(q, gates, kcmp, vcmpt, bcmp, ovl)


def _nsa_attn_kernel(*refs):
    for inner in range(ATTN_TILES_PER_STEP):
        _nsa_attn_tile(inner, *refs)


def _nsa_attn_tile(inner, q_ref, gate_ref, k_ref, vt_ref, ocmp_ref, bias_ref,
                   o_ref, s0_ref, s1_ref, x0_ref, x1_ref, p0_ref, p1_ref, a0_ref, a1_ref,
                   m_ref, acc_ref):
    tiles_per_row = q_ref.shape[2] // TQA
    last_step = tiles_per_row - 1
    if tiles_per_row == ATTN_TILES_PER_STEP:
        i = inner
    else:
        i = pl.program_id(2) * ATTN_TILES_PER_STEP + inner
    rows = slice(inner * TQA, (inner + 1) * TQA)
    s_bufs, x_bufs = (s0_ref, s1_ref), (x0_ref, x1_ref)
    p_bufs, a_bufs = (p0_ref, p1_ref), (a0_ref, a1_ref)
    masked_variant = N_SEL_VARIANTS + N_WIN_TILES

    def n_tasks_of(step):
        n_sel = step + 1
        n_win = min(N_WIN_TILES, n_sel) if isinstance(step, int) else jnp.minimum(N_WIN_TILES, n_sel)
        return n_sel, n_sel + n_win

    def task(t, step):
        t, step = jnp.asarray(t, jnp.int32), jnp.asarray(step, jnp.int32)
        n_sel, n_tasks = n_tasks_of(step)
        is_win = t >= n_sel
        c = jnp.clip(jnp.where(is_win, step - (t - n_sel), t), 0, step)
        behind = step - c
        variant = jnp.where(is_win, N_SEL_VARIANTS + behind, jnp.minimum(behind, N_SEL_VARIANTS - 1))
        variant = jnp.where(t < n_tasks, variant, masked_variant)
        return is_win.astype(jnp.int32), c, variant

    def score_pair(u, step):
        pair = [task(2 * u, step), task(2 * u + 1, step)]
        q0 = pl.multiple_of(jnp.asarray(step, jnp.int32) * TQA, TQA)
        k = jnp.concatenate([k_ref[0, 0, br, pl.ds(pl.multiple_of(c * TKA, TKA), TKA), :] for br, c, _ in pair],
                            axis=0)
        for r in range(NSA_REP):
            s2 = _dot_nt(k, q_ref[0, r, pl.ds(q0, TQA), :])
            for slot, (_, _, variant) in enumerate(pair):
                s = s2[slot * TKA:(slot + 1) * TKA] + bias_ref[r, variant]
                s_bufs[slot][r] = s
                x_bufs[slot][r] = jnp.max(s, axis=0, keepdims=True)

    def softmax_stage(t, slot):
        br, _, _ = task(t, i)
        for r in range(NSA_REP):
            m_old = m_ref[br, r]
            m_new = jnp.maximum(m_old, x_bufs[slot][r])
            m_ref[br, r] = m_new
            a_bufs[slot][r] = jnp.exp2(m_old - m_new)
            p_bufs[slot][r] = jnp.exp2(s_bufs[slot][r] - m_new).astype(BF16)

    ones_rows = jnp.where(lax.broadcasted_iota(jnp.int32, (ACC_ROWS - HEAD_DIM, TKA), 0) == 0, 1.0, 0.0).astype(BF16)

    def value_stage(t, slot):
        br, c, _ = task(t, i)
        blocks = TKA // VT_BLOCK
        vt = jnp.concatenate([vt_ref[0, 0, br, blocks * c + u] for u in range(blocks)], axis=1)
        vt = jnp.concatenate([vt, ones_rows], axis=0)
        for r in range(NSA_REP):
            acc_ref[br, r] = a_bufs[slot][r] * acc_ref[br, r] + _dot(vt, p_bufs[slot][r])

    m_ref[...] = jnp.full(m_ref.shape, MASKED, F32)
    acc_ref[...] = jnp.zeros(acc_ref.shape, F32)

    if isinstance(i, int):
        if i == 0:
            score_pair(0, 0)
    elif inner == 0:
        @pl.when(i == 0)
        def _():
            score_pair(0, i)

    n_trips = (n_tasks_of(i)[1] + 1) // 2

    def softmax_and_next_scores(u):
        softmax_stage(2 * u, 0)
        softmax_stage(2 * u + 1, 1)
        in_tile = u + 1 < n_trips
        score_pair(jnp.where(in_tile, u + 1, 0), jnp.where(in_tile, i, jnp.minimum(i + 1, last_step)))

    def trip(u, carry):
        value_stage(2 * u - 2, 0)
        value_stage(2 * u - 1, 1)
        softmax_and_next_scores(u)
        return carry

    softmax_and_next_scores(0)
    lax.fori_loop(1, n_trips, trip, 0)
    value_stage(2 * n_trips - 2, 0)
    value_stage(2 * n_trips - 1, 1)

    gate = gate_ref[0, :, rows]
    for r in range(NSA_REP):
        w_sel = gate[4 + r:5 + r] / acc_ref[0, r, HEAD_DIM:HEAD_DIM + 1, :]
        w_win = gate[8 + r:9 + r] / acc_ref[1, r, HEAD_DIM:HEAD_DIM + 1, :]
        o_t = w_sel * acc_ref[0, r, 0:HEAD_DIM, :] + w_win * acc_ref[1, r, 0:HEAD_DIM, :]
        cols = slice(r * HEAD_DIM, (r + 1) * HEAD_DIM)
        o_ref[0, rows, cols] = (ocmp_ref[0, rows, cols] + o_t.T).astype(BF16)


def _nsa_attn(q_aug, gates, k_sw, vt_sw, o_cmp, bias):
    B, H, S, Dq = q_aug.shape
    Dh = HEAD_DIM
    G = NSA_GROUPS
    n_vt = S // VT_BLOCK
    n_var = bias.shape[1]
    step_rows = ATTN_TILES_PER_STEP * TQA
    return pl.pallas_call(
        _nsa_attn_kernel,
        grid=(G, B, S // step_rows),
        in_specs=[
            pl.BlockSpec((1, NSA_REP, S, Dq), lambda g, b, i: (b, g, 0, 0)),
            pl.BlockSpec((1, 16, step_rows), lambda g, b, i: (b, g, i)),
            pl.BlockSpec((1, 1, 2, S, Dq), lambda g, b, i: (b, g, 0, 0, 0)),
            pl.BlockSpec((1, 1, 2, n_vt, Dh, VT_BLOCK), lambda g, b, i: (b, g, 0, 0, 0, 0)),
            pl.BlockSpec((1, step_rows, NSA_REP * Dh), lambda g, b, i: (b, i, g)),
            pl.BlockSpec((NSA_REP, n_var, TKA, TQA), lambda g, b, i: (g, 0, 0, 0), pipeline_mode=pl.Buffered(1)),
        ],
        out_specs=pl.BlockSpec((1, step_rows, NSA_REP * Dh), lambda g, b, i: (b, i, g)),
        out_shape=jax.ShapeDtypeStruct((B, S, H * Dh), BF16),
        scratch_shapes=[
            pltpu.VMEM((NSA_REP, TKA, TQA), F32),
            pltpu.VMEM((NSA_REP, TKA, TQA), F32),
            pltpu.VMEM((NSA_REP, 1, TQA), F32),
            pltpu.VMEM((NSA_REP, 1, TQA), F32),
            pltpu.VMEM((NSA_REP, TKA, TQA), BF16),
            pltpu.VMEM((NSA_REP, TKA, TQA), BF16),
            pltpu.VMEM((NSA_REP, 1, TQA), F32),
            pltpu.VMEM((NSA_REP, 1, TQA), F32),
            pltpu.VMEM((2, NSA_REP, 1, TQA), F32),
            pltpu.VMEM((2, NSA_REP, ACC_ROWS, TQA), F32),
        ],
        compiler_params=_compiler_params(("parallel", "parallel", "arbitrary")),
        name="nsa_attn",
    )(q_aug, gates, k_sw, vt_sw, o_cmp, bias)


MLP_CHUNK = 1024


def _mix_out_mlp_kernel(x_ref, a_ref, wo_ref, nw_ref, wup_ref, wdn_ref, o_ref):
    x1 = x_ref[...] + _dot(a_ref[...], wo_ref[...])
    h = _rms_rows(x1, nw_ref[...]).astype(BF16)

    acc = None
    for c in range(wup_ref.shape[1] // MLP_CHUNK):
        hid = slice(c * MLP_CHUNK, (c + 1) * MLP_CHUNK)
        u = jnp.maximum(_dot(h, wup_ref[:, hid]), 0.0)
        d = _dot((u * u).astype(BF16), wdn_ref[hid, :])
        acc = d if acc is None else acc + d
    o_ref[...] = x1 + acc


def _mix_out_mlp(x, a, wo, nw, wup, wdn, tm):
    T, D = x.shape
    K = a.shape[1]
    return pl.pallas_call(
        _mix_out_mlp_kernel,
        grid=(T // tm,),
        in_specs=[
            pl.BlockSpec((tm, D), lambda i: (i, 0)),
            pl.BlockSpec((tm, K), lambda i: (i, 0)),
            _const_spec(wo.shape), _const_spec(nw.shape), _const_spec(wup.shape), _const_spec(wdn.shape),
        ],
        out_specs=pl.BlockSpec((tm, D), lambda i: (i, 0)),
        out_shape=jax.ShapeDtypeStruct((T, D), F32),
        compiler_params=_compiler_params(("parallel",)),
        name="mix_out_mlp",
    )(x, a, wo, nw, wup, wdn)


SSD_NCHUNK = 512


def _ssd_proj_kernel(x_ref, nw_ref, wz_ref, wx_ref, wdt_ref, cw_ref, cb_ref, dtb_ref,
                     z_ref, xbc_ref, dt_ref, buf_ref):
    i = pl.program_id(1)
    tm = x_ref.shape[1]
    @pl.when(i == 0)
    def _():
        buf_ref[0:SUBLANES, :] = jnp.zeros((SUBLANES, buf_ref.shape[1]), F32)

    @pl.when(i > 0)
    def _():
        buf_ref[0:SUBLANES, :] = buf_ref[tm:tm + SUBLANES, :]

    h = _rms_rows(x_ref[0], nw_ref[...]).astype(BF16)

    for c in range(wz_ref.shape[0]):
        z_ref[0, :, c * SSD_NCHUNK:(c + 1) * SSD_NCHUNK] = _dot(h, wz_ref[c])

    for c in range(wx_ref.shape[0]):
        cols = slice(c * SSD_NCHUNK, (c + 1) * SSD_NCHUNK)
        buf_ref[SUBLANES:SUBLANES + tm, cols] = _dot(h, wx_ref[c])
        acc = cb_ref[:, cols] + buf_ref[SUBLANES:SUBLANES + tm, cols] * cw_ref[SSD_CONV - 1:SSD_CONV, cols]
        for k in range(SSD_CONV - 1):
            off = SUBLANES - (SSD_CONV - 1) + k
            acc += buf_ref[off:off + tm, cols] * cw_ref[k:k + 1, cols]
        xbc_ref[0, :, cols] = _silu(acc)

    dt_ref[0] = jax.nn.softplus(_dot(h, wdt_ref[...]) + dtb_ref[...])


def _ssd_proj(x, nw, wz, wx, wdt, cw, cb, dtb, tm):
    B, S, D = x.shape
    consts = (nw, wz, wx, wdt, cw, cb, dtb)
    return pl.pallas_call(
        _ssd_proj_kernel,
        grid=(B, S // tm),
        in_specs=[pl.BlockSpec((1, tm, D), lambda b, i: (b, i, 0))] + [_const_spec(c.shape) for c in consts],
        out_specs=[
            pl.BlockSpec((1, tm, SSD_D_INNER), lambda b, i: (b, i, 0)),
            pl.BlockSpec((1, tm, SSD_CONV_DIM), lambda b, i: (b, i, 0)),
            pl.BlockSpec((1, tm, LANES), lambda b, i: (b, i, 0)),
        ],
        out_shape=[
            jax.ShapeDtypeStruct((B, S, SSD_D_INNER), F32),
            jax.ShapeDtypeStruct((B, S, SSD_CONV_DIM), F32),
            jax.ShapeDtypeStruct((B, S, LANES), F32),
        ],
        scratch_shapes=[pltpu.VMEM((tm + SUBLANES, SSD_CONV_DIM), F32)],
        compiler_params=_compiler_params(("parallel", "arbitrary")),
        name="ssd_proj",
    )(x, *consts)


SCAN_CHUNKS_PER_STEP = 4


def _ssd_scan_kernel(*refs):
    for inner in range(SCAN_CHUNKS_PER_STEP):
        _ssd_scan_chunk(inner, *refs)


def _ssd_scan_chunk(inner, xbc_ref, dt_ref, z_ref, alog_ref, dskip_ref, nw_ref, tril_ref, hexp_ref,
                    y_ref, state_ref, acs_ref, acst_ref, dtt_ref, e3_ref, wt_ref):
    L = SSD_CHUNK
    P = SSD_HEAD_DIM
    N = SSD_STATE
    hpg = SSD_HEADS // SSD_GROUPS
    gw = hpg * P
    chunk = pl.program_id(1) * SCAN_CHUNKS_PER_STEP + inner
    last_chunk = pl.num_programs(1) * SCAN_CHUNKS_PER_STEP - 1
    rows = slice(inner * L, (inner + 1) * L)

    def chunk_decays(ci):
        dt = dt_ref[0, pl.ds(pl.multiple_of(ci * L, L), L), :]
        a = dt * (-jnp.exp(alog_ref[...]))
        tril = tril_ref[...]
        ah, am, al = _split3(a)
        acs = _dot(tril, ah) + _dot(tril, am) + _dot(tril, al)
        acs_t = acs.T
        dt_t = dt.T
        acs_ref[...] = acs
        acst_ref[...] = acs_t
        dtt_ref[...] = dt_t
        for n, term in enumerate(_split3(jnp.exp(acs))):
            e3_ref[n] = term
        wt_ref[...] = jnp.exp(acs_t[:, L - 1:L] - acs_t) * dt_t

    if inner == 0:
        @pl.when(chunk == 0)
        def _():
            state_ref[...] = jnp.zeros(state_ref.shape, F32)
            chunk_decays(0)

    acs, acs_t, dt_t, w_t = acs_ref[...], acst_ref[...], dtt_ref[...], wt_ref[...]
    eh, em, el = e3_ref[0], e3_ref[1], e3_ref[2]
    causal = (lax.broadcasted_iota(jnp.int32, (L, L), 0) >= lax.broadcasted_iota(jnp.int32, (L, L), 1))
    head_of_lane = lax.broadcasted_iota(jnp.int32, (L, gw), 1) // P

    for g in range(SSD_GROUPS):
        b_off = SSD_D_INNER + g * N
        c_off = SSD_D_INNER + SSD_GROUPS * N + g * N
        bm = xbc_ref[0, rows, b_off:b_off + N]
        cm = xbc_ref[0, rows, c_off:c_off + N].astype(BF16)
        cb = _dot_nt(cm, bm.astype(BF16))
        bm_t = bm.T
        xg = xbc_ref[0, rows, g * gw:(g + 1) * gw]
        hexp = hexp_ref[g]
        e_exp = _dot(eh, hexp) + _dot(em, hexp) + _dot(el, hexp)
        st = state_ref[g]
        y = _dot(cm, st.astype(BF16)) * e_exp
        st = st * e_exp[L - 1:L, :]
        for j in range(hpg):
            hd = g * hpg + j
            xj = jnp.where(head_of_lane == j, xg, 0.0).astype(BF16)
            diff = acs[:, hd:hd + 1] - acs_t[hd:hd + 1, :]
            decay = jnp.exp(jnp.where(causal, diff, -jnp.inf))
            y = y + _dot((cb * decay * dt_t[hd:hd + 1, :]).astype(BF16), xj)
            st = st + _dot((bm_t * w_t[hd:hd + 1, :]).astype(BF16), xj)
        state_ref[g] = st
        cols = slice(g * gw, (g + 1) * gw)
        zg = z_ref[0, rows, cols]
        y = (y + xg * dskip_ref[:, cols]) * _silu(zg)
        y_ref[0, rows, cols] = _rms_rows(y, nw_ref[:, cols]).astype(BF16)

    chunk_decays(jnp.minimum(chunk + 1, last_chunk))


def _ssd_scan(xbc, dt, z, alog, dskip, nw, tril, hexp):
    B, S, _ = xbc.shape
    L = SSD_CHUNK
    consts = (alog, dskip, nw, tril, hexp)
    step_rows = SCAN_CHUNKS_PER_STEP * L
    return pl.pallas_call(
        _ssd_scan_kernel,
        grid=(B, S // step_rows),
        in_specs=[
            pl.BlockSpec((1, step_rows, SSD_CONV_DIM), lambda b, c: (b, c, 0)),
            pl.BlockSpec((1, S, LANES), lambda b, c: (b, 0, 0)),
            pl.BlockSpec((1, step_rows, SSD_D_INNER), lambda b, c: (b, c, 0)),
        ] + [_const_spec(c.shape) for c in consts],
        out_specs=pl.BlockSpec((1, step_rows, SSD_D_INNER), lambda b, c: (b, c, 0)),
        out_shape=jax.ShapeDtypeStruct((B, S, SSD_D_INNER), BF16),
        scratch_shapes=[
            pltpu.VMEM((SSD_GROUPS, SSD_STATE, SSD_D_INNER // SSD_GROUPS), F32),
            pltpu.VMEM((L, LANES), F32),
            pltpu.VMEM((LANES, L), F32),
            pltpu.VMEM((LANES, L), F32),
            pltpu.VMEM((3, L, LANES), BF16),
            pltpu.VMEM((LANES, L), F32),
        ],
        compiler_params=_compiler_params(("parallel", "arbitrary")),
        name="ssd_scan",
    )(xbc, dt, z, *consts)


def _rel_bucket_np(dist):
    max_exact = REL_BUCKETS // 2
    d = np.maximum(dist, 0)
    df = np.maximum(d, 1).astype(np.float32)
    large = max_exact + (np.log(df / np.float32(max_exact)) / np.float32(math.log(REL_MAX_DIST / max_exact))
                         * np.float32(REL_BUCKETS - max_exact)).astype(np.int32)
    large = np.minimum(large, REL_BUCKETS - 1)
    return np.where(d < max_exact, d, large).astype(np.int32)


@functools.lru_cache(maxsize=None)
def _position_tables(S):
    n_cmp = (S - CMP_BLOCK) // CMP_STRIDE + 1
    n_sel = S // SEL_BLOCK

    masked_code = REL_BUCKETS

    def code(dist, ok):
        return np.where(ok, _rel_bucket_np(dist), masked_code).astype(np.int32).reshape(1, -1)

    n = np.arange(N_CMP_PAD)[None, :, None]
    t = np.arange(S // TQC)[:, None, None] * TQC + np.arange(TQC)[None, None, :]
    dist = t - (n * CMP_STRIDE + CMP_BLOCK - 1)
    cmp_code = code(dist, (dist >= 0) & (n < n_cmp))
    assert TQA == TKA and (N_SEL_VARIANTS - 1) * TQA - (TKA - 1) >= REL_MAX_DIST
    ti = np.arange(TQA)[None, :]
    j = np.arange(TKA)[:, None]
    tile_dist = [TQA * behind + ti - j for behind in range(N_WIN_TILES)]
    far = np.full((TKA, TQA), 2 * REL_MAX_DIST)
    sel_dist = np.stack(tile_dist[:N_SEL_VARIANTS - 1] + [far])
    win_dist = np.stack(tile_dist)
    attn_code = np.concatenate([code(sel_dist, sel_dist >= 0),
                                code(win_dist, (win_dist >= 0) & (win_dist < WINDOW)),
                                np.full((1, TKA * TQA), masked_code, np.int32)],
                               axis=1)
    jb = np.arange(N_SEL_PAD)[:, None]
    nn = np.arange(N_CMP_PAD)[None, :]
    overlap = ((nn * CMP_STRIDE < jb * SEL_BLOCK + SEL_BLOCK) & (nn * CMP_STRIDE + CMP_BLOCK > jb * SEL_BLOCK)
               & (nn < n_cmp) & (jb < n_sel)).astype(np.float32)
    key_block = (np.arange(S)[:, None] // SEL_BLOCK == np.arange(HEAD_DIM)[None, :]).astype(np.float32)
    hpg = SSD_HEADS // SSD_GROUPS
    lane_head = np.arange(hpg * SSD_HEAD_DIM)[None, None, :] // SSD_HEAD_DIM
    head_expand = (np.arange(LANES)[None, :, None]
                   == np.arange(SSD_GROUPS)[:, None, None] * hpg + lane_head).astype(np.float32)
    return cmp_code, attn_code, overlap, key_block, head_expand


def kernel(x, norm_mix_w, norm_mlp_w, rel_table, nsa_w_in, nsa_q_gain, nsa_k_gain, cmp_pe_k, cmp_w1_k, cmp_w2_k, cmp_pe_v, cmp_w1_v, cmp_w2_v, nsa_w_out, ssd_w_in, ssd_conv_w, ssd_conv_b, ssd_dt_bias, ssd_a_log, ssd_d, ssd_norm_w, ssd_w_out, mlp_w_up, mlp_w_down):
    B, S, D = x.shape
    T = B * S
    G, R = NSA_GROUPS, NSA_REP
    cmp_code, attn_code, overlap, key_block, head_expand = _position_tables(S)

    w_in = nsa_w_in[0].astype(BF16)
    kv0 = NSA_Q_DIM
    kv_cols = lambda c: w_in[:, kv0 + c * NSA_KV_DIM:kv0 + (c + 1) * NSA_KV_DIM]
    wq = w_in[:, :NSA_Q_DIM]
    wk = jnp.stack([kv_cols(0), kv_cols(1), kv_cols(2), kv_cols(4)])
    wvt = jnp.stack([kv_cols(3).T, kv_cols(5).T])
    wg = w_in[:, kv0 + 6 * NSA_KV_DIM:].reshape(D, 3, G, R).transpose(0, 2, 1, 3).reshape(D, G, 3 * R)
    wgt = jnp.pad(wg, ((0, 0), (0, 0), (0, 16 - 3 * R))).reshape(D, G * 16).T
    qg = jnp.tile(nsa_q_gain[0], 4)[None, :]
    kg = jnp.tile(nsa_k_gain[0], 4)[None, :]
    seg = jnp.asarray(np.kron(np.eye(4, dtype=np.float32), np.ones((HEAD_DIM, HEAD_DIM), np.float32)), BF16)
    q, kc, vc, k_sw, vt_sw, gates = _nsa_proj(
        x, jnp.asarray(key_block, BF16), norm_mix_w[0][None, :], wq, wk, wvt, wgt, qg, kg, seg, tm=512)

    kcmp, vcmpt = _nsa_compress(
        kc, vc, cmp_pe_k[0], cmp_pe_v[0], cmp_w1_k[0].astype(BF16), cmp_w1_v[0].astype(BF16),
        cmp_w2_k[0].astype(BF16), cmp_w2_v[0].astype(BF16).T, nsa_k_gain[0][None, :])

    by_code = jnp.concatenate([rel_table.T * LOG2E, jnp.full((NSA_HEADS, 1), MASKED, F32),
                               jnp.zeros((NSA_HEADS, BIAS_CODES - REL_BUCKETS - 1), F32)], axis=1)
    table3 = jnp.concatenate(_split3(by_code), axis=0)
    bcmp = _bias_table(table3, jnp.asarray(cmp_code)).reshape(NSA_HEADS, S // TQC, N_CMP_PAD, TQC)
    battn = _bias_table(table3, jnp.asarray(attn_code)).reshape(NSA_HEADS, -1, TKA, TQA)
    o_cmp, q_aug = _nsa_cmp_select(q, gates, kcmp, vcmpt, bcmp, jnp.asarray(overlap, BF16))
    o = _nsa_attn(q_aug, gates, k_sw, vt_sw, o_cmp, battn)

    def mlp_weights(li):
        return mlp_w_up[li].astype(BF16), mlp_w_down[li].astype(BF16)

    wup, wdn = mlp_weights(0)
    x2 = _mix_out_mlp(x.reshape(T, D), o.reshape(T, NSA_Q_DIM), nsa_w_out[0].astype(BF16),
                      norm_mlp_w[0][None, :], wup, wdn, tm=512)

    w_in = ssd_w_in[0]
    def col_chunks(w):
        return w.reshape(D, -1, SSD_NCHUNK).transpose(1, 0, 2).astype(BF16)

    wz = col_chunks(w_in[:, :SSD_D_INNER])
    wx = col_chunks(w_in[:, SSD_D_INNER:SSD_D_INNER + SSD_CONV_DIM])
    lane_pad = LANES - SSD_HEADS
    wdt = jnp.pad(w_in[:, SSD_D_INNER + SSD_CONV_DIM:], ((0, 0), (0, lane_pad))).astype(BF16)
    pad_heads = lambda v: jnp.pad(v, (0, lane_pad))[None, :]
    z, xbc, dt = _ssd_proj(x2.reshape(B, S, D), norm_mix_w[1][None, :], wz, wx, wdt,
                           ssd_conv_w[0], ssd_conv_b[0][None, :], pad_heads(ssd_dt_bias[0]), tm=512)
    tril = jnp.asarray(np.tril(np.ones((SSD_CHUNK, SSD_CHUNK), np.float32)), BF16)
    dskip = jnp.repeat(ssd_d[0], SSD_HEAD_DIM)[None, :]
    y = _ssd_scan(xbc, dt, z, pad_heads(ssd_a_log[0]), dskip, ssd_norm_w[0][None, :], tril,
                  jnp.asarray(head_expand, BF16))

    wup, wdn = mlp_weights(1)
    x4 = _mix_out_mlp(x2, y.reshape(T, SSD_D_INNER), ssd_w_out[0].astype(BF16),
                      norm_mlp_w[1][None, :], wup, wdn, tm=512)
    return x4.reshape(B, S, D)
```
